```python
import jax, jax.numpy as jnp
from jax import lax
import numpy as np

D_MODEL = 2048
BATCH = 8
SEQ = 4096
DEPTH = 4

N_MIXERS = 2
N_HEADS = 16
HEAD_DIM = D_MODEL // N_HEADS
D_FF = 4 * D_MODEL
CONV_WIDTH = 3
Q_BLOCK = 128
N_ATTN_LAYERS = (DEPTH + 1) // 2
N_CONV_LAYERS = DEPTH // 2
N_MOD = 6
DEEPNORM_ALPHA = (2.0 * DEPTH) ** 0.25
DEEPNORM_BETA = (8.0 * DEPTH) ** -0.25
LN_EPS = 1e-5
MOD_INIT_SCALE = 0.1

kernel_name = "hybrid_stickbreak_shortconv_deepnorm_adaln"


def layer_norm(x, g, b):
    xf = x.astype(jnp.float32)
    mu = jnp.mean(xf, axis=-1, keepdims=True)
    xc = xf - mu
    var = jnp.mean(xc * xc, axis=-1, keepdims=True)
    y = xc * lax.rsqrt(var + LN_EPS)
    return (y * g.astype(jnp.float32) + b.astype(jnp.float32)).astype(x.dtype)


def stick_breaking_attention(h, w_qkv, w_o):
    bsz, seq, _ = h.shape
    qkv = h @ w_qkv
    q, k, v = jnp.split(qkv, 3, axis=-1)
    q = q.reshape(bsz, seq, N_HEADS, HEAD_DIM)
    k = k.reshape(bsz, seq, N_HEADS, HEAD_DIM)
    v = v.reshape(bsz, seq, N_HEADS, HEAD_DIM)
    scale = HEAD_DIM ** -0.5
    outs = []
    for blk in range(seq // Q_BLOCK):
        start = blk * Q_BLOCK
        end = start + Q_BLOCK
        qb = q[:, start:end].astype(jnp.float32)
        kb = k[:, :end].astype(jnp.float32)
        vb = v[:, :end].astype(jnp.float32)
        z = jnp.einsum('bqhd,bkhd->bhqk', qb, kb) * scale
        t_idx = start + jnp.arange(Q_BLOCK)[:, None]
        s_idx = jnp.arange(end)[None, :]
        causal = s_idx < t_idx
        log_beta = jax.nn.log_sigmoid(z)
        log_1m = jnp.where(causal, jax.nn.log_sigmoid(-z), 0.0)
        log_stick = lax.cumsum(log_1m, axis=3, reverse=True) - log_1m
        a = jnp.where(causal, jnp.exp(log_beta + log_stick), 0.0)
        outs.append(jnp.einsum('bhqk,bkhd->bqhd', a, vb))
    o = jnp.concatenate(outs, axis=1).reshape(bsz, seq, D_MODEL).astype(h.dtype)
    return o @ w_o


def short_conv_mixer(h, w_in, conv_w, conv_b, w_out):
    bcu = h @ w_in
    b_gate, c_gate, u = jnp.split(bcu, 3, axis=-1)
    gated = c_gate * u
    y = lax.conv_general_dilated(
        gated, conv_w[:, None, :].astype(gated.dtype),
        window_strides=(1,), padding=[(CONV_WIDTH - 1, 0)],
        dimension_numbers=('NWC', 'WIO', 'NWC'),
        feature_group_count=D_MODEL) + conv_b
    return (b_gate * y) @ w_out


def sq_relu_mlp(h, w1, b1, w2, b2):
    return jnp.square(jax.nn.relu(h @ w1 + b1)) @ w2 + b2


def _fwd_setup_inputs(seed: int = 0) -> dict:
    key = jax.random.key(seed)
    ks = jax.random.split(key, 20)
    d, f = D_MODEL, D_FF
    nrm = jax.random.normal
    x = nrm(ks[0], (BATCH, SEQ, d), jnp.float32)
    c = nrm(ks[1], (BATCH, d), jnp.float32)
    mod_w = nrm(ks[2], (DEPTH, d, N_MOD * d), jnp.float32) * (d ** -0.5) * MOD_INIT_SCALE
    mod_b = nrm(ks[3], (DEPTH, N_MOD * d), jnp.float32) * 0.01
    ln_g = 1.0 + 0.02 * nrm(ks[4], (DEPTH, 2, d), jnp.float32)
    ln_b = 0.02 * nrm(ks[5], (DEPTH, 2, d), jnp.float32)
    v_col_scale = jnp.concatenate([jnp.ones((2 * d,), jnp.float32),
                                   jnp.full((d,), DEEPNORM_BETA, jnp.float32)])
    attn_w_qkv = nrm(ks[6], (N_ATTN_LAYERS, d, 3 * d), jnp.float32) * (d ** -0.5) * v_col_scale
    attn_w_o = nrm(ks[7], (N_ATTN_LAYERS, d, d), jnp.float32) * (d ** -0.5) * DEEPNORM_BETA
    conv_w_in = nrm(ks[8], (N_CONV_LAYERS, d, 3 * d), jnp.float32) * (d ** -0.5)
    conv_w = nrm(ks[9], (N_CONV_LAYERS, CONV_WIDTH, d), jnp.float32) * (CONV_WIDTH ** -0.5)
    conv_b = nrm(ks[10], (N_CONV_LAYERS, d), jnp.float32) * 0.01
    conv_w_out = nrm(ks[11], (N_CONV_LAYERS, d, d), jnp.float32) * (d ** -0.5) * DEEPNORM_BETA
    mlp_w1 = nrm(ks[12], (DEPTH, d, f), jnp.float32) * (d ** -0.5) * DEEPNORM_BETA
    mlp_b1 = nrm(ks[13], (DEPTH, f), jnp.float32) * 0.01
    mlp_w2 = nrm(ks[14], (DEPTH, f, d), jnp.float32) * (f ** -0.5) * DEEPNORM_BETA
    mlp_b2 = nrm(ks[15], (DEPTH, d), jnp.float32) * 0.01
    return {"x": x, "c": c, "mod_w": mod_w, "mod_b": mod_b, "ln_g": ln_g, "ln_b": ln_b,
            "attn_w_qkv": attn_w_qkv, "attn_w_o": attn_w_o,
            "conv_w_in": conv_w_in, "conv_w": conv_w, "conv_b": conv_b, "conv_w_out": conv_w_out,
            "mlp_w1": mlp_w1, "mlp_b1": mlp_b1, "mlp_w2": mlp_w2, "mlp_b2": mlp_b2}


def _fwd_reference(x, c, mod_w, mod_b, ln_g, ln_b, attn_w_qkv, attn_w_o,
              conv_w_in, conv_w, conv_b, conv_w_out, mlp_w1, mlp_b1, mlp_w2, mlp_b2):
    cond = jax.nn.silu(c)
    for i in range(DEPTH):
        mod = (cond @ mod_w[i] + mod_b[i])[:, None, :]
        sh_m, sc_m, g_m, sh_f, sc_f, g_f = jnp.split(mod, N_MOD, axis=-1)
        h = x * (1.0 + sc_m) + sh_m
        if i % N_MIXERS == 0:
            j = i // N_MIXERS
            y = stick_breaking_attention(h, attn_w_qkv[j], attn_w_o[j])
        else:
            j = i // N_MIXERS
            y = short_conv_mixer(h, conv_w_in[j], conv_w[j], conv_b[j], conv_w_out[j])
        x = layer_norm(DEEPNORM_ALPHA * x + (1.0 + g_m) * y, ln_g[i, 0], ln_b[i, 0])
        h = x * (1.0 + sc_f) + sh_f
        y = sq_relu_mlp(h, mlp_w1[i], mlp_b1[i], mlp_w2[i], mlp_b2[i])
        x = layer_norm(DEEPNORM_ALPHA * x + (1.0 + g_f) * y, ln_g[i, 1], ln_b[i, 1])
    return x


import jax as _jax
import jax.numpy as _jnp

TWIN_FORMAT = 'train_step'
FWD_PARAMS = ['x', 'c', 'mod_w', 'mod_b', 'ln_g', 'ln_b', 'attn_w_qkv', 'attn_w_o', 'conv_w_in', 'conv_w', 'conv_b', 'conv_w_out', 'mlp_w1', 'mlp_b1', 'mlp_w2', 'mlp_b2']
TWIN_WEIGHTS = ['mod_w', 'mod_b', 'ln_g', 'ln_b', 'attn_w_qkv', 'attn_w_o', 'conv_w_in', 'conv_w', 'conv_b', 'conv_w_out', 'mlp_w1', 'mlp_b1', 'mlp_w2', 'mlp_b2']
TWIN_DIFF_INPUT = 'x'
TWIN_INPUTS = ['x', 'c', 'mod_w', 'mod_b', 'ln_g', 'ln_b', 'attn_w_qkv', 'attn_w_o', 'conv_w_in', 'conv_w', 'conv_b', 'conv_w_out', 'mlp_w1', 'mlp_b1', 'mlp_w2', 'mlp_b2', 'loss_target', 'm_mod_w', 'm_mod_b', 'm_ln_g', 'm_ln_b', 'm_attn_w_qkv', 'm_attn_w_o', 'm_conv_w_in', 'm_conv_w', 'm_conv_b', 'm_conv_w_out', 'm_mlp_w1', 'm_mlp_b1', 'm_mlp_w2', 'm_mlp_b2', 'v_mod_w', 'v_mod_b', 'v_ln_g', 'v_ln_b', 'v_attn_w_qkv', 'v_attn_w_o', 'v_conv_w_in', 'v_conv_w', 'v_conv_b', 'v_conv_w_out', 'v_mlp_w1', 'v_mlp_b1', 'v_mlp_w2', 'v_mlp_b2']
TWIN_OUTPUTS = ['loss', 'grad_x', 'grad_mod_w', 'grad_mod_b', 'grad_ln_g', 'grad_ln_b', 'grad_attn_w_qkv', 'grad_attn_w_o', 'grad_conv_w_in', 'grad_conv_w', 'grad_conv_b', 'grad_conv_w_out', 'grad_mlp_w1', 'grad_mlp_b1', 'grad_mlp_w2', 'grad_mlp_b2', 'delta_mod_w', 'delta_mod_b', 'delta_ln_g', 'delta_ln_b', 'delta_attn_w_qkv', 'delta_attn_w_o', 'delta_conv_w_in', 'delta_conv_w', 'delta_conv_b', 'delta_conv_w_out', 'delta_mlp_w1', 'delta_mlp_b1', 'delta_mlp_w2', 'delta_mlp_b2', 'new_m_mod_w', 'new_m_mod_b', 'new_m_ln_g', 'new_m_ln_b', 'new_m_attn_w_qkv', 'new_m_attn_w_o', 'new_m_conv_w_in', 'new_m_conv_w', 'new_m_conv_b', 'new_m_conv_w_out', 'new_m_mlp_w1', 'new_m_mlp_b1', 'new_m_mlp_w2', 'new_m_mlp_b2', 'new_v_mod_w', 'new_v_mod_b', 'new_v_ln_g', 'new_v_ln_b', 'new_v_attn_w_qkv', 'new_v_attn_w_o', 'new_v_conv_w_in', 'new_v_conv_w', 'new_v_conv_b', 'new_v_conv_w_out', 'new_v_mlp_w1', 'new_v_mlp_b1', 'new_v_mlp_w2', 'new_v_mlp_b2']
TWIN_LEAF_KINDS = {'loss': 'loss', 'grad_x': 'grad_x', 'grad_mod_w': 'grad_w', 'grad_mod_b': 'grad_w', 'grad_ln_g': 'grad_w', 'grad_ln_b': 'grad_w', 'grad_attn_w_qkv': 'grad_w', 'grad_attn_w_o': 'grad_w', 'grad_conv_w_in': 'grad_w', 'grad_conv_w': 'grad_w', 'grad_conv_b': 'grad_w', 'grad_conv_w_out': 'grad_w', 'grad_mlp_w1': 'grad_w', 'grad_mlp_b1': 'grad_w', 'grad_mlp_w2': 'grad_w', 'grad_mlp_b2': 'grad_w', 'delta_mod_w': 'delta_w', 'delta_mod_b': 'delta_w', 'delta_ln_g': 'delta_w', 'delta_ln_b': 'delta_w', 'delta_attn_w_qkv': 'delta_w', 'delta_attn_w_o': 'delta_w', 'delta_conv_w_in': 'delta_w', 'delta_conv_w': 'delta_w', 'delta_conv_b': 'delta_w', 'delta_conv_w_out': 'delta_w', 'delta_mlp_w1': 'delta_w', 'delta_mlp_b1': 'delta_w', 'delta_mlp_w2': 'delta_w', 'delta_mlp_b2': 'delta_w', 'new_m_mod_w': 'new_m', 'new_m_mod_b': 'new_m', 'new_m_ln_g': 'new_m', 'new_m_ln_b': 'new_m', 'new_m_attn_w_qkv': 'new_m', 'new_m_attn_w_o': 'new_m', 'new_m_conv_w_in': 'new_m', 'new_m_conv_w': 'new_m', 'new_m_conv_b': 'new_m', 'new_m_conv_w_out': 'new_m', 'new_m_mlp_w1': 'new_m', 'new_m_mlp_b1': 'new_m', 'new_m_mlp_w2': 'new_m', 'new_m_mlp_b2': 'new_m', 'new_v_mod_w': 'new_v', 'new_v_mod_b': 'new_v', 'new_v_ln_g': 'new_v', 'new_v_ln_b': 'new_v', 'new_v_attn_w_qkv': 'new_v', 'new_v_attn_w_o': 'new_v', 'new_v_conv_w_in': 'new_v', 'new_v_conv_w': 'new_v', 'new_v_conv_b': 'new_v', 'new_v_conv_w_out': 'new_v', 'new_v_mlp_w1': 'new_v', 'new_v_mlp_b1': 'new_v', 'new_v_mlp_w2': 'new_v', 'new_v_mlp_b2': 'new_v'}


def _forward(args):
    return _fwd_reference(*[args[k] for k in FWD_PARAMS])


def _output_shape():
    def fwd():
        inp = _fwd_setup_inputs(0)
        return _fwd_reference(*[inp[k] for k in FWD_PARAMS])
    out = _jax.eval_shape(fwd)
    return out.shape, out.dtype

N_MICROBATCH = 1
ADAM_LR = 0.001
ADAM_B1 = 0.9
ADAM_B2 = 0.999
ADAM_EPS = 1e-08
ADAM_WD = 0.01
ADAM_STEP = 10
PER_EXAMPLE_BATCH_AXIS = {'x': 0, 'c': 0, 'loss_target': 0}
SHARED_INPUTS = []
_WEIGHT_DTYPES = {'mod_w': _jnp.float32, 'mod_b': _jnp.float32, 'ln_g': _jnp.float32, 'ln_b': _jnp.float32, 'attn_w_qkv': _jnp.float32, 'attn_w_o': _jnp.float32, 'conv_w_in': _jnp.float32, 'conv_w': _jnp.float32, 'conv_b': _jnp.float32, 'conv_w_out': _jnp.float32, 'mlp_w1': _jnp.float32, 'mlp_b1': _jnp.float32, 'mlp_w2': _jnp.float32, 'mlp_b2': _jnp.float32}
MOMENT_SCALE = {'mod_w': 1.111910e-02, 'mod_b': 2.065559e-02, 'ln_g': 5.700577e+00, 'ln_b': 4.455970e-01, 'attn_w_qkv': 9.502989e-03, 'attn_w_o': 1.587175e-02, 'conv_w_in': 2.227811e-02, 'conv_w': 2.236148e-02, 'conv_b': 2.211020e-02, 'conv_w_out': 5.307887e-02, 'mlp_w1': 7.054860e-03, 'mlp_b1': 1.200464e-02, 'mlp_w2': 1.688749e-02, 'mlp_b2': 1.621920e-01}


def _to_microbatches(a, axis):
    t = _jnp.moveaxis(a, axis, 0)
    t = t.reshape((N_MICROBATCH, t.shape[0] // N_MICROBATCH) + t.shape[1:])
    return _jnp.moveaxis(t, 1, axis + 1)


def setup_inputs(seed: int = 0) -> dict:
    inp = _fwd_setup_inputs(seed)
    key = _jax.random.fold_in(_jax.random.key(seed), 7919)
    shape, _ = _output_shape()
    out = dict(inp)
    out["loss_target"] = _jax.random.normal(_jax.random.fold_in(key, 0), shape, _jnp.float32)
    for i, name in enumerate(TWIN_WEIGHTS):
        w = inp[name].astype(_jnp.float32)
        if MOMENT_SCALE is None:
            s = _jnp.sqrt(_jnp.mean(_jnp.square(w)) + 1e-30)
        else:
            s = MOMENT_SCALE[name]
        km, kv = _jax.random.split(_jax.random.fold_in(key, i + 1))
        out[name] = w
        out["m_" + name] = s * _jax.random.normal(km, w.shape, _jnp.float32)
        out["v_" + name] = (s * s) * _jax.random.uniform(kv, w.shape, _jnp.float32, 0.5, 1.5)
    if N_MICROBATCH > 1:
        for name, axis in PER_EXAMPLE_BATCH_AXIS.items():
            out[name] = _to_microbatches(out[name], axis)
    return {'x': out['x'], 'c': out['c'], 'mod_w': out['mod_w'], 'mod_b': out['mod_b'], 'ln_g': out['ln_g'], 'ln_b': out['ln_b'], 'attn_w_qkv': out['attn_w_qkv'], 'attn_w_o': out['attn_w_o'], 'conv_w_in': out['conv_w_in'], 'conv_w': out['conv_w'], 'conv_b': out['conv_b'], 'conv_w_out': out['conv_w_out'], 'mlp_w1': out['mlp_w1'], 'mlp_b1': out['mlp_b1'], 'mlp_w2': out['mlp_w2'], 'mlp_b2': out['mlp_b2'], 'loss_target': out['loss_target'], 'm_mod_w': out['m_mod_w'], 'm_mod_b': out['m_mod_b'], 'm_ln_g': out['m_ln_g'], 'm_ln_b': out['m_ln_b'], 'm_attn_w_qkv': out['m_attn_w_qkv'], 'm_attn_w_o': out['m_attn_w_o'], 'm_conv_w_in': out['m_conv_w_in'], 'm_conv_w': out['m_conv_w'], 'm_conv_b': out['m_conv_b'], 'm_conv_w_out': out['m_conv_w_out'], 'm_mlp_w1': out['m_mlp_w1'], 'm_mlp_b1': out['m_mlp_b1'], 'm_mlp_w2': out['m_mlp_w2'], 'm_mlp_b2': out['m_mlp_b2'], 'v_mod_w': out['v_mod_w'], 'v_mod_b': out['v_mod_b'], 'v_ln_g': out['v_ln_g'], 'v_ln_b': out['v_ln_b'], 'v_attn_w_qkv': out['v_attn_w_qkv'], 'v_attn_w_o': out['v_attn_w_o'], 'v_conv_w_in': out['v_conv_w_in'], 'v_conv_w': out['v_conv_w'], 'v_conv_b': out['v_conv_b'], 'v_conv_w_out': out['v_conv_w_out'], 'v_mlp_w1': out['v_mlp_w1'], 'v_mlp_b1': out['v_mlp_b1'], 'v_mlp_w2': out['v_mlp_w2'], 'v_mlp_b2': out['v_mlp_b2']}


def _loss(weights, diff, rest, loss_target):
    with _jax.named_scope("forward"):
        args = {**rest, TWIN_DIFF_INPUT: diff, **{k: w.astype(_WEIGHT_DTYPES[k]) for k, w in weights.items()}}
        y = _forward(args)
    with _jax.named_scope("loss_head"):
        err = _jnp.square(y.astype(_jnp.float32) - loss_target)
        return 0.5 * _jnp.sum(_jnp.mean(err, axis=-1)) if err.ndim else 0.5 * err


def _adamw(w, g, m, v):
    m = ADAM_B1 * m + (1.0 - ADAM_B1) * g
    v = ADAM_B2 * v + (1.0 - ADAM_B2) * _jnp.square(g)
    m_hat = m / (1.0 - ADAM_B1 ** ADAM_STEP)
    v_hat = v / (1.0 - ADAM_B2 ** ADAM_STEP)
    delta = -ADAM_LR * (m_hat / (_jnp.sqrt(v_hat) + ADAM_EPS) + ADAM_WD * w)
    return delta, m, v


def reference(x, c, mod_w, mod_b, ln_g, ln_b, attn_w_qkv, attn_w_o, conv_w_in, conv_w, conv_b, conv_w_out, mlp_w1, mlp_b1, mlp_w2, mlp_b2, loss_target, m_mod_w, m_mod_b, m_ln_g, m_ln_b, m_attn_w_qkv, m_attn_w_o, m_conv_w_in, m_conv_w, m_conv_b, m_conv_w_out, m_mlp_w1, m_mlp_b1, m_mlp_w2, m_mlp_b2, v_mod_w, v_mod_b, v_ln_g, v_ln_b, v_attn_w_qkv, v_attn_w_o, v_conv_w_in, v_conv_w, v_conv_b, v_conv_w_out, v_mlp_w1, v_mlp_b1, v_mlp_w2, v_mlp_b2):
    given = dict(x=x, c=c, mod_w=mod_w, mod_b=mod_b, ln_g=ln_g, ln_b=ln_b, attn_w_qkv=attn_w_qkv, attn_w_o=attn_w_o, conv_w_in=conv_w_in, conv_w=conv_w, conv_b=conv_b, conv_w_out=conv_w_out, mlp_w1=mlp_w1, mlp_b1=mlp_b1, mlp_w2=mlp_w2, mlp_b2=mlp_b2, loss_target=loss_target, m_mod_w=m_mod_w, m_mod_b=m_mod_b, m_ln_g=m_ln_g, m_ln_b=m_ln_b, m_attn_w_qkv=m_attn_w_qkv, m_attn_w_o=m_attn_w_o, m_conv_w_in=m_conv_w_in, m_conv_w=m_conv_w, m_conv_b=m_conv_b, m_conv_w_out=m_conv_w_out, m_mlp_w1=m_mlp_w1, m_mlp_b1=m_mlp_b1, m_mlp_w2=m_mlp_w2, m_mlp_b2=m_mlp_b2, v_mod_w=v_mod_w, v_mod_b=v_mod_b, v_ln_g=v_ln_g, v_ln_b=v_ln_b, v_attn_w_qkv=v_attn_w_qkv, v_attn_w_o=v_attn_w_o, v_conv_w_in=v_conv_w_in, v_conv_w=v_conv_w, v_conv_b=v_conv_b, v_conv_w_out=v_conv_w_out, v_mlp_w1=v_mlp_w1, v_mlp_b1=v_mlp_b1, v_mlp_w2=v_mlp_w2, v_mlp_b2=v_mlp_b2)
    weights = {n: given[n] for n in TWIN_WEIGHTS}
    shared = {n: given[n] for n in SHARED_INPUTS}
    per_example = {n: given[n] for n in ['x', 'c']}
    grad_fn = _jax.value_and_grad(_loss, argnums=(0, 1))

    def one_microbatch(ex, loss_target):
        ex = dict(ex)
        diff = ex.pop(TWIN_DIFF_INPUT)
        return grad_fn(weights, diff, {**shared, **ex}, loss_target)

    if N_MICROBATCH == 1:
        loss, (grad_w, grad_x) = one_microbatch(per_example, given["loss_target"])
    else:
        def body(carry, xs):
            loss_sum, grad_sum = carry
            l_k, (gw_k, gx_k) = one_microbatch(xs[0], xs[1])
            with _jax.named_scope("update"):
                return (loss_sum + l_k, _jax.tree.map(_jnp.add, grad_sum, gw_k)), gx_k

        init = (_jnp.zeros((), _jnp.float32), _jax.tree.map(_jnp.zeros_like, weights))
        (loss, grad_w), grad_x = _jax.lax.scan(body, init, (per_example, given["loss_target"]))
    with _jax.named_scope("update"):
        delta_w, new_m, new_v = {}, {}, {}
        for n in TWIN_WEIGHTS:
            delta_w[n], new_m[n], new_v[n] = _adamw(weights[n], grad_w[n], given["m_" + n], given["v_" + n])
    return (loss, grad_x, *[grad_w[n] for n in TWIN_WEIGHTS], *[delta_w[n] for n in TWIN_WEIGHTS],
            *[new_m[n] for n in TWIN_WEIGHTS], *[new_v[n] for n in TWIN_WEIGHTS])
```

```python
import functools

import jax
import jax.numpy as jnp
from jax import lax
from jax.experimental import pallas as pl
from jax.experimental.pallas import tpu as pltpu

F32 = jnp.float32
BF16 = jnp.bfloat16
N_DEV = 8
HEAD_DIM = 128
LANES = 128
DEEPNORM_ALPHA = 8.0 ** 0.25
LN_EPS = 1e-5
ADAM_LR = 0.001
ADAM_B1 = 0.9
ADAM_B2 = 0.999
ADAM_EPS = 1e-08
ADAM_WD = 0.01
ADAM_STEP = 10
VMEM_LIMIT_BYTES = 52 * 1024 * 1024
MESH = pl.DeviceIdType.MESH
HBM_SPEC = pl.BlockSpec(memory_space=pltpu.HBM)
ANY_SPEC = pl.BlockSpec(memory_space=pl.ANY)


def _pcall(body, **kw):
    return pl.pallas_call(body, **kw)


def _cp(sem=None):
    return pltpu.CompilerParams(dimension_semantics=sem, vmem_limit_bytes=VMEM_LIMIT_BYTES)


def _tile(n, pref):
    t = min(n, pref)
    assert n % t == 0, (n, pref)
    return t


def _sds(shape, dtype):
    return jax.ShapeDtypeStruct(tuple(shape), dtype)


def _allgather(xs, name):
    n = len(xs)

    def body(*refs):
        x_refs = refs[:n]
        out_refs = refs[n:2 * n]
        send_sems, recv_sems, local_sems = refs[2 * n:]
        x, y, c = lax.axis_index("x"), lax.axis_index("y"), lax.axis_index("c")
        me, sibling = (x, y, c), (x, y, 1 - c)
        chips = [(1 - x, y), (x, 1 - y), (1 - x, 1 - y)]

        def slot(t, px, py, pc):
            return out_refs[t].at[4 * px + 2 * py + pc]

        def copy(t, k, block, to, src=None):
            return pltpu.make_async_remote_copy(
                src_ref=slot(t, *block) if src is None else src, dst_ref=slot(t, *block),
                send_sem=send_sems.at[t, k], recv_sem=recv_sems.at[t, k],
                device_id=to, device_id_type=MESH)

        mine = [pltpu.make_async_copy(x_refs[t], slot(t, *me), local_sems.at[t]) for t in range(n)]
        for cp in mine:
            cp.start()
        first = []
        for t in range(n):
            first.append(copy(t, 0, me, sibling, src=x_refs[t]))
            for j, chip in enumerate(chips):
                first.append(copy(t, 1 + j, me, (*chip, c), src=x_refs[t]))
        for cp in first:
            cp.start()
        passed = []
        for j, chip in enumerate(chips):
            for t in range(n):
                copy(t, 1 + j, (*chip, c), me).wait_recv()
                fwd = copy(t, 4 + j, (*chip, c), sibling)
                fwd.start()
                passed.append(fwd)
        for t in range(n):
            copy(t, 0, sibling, me).wait_recv()
            for j, chip in enumerate(chips):
                copy(t, 4 + j, (*chip, 1 - c), me).wait_recv()
        for cp in first + passed:
            cp.wait_send()
        for cp in mine:
            cp.wait()

    outs = _pcall(
        body, name=name,
        out_shape=[_sds((N_DEV,) + a.shape, a.dtype) for a in xs],
        in_specs=[HBM_SPEC] * n, out_specs=[HBM_SPEC] * n,
        scratch_shapes=[pltpu.SemaphoreType.DMA((n, 7)), pltpu.SemaphoreType.DMA((n, 7)),
                        pltpu.SemaphoreType.DMA((n,))],
    )(*xs)
    return list(outs)


def _rs_sibling(gs, name):
    n = len(gs)

    def body(*refs):
        g_refs = refs[:n]
        r_refs = refs[n:2 * n]
        send_sems, recv_sems = refs[2 * n:]
        x, y, c = lax.axis_index("x"), lax.axis_index("y"), lax.axis_index("c")
        me = 4 * x + 2 * y + c
        cps = []
        for t in range(n):
            for k in range(4):
                cps.append(pltpu.make_async_remote_copy(
                    src_ref=g_refs[t].at[jnp.bitwise_xor(2 * k + 1, me)], dst_ref=r_refs[t].at[k],
                    send_sem=send_sems.at[t, k], recv_sem=recv_sems.at[t, k],
                    device_id=(x, y, 1 - c), device_id_type=MESH))
        for cp in cps:
            cp.start()
        for cp in cps:
            cp.wait()

    outs = _pcall(
        body, name=name,
        out_shape=[_sds((4,) + g.shape[1:], g.dtype) for g in gs],
        in_specs=[HBM_SPEC] * n, out_specs=[HBM_SPEC] * n,
        scratch_shapes=[pltpu.SemaphoreType.DMA((n, 4)), pltpu.SemaphoreType.DMA((n, 4))],
    )(*gs)
    return list(outs)


def _rs_chips(ps, name):
    n = len(ps)

    def body(*refs):
        p_refs = refs[:n]
        r_refs = refs[n:2 * n]
        send_sems, recv_sems = refs[2 * n:]
        x, y, c = lax.axis_index("x"), lax.axis_index("y"), lax.axis_index("c")
        cps = []
        for t in range(n):
            for k in range(1, 4):
                kx, ky = k >> 1, k & 1
                to = (1 - x if kx else x, 1 - y if ky else y, c)
                cps.append(pltpu.make_async_remote_copy(
                    src_ref=p_refs[t].at[k], dst_ref=r_refs[t].at[k],
                    send_sem=send_sems.at[t, k - 1], recv_sem=recv_sems.at[t, k - 1],
                    device_id=to, device_id_type=MESH))
        for cp in cps:
            cp.start()
        for cp in cps:
            cp.wait()

    outs = _pcall(
        body, name=name,
        out_shape=[_sds(p.shape, p.dtype) for p in ps],
        in_specs=[HBM_SPEC] * n, out_specs=[HBM_SPEC] * n,
        scratch_shapes=[pltpu.SemaphoreType.DMA((n, 3)), pltpu.SemaphoreType.DMA((n, 3))],
    )(*ps)
    return list(outs)


def _rs_add(g, r1, me_arr, name):
    _, R, C = g.shape
    tr = _tile(R, 256)

    def body(me_ref, g_ref, r_ref, o_ref):
        o_ref[...] = (g_ref[...].astype(F32) + r_ref[...].astype(F32)).astype(BF16)

    grid_spec = pltpu.PrefetchScalarGridSpec(
        num_scalar_prefetch=1, grid=(4, R // tr),
        in_specs=[pl.BlockSpec((None, tr, C), lambda k, r, me: (jnp.bitwise_xor(2 * k, me[0]), r, 0)),
                  pl.BlockSpec((None, tr, C), lambda k, r, me: (k, r, 0))],
        out_specs=pl.BlockSpec((None, tr, C), lambda k, r, me: (k, r, 0)))
    return _pcall(body, name=name, grid_spec=grid_spec, out_shape=_sds((4, R, C), BF16),
                  compiler_params=_cp(("parallel", "parallel")))(me_arr, g, r1)


_DIMS = {"nn": (((1,), (0,)), ((), ())), "nt": (((1,), (1,)), ((), ())), "tn": (((0,), (0,)), ((), ()))}


def _dot(a, b, mode="nn"):
    return lax.dot_general(a, b, _DIMS[mode], preferred_element_type=F32)


def _matmul(name, mode, grid, nk, a, a_spec, b, b_spec, out_shapes, out_specs, acc_shape,
            epilogue="plain", extras=(), extra_specs=(), sem=None):
    n_extra, n_out = len(extras), len(out_shapes)

    def body(*refs):
        a_ref, b_ref = refs[0], refs[1]
        ex = refs[2:2 + n_extra]
        outs = refs[2 + n_extra:2 + n_extra + n_out]
        part = _dot(a_ref[...].astype(BF16), b_ref[...].astype(BF16), mode)

        def finish(acc):
            if epilogue == "plain":
                if n_extra:
                    acc = acc + ex[0][...]
                outs[0][...] = acc.astype(outs[0].dtype)
            elif epilogue == "relu2":
                u = acc + ex[0][...]
                outs[0][...] = u.astype(outs[0].dtype)
                outs[1][...] = jnp.square(jnp.maximum(u, 0.0)).astype(outs[1].dtype)
            else:
                du = acc * (2.0 * jnp.maximum(ex[0][...].astype(F32), 0.0))
                outs[0][...] = du.astype(outs[0].dtype)
                col = jnp.sum(du, axis=0, keepdims=True)
                inner = pl.program_id(1)

                @pl.when(inner == 0)
                def _():
                    outs[1][...] = col

                @pl.when(inner > 0)
                def _():
                    outs[1][...] += col

        if nk == 1:
            finish(part)
        else:
            acc_ref = refs[2 + n_extra + n_out]
            kidx = pl.program_id(len(grid) - 1)

            @pl.when(kidx == 0)
            def _():
                acc_ref[...] = part

            @pl.when(kidx > 0)
            def _():
                acc_ref[...] += part

            @pl.when(kidx == nk - 1)
            def _():
                finish(acc_ref[...])

    if sem is None:
        sem = ("parallel",) * (len(grid) - (1 if nk > 1 else 0)) + (("arbitrary",) if nk > 1 else ())
    outs = _pcall(
        body, name=name, grid=grid,
        in_specs=[a_spec, b_spec] + list(extra_specs), out_specs=list(out_specs),
        out_shape=list(out_shapes),
        scratch_shapes=[pltpu.VMEM(acc_shape, F32)] if nk > 1 else [],
        compiler_params=_cp(sem),
    )(a, b, *extras)
    return list(outs)


def _cast_bf16(w3, l, name):
    _, R, C = w3.shape
    tr = _tile(R, 256)

    def body(x_ref, o_ref):
        o_ref[...] = x_ref[...].astype(BF16)

    return _pcall(body, name=name, grid=(R // tr,),
                  in_specs=[pl.BlockSpec((None, tr, C), lambda r: (l, r, 0))],
                  out_specs=pl.BlockSpec((tr, C), lambda r: (r, 0)),
                  out_shape=_sds((R, C), BF16), compiler_params=_cp(("parallel",)))(w3)


def _silu_bf16(c16, name):
    def body(c_ref, o_ref):
        v = c_ref[...]
        o_ref[...] = (v / (1.0 + jnp.exp(-v))).astype(BF16)

    return _pcall(body, name=name, out_shape=_sds(c16.shape, BF16))(c16)


def _sum_devices(g, name):
    def body(g_ref, o_ref):
        acc = g_ref[0]
        for d in range(1, N_DEV):
            acc = acc + g_ref[d]
        o_ref[...] = acc

    return _pcall(body, name=name, out_shape=_sds(g.shape[1:], F32))(g)


def _modulate(name, x, sc, sh):
    S, D = x.shape
    ts = _tile(S, 256)

    def body(x_ref, sc_ref, sh_ref, h_ref):
        h_ref[...] = (x_ref[...] * (1.0 + sc_ref[...]) + sh_ref[...]).astype(BF16)

    row = pl.BlockSpec((ts, D), lambda i: (i, 0))
    vec = pl.BlockSpec((1, D), lambda i: (0, 0))
    return _pcall(body, name=name, grid=(S // ts,), in_specs=[row, vec, vec], out_specs=row,
                  out_shape=_sds((S, D), BF16), compiler_params=_cp(("parallel",)))(x, sc, sh)


def _ln_stats(r):
    mu = jnp.mean(r, axis=-1, keepdims=True)
    xc = r - mu
    var = jnp.mean(xc * xc, axis=-1, keepdims=True)
    rstd = lax.rsqrt(var + LN_EPS)
    return xc * rstd, rstd


def _ln_fwd(name, xin, y, gate, lng, lnb, sc, sh):
    S, D = xin.shape
    ts = _tile(S, 256)

    def body(x_ref, y_ref, g_ref, lg_ref, lb_ref, sc_ref, sh_ref, xo_ref, h_ref):
        r = DEEPNORM_ALPHA * x_ref[...] + (1.0 + g_ref[...]) * y_ref[...]
        xhat, _ = _ln_stats(r)
        xo = xhat * lg_ref[...] + lb_ref[...]
        xo_ref[...] = xo
        h_ref[...] = (xo * (1.0 + sc_ref[...]) + sh_ref[...]).astype(BF16)

    row = pl.BlockSpec((ts, D), lambda i: (i, 0))
    vec = pl.BlockSpec((1, D), lambda i: (0, 0))
    return _pcall(body, name=name, grid=(S // ts,), in_specs=[row, row] + [vec] * 5,
                  out_specs=[row, row], out_shape=[_sds((S, D), F32), _sds((S, D), BF16)],
                  compiler_params=_cp(("parallel",)))(xin, y, gate, lng, lnb, sc, sh)


def _boundary(name, mode, big, vecs):
    S, D = big[0].shape
    ts = _tile(S, 128)
    nb, nv = len(big), len(vecs)

    def body(*refs):
        b = [r[...] for r in refs[:nb]]
        v = [r[...] for r in refs[nb:nb + nv]]
        outs = refs[nb + nv:]
        vec_ref = outs[-1]
        i = pl.program_id(0)
        rows = [None] * 8

        def colsum(a):
            return jnp.sum(a, axis=0, keepdims=True)

        if mode == "first":
            dxr, dh, xk = b
            sc = v[0]
            outs[0][...] = dxr + dh * (1.0 + sc)
            rows[0], rows[1] = colsum(dh * xk), colsum(dh)
        else:
            if mode == "mid":
                dxr, dh, xprev, yk = b
                sc, gate, lng, lnb = v
            else:
                target, xprev, yk = b
                gate, lng, lnb = v
            r = DEEPNORM_ALPHA * xprev + (1.0 + gate) * yk
            xhat, rstd = _ln_stats(r)
            if mode == "mid":
                xk = xhat * lng + lnb
                d = dxr + dh * (1.0 + sc)
                rows[0], rows[1] = colsum(dh * xk), colsum(dh)
            else:
                err = xhat * lng + lnb - target
                d = err * (1.0 / D)
                rows[6] = jnp.zeros((1, D), F32) + jnp.sum(err * err)
            rows[2], rows[3] = colsum(d * xhat), colsum(d)
            dxhat = d * lng
            m1 = jnp.mean(dxhat, axis=-1, keepdims=True)
            m2 = jnp.mean(dxhat * xhat, axis=-1, keepdims=True)
            dr = rstd * (dxhat - m1 - xhat * m2)
            dy = (1.0 + gate) * dr
            rows[4], rows[5] = colsum(dr * yk), colsum(dy)
            outs[0][...] = dy.astype(BF16)
            outs[1][...] = DEEPNORM_ALPHA * dr
        @pl.when(i == 0)
        def _():
            vec_ref[...] = jnp.zeros_like(vec_ref)

        for j, row_j in enumerate(rows):
            if row_j is not None:
                vec_ref[j:j + 1, :] += row_j

    row = pl.BlockSpec((ts, D), lambda i: (i, 0))
    vec = pl.BlockSpec((1, D), lambda i: (0, 0))
    vec_out = pl.BlockSpec((8, D), lambda i: (0, 0))
    if mode == "first":
        out_specs, out_shape = [row, vec_out], [_sds((S, D), F32), _sds((8, D), F32)]
    else:
        out_specs = [row, row, vec_out]
        out_shape = [_sds((S, D), BF16), _sds((S, D), F32), _sds((8, D), F32)]
    return _pcall(body, name=name, grid=(S // ts,), in_specs=[row] * nb + [vec] * nv,
                  out_specs=out_specs, out_shape=out_shape,
                  compiler_params=_cp(("arbitrary",)))(*big, *vecs)


def _split_bf16(a):
    hi = a.astype(BF16)
    lo = (a - hi.astype(F32)).astype(BF16)
    return hi, lo


def _tri_dot(a, tri):
    hi, lo = _split_bf16(a)
    return _dot(hi, tri) + _dot(lo, tri)


def _logsig_parts(z):
    e = jnp.exp(-jnp.abs(z))
    sp = jnp.log(1.0 + e)
    return e, jnp.minimum(z, 0.0) - sp, -jnp.maximum(z, 0.0) - sp


def _attn_fwd(name, qkv3):
    _, S, D = qkv3.shape
    H = D // HEAD_DIM
    tq, tk = _tile(S, 256), 128
    scale = HEAD_DIM ** -0.5

    def body(q_ref, k_ref, v_ref, o_ref, t_ref):
        i = pl.program_id(1)
        q = q_ref[...]
        t_idx = i * tq + lax.broadcasted_iota(jnp.int32, (tq, tk), 0)
        s_loc = lax.broadcasted_iota(jnp.int32, (tq, tk), 1)
        later = (lax.broadcasted_iota(jnp.int32, (tk, tk), 0)
                 > lax.broadcasted_iota(jnp.int32, (tk, tk), 1)).astype(BF16)
        nkb = (i + 1) * (tq // tk)

        def step(it, carry):
            tail, acc = carry
            k0 = pl.multiple_of((nkb - 1 - it) * tk, tk)
            kb = k_ref[pl.ds(k0, tk), :]
            vb = v_ref[pl.ds(k0, tk), :]
            z = _dot(q, kb, "nt") * scale
            mask = (k0 + s_loc) < t_idx
            _, logb, log1m = _logsig_parts(z)
            lm = jnp.where(mask, log1m, 0.0)
            stick = _tri_dot(lm, later) + tail
            a = jnp.where(mask, jnp.exp(logb + stick), 0.0)
            acc = acc + _dot(a.astype(BF16), vb)
            return tail + jnp.sum(lm, axis=1, keepdims=True), acc

        tail, acc = lax.fori_loop(0, nkb, step, (jnp.zeros((tq, 1), F32), jnp.zeros((tq, HEAD_DIM), F32)))
        o_ref[...] = acc.astype(BF16)
        t_ref[...] = jnp.broadcast_to(tail, (tq, HEAD_DIM))

    return _pcall(
        body, name=name, grid=(H, S // tq),
        in_specs=[pl.BlockSpec((None, tq, HEAD_DIM), lambda h, i: (0, i, h)),
                  pl.BlockSpec((None, S, HEAD_DIM), lambda h, i: (1, 0, h)),
                  pl.BlockSpec((None, S, HEAD_DIM), lambda h, i: (2, 0, h))],
        out_specs=[pl.BlockSpec((tq, HEAD_DIM), lambda h, i: (i, h)),
                   pl.BlockSpec((tq, HEAD_DIM), lambda h, i: (i, h))],
        out_shape=[_sds((S, D), BF16), _sds((S, D), F32)],
        compiler_params=_cp(("parallel", "parallel")),
    )(qkv3, qkv3, qkv3)


def _attn_bwd(name, qkv3, do, totals):
    _, S, D = qkv3.shape
    H = D // HEAD_DIM
    tq, tk = _tile(S, 256), 128
    nq = S // tq
    scale = HEAD_DIM ** -0.5

    def body(q_ref, k_ref, v_ref, do_ref, t_ref, out_ref, dk_acc, dv_acc):
        i = pl.program_id(1)

        @pl.when(i == 0)
        def _():
            dk_acc[...] = jnp.zeros_like(dk_acc)
            dv_acc[...] = jnp.zeros_like(dv_acc)

        q = q_ref[...]
        do_b = do_ref[...]
        total = t_ref[...]
        t_idx = i * tq + lax.broadcasted_iota(jnp.int32, (tq, tk), 0)
        s_loc = lax.broadcasted_iota(jnp.int32, (tq, tk), 1)
        jj = lax.broadcasted_iota(jnp.int32, (tk, tk), 0)
        ss = lax.broadcasted_iota(jnp.int32, (tk, tk), 1)
        upto = (jj <= ss).astype(BF16)
        before = (jj < ss).astype(BF16)
        nkb = (i + 1) * (tq // tk)

        def step(jb, carry):
            head, dsum, dq = carry
            k0 = pl.multiple_of(jb * tk, tk)
            kb = k_ref[pl.ds(k0, tk), :]
            vb = v_ref[pl.ds(k0, tk), :]
            z = _dot(q, kb, "nt") * scale
            mask = (k0 + s_loc) < t_idx
            e, logb, log1m = _logsig_parts(z)
            lm = jnp.where(mask, log1m, 0.0)
            stick = total - head - _tri_dot(lm, upto)
            a = jnp.where(mask, jnp.exp(logb + stick), 0.0)
            da = _dot(do_b, vb, "nt")
            de = da * a
            dv_acc[pl.ds(k0, tk), :] += _dot(a.astype(BF16), do_b, "tn")
            csum = dsum + _tri_dot(de, before)
            inv = 1.0 / (1.0 + e)
            sig = jnp.where(z >= 0.0, inv, e * inv)
            dz = jnp.where(mask, de * (1.0 - sig) - csum * sig, 0.0) * scale
            dzb = dz.astype(BF16)
            dk_acc[pl.ds(k0, tk), :] += _dot(dzb, q, "tn")
            dq = dq + _dot(dzb, kb)
            return (head + jnp.sum(lm, axis=1, keepdims=True),
                    dsum + jnp.sum(de, axis=1, keepdims=True), dq)

        zero1 = jnp.zeros((tq, 1), F32)
        _, _, dq = lax.fori_loop(0, nkb, step, (zero1, zero1, jnp.zeros((tq, HEAD_DIM), F32)))
        out_ref[0, pl.ds(pl.multiple_of(i * tq, tq), tq), :] = dq.astype(BF16)

        @pl.when(i == nq - 1)
        def _():
            out_ref[1] = dk_acc[...].astype(BF16)
            out_ref[2] = dv_acc[...].astype(BF16)

    return _pcall(
        body, name=name, grid=(H, nq),
        in_specs=[pl.BlockSpec((None, tq, HEAD_DIM), lambda h, i: (0, i, h)),
                  pl.BlockSpec((None, S, HEAD_DIM), lambda h, i: (1, 0, h)),
                  pl.BlockSpec((None, S, HEAD_DIM), lambda h, i: (2, 0, h)),
                  pl.BlockSpec((tq, HEAD_DIM), lambda h, i: (i, h)),
                  pl.BlockSpec((tq, HEAD_DIM), lambda h, i: (i, h))],
        out_specs=pl.BlockSpec((3, S, HEAD_DIM), lambda h, i: (0, 0, h)),
        out_shape=_sds((3, S, D), BF16),
        scratch_shapes=[pltpu.VMEM((S, HEAD_DIM), F32), pltpu.VMEM((S, HEAD_DIM), F32)],
        compiler_params=_cp(("parallel", "arbitrary")),
    )(qkv3, qkv3, qkv3, do, totals)


def _conv_taps(g, t):
    S = g.shape[0]
    g1 = jnp.where(t >= 1, pltpu.roll(g, 1, 0), 0.0)
    g2 = jnp.where(t >= 2, pltpu.roll(g, 2, 0), 0.0)
    return g1, g2


def _conv_fwd(name, bcu3, cw, cb):
    _, S, D = bcu3.shape
    tf = LANES

    def body(x_ref, w_ref, b_ref, p_ref):
        t = lax.broadcasted_iota(jnp.int32, (S, tf), 0)
        g = x_ref[1] * x_ref[2]
        g1, g2 = _conv_taps(g, t)
        y = w_ref[0:1, :] * g2 + w_ref[1:2, :] * g1 + w_ref[2:3, :] * g + b_ref[...]
        p_ref[...] = (x_ref[0] * y).astype(BF16)

    return _pcall(
        body, name=name, grid=(D // tf,),
        in_specs=[pl.BlockSpec((3, S, tf), lambda f: (0, 0, f)), pl.BlockSpec((3, tf), lambda f: (0, f)),
                  pl.BlockSpec((1, tf), lambda f: (0, f))],
        out_specs=pl.BlockSpec((S, tf), lambda f: (0, f)), out_shape=_sds((S, D), BF16),
        compiler_params=_cp(("parallel",)),
    )(bcu3, cw, cb)


def _conv_bwd(name, bcu3, dp, cw, cb):
    _, S, D = bcu3.shape
    tf = LANES

    def body(x_ref, dp_ref, w_ref, b_ref, o_ref, vec_ref):
        t = lax.broadcasted_iota(jnp.int32, (S, tf), 0)
        bg, cg, u = x_ref[0], x_ref[1], x_ref[2]
        g = cg * u
        g1, g2 = _conv_taps(g, t)
        w0, w1, w2 = w_ref[0:1, :], w_ref[1:2, :], w_ref[2:3, :]
        y = w0 * g2 + w1 * g1 + w2 * g + b_ref[...]
        dpv = dp_ref[...]
        o_ref[0] = (dpv * y).astype(BF16)
        dy = dpv * bg

        def colsum(a):
            return jnp.sum(a, axis=0, keepdims=True)

        vec_ref[...] = jnp.zeros_like(vec_ref)
        for j, row_j in enumerate((dy, dy * g2, dy * g1, dy * g)):
            vec_ref[j:j + 1, :] = colsum(row_j)
        dy1 = jnp.where(t < S - 1, pltpu.roll(dy, S - 1, 0), 0.0)
        dy2 = jnp.where(t < S - 2, pltpu.roll(dy, S - 2, 0), 0.0)
        dg = w2 * dy + w1 * dy1 + w0 * dy2
        o_ref[1] = (dg * u).astype(BF16)
        o_ref[2] = (dg * cg).astype(BF16)

    return _pcall(
        body, name=name, grid=(D // tf,),
        in_specs=[pl.BlockSpec((3, S, tf), lambda f: (0, 0, f)), pl.BlockSpec((S, tf), lambda f: (0, f)),
                  pl.BlockSpec((3, tf), lambda f: (0, f)), pl.BlockSpec((1, tf), lambda f: (0, f))],
        out_specs=[pl.BlockSpec((3, S, tf), lambda f: (0, 0, f)), pl.BlockSpec((8, tf), lambda f: (0, f))],
        out_shape=[_sds((3, S, D), BF16), _sds((8, D), F32)],
        compiler_params=_cp(("parallel",)),
    )(bcu3, dp, cw, cb)


def _adamw(name, w3, m3, v3, l, parts, carry=None):
    L, R, C = w3.shape
    budget_rows = (1 << 18) // C
    tr = R if R <= budget_rows else _tile(R, 1 << (budget_rows.bit_length() - 1))
    n_parts = len(parts)
    n_carry = 0 if carry is None else 4
    c1 = 1.0 / (1.0 - ADAM_B1 ** ADAM_STEP)
    c2 = 1.0 / (1.0 - ADAM_B2 ** ADAM_STEP)

    def body(*refs):
        w_ref, m_ref, v_ref = refs[:3]
        p_refs = refs[3:3 + n_parts]
        og, od, om, ov = refs[3 + n_parts + n_carry:]
        g = p_refs[0][...].astype(F32)
        for p in p_refs[1:]:
            g = g + p[...].astype(F32)
        m = ADAM_B1 * m_ref[...] + (1.0 - ADAM_B1) * g
        v = ADAM_B2 * v_ref[...] + (1.0 - ADAM_B2) * jnp.square(g)
        og[...] = g
        om[...] = m
        ov[...] = v
        od[...] = -ADAM_LR * ((m * c1) / (jnp.sqrt(v * c2) + ADAM_EPS) + ADAM_WD * w_ref[...])

    lay = pl.BlockSpec((None, tr, C), lambda r: (l, r, 0))
    in_specs = [lay, lay, lay]
    args = [w3, m3, v3]
    for arr, idx in parts:
        in_specs.append(pl.BlockSpec((None, tr, C), functools.partial(lambda r, j: (j, r, 0), j=idx)))
        args.append(arr)
    aliases = {}
    if carry is not None:
        for j, buf in enumerate(carry):
            aliases[len(args)] = j
            in_specs.append(ANY_SPEC)
            args.append(buf)
    return _pcall(body, name=name, grid=(R // tr,), in_specs=in_specs, out_specs=[lay] * 4,
                  out_shape=[_sds((L, R, C), F32)] * 4, input_output_aliases=aliases,
                  compiler_params=_cp(("parallel",)))(*args)


def kernel(x, c, mod_w, mod_b, ln_g, ln_b, attn_w_qkv, attn_w_o, conv_w_in, conv_w, conv_b, conv_w_out, mlp_w1, mlp_b1, mlp_w2, mlp_b2, loss_target, m_mod_w, m_mod_b, m_ln_g, m_ln_b, m_attn_w_qkv, m_attn_w_o, m_conv_w_in, m_conv_w, m_conv_b, m_conv_w_out, m_mlp_w1, m_mlp_b1, m_mlp_w2, m_mlp_b2, v_mod_w, v_mod_b, v_ln_g, v_ln_b, v_attn_w_qkv, v_attn_w_o, v_conv_w_in, v_conv_w, v_conv_b, v_conv_w_out, v_mlp_w1, v_mlp_b1, v_mlp_w2, v_mlp_b2):
    xs, target = x[0], loss_target[0]
    S, D = xs.shape
    L = mod_w.shape[0]
    LA, LC = attn_w_qkv.shape[0], conv_w_in.shape[0]
    F = mlp_b1.shape[1]
    DB, FB, NC = D // N_DEV, F // N_DEV, mod_w.shape[2]
    CT = DB
    me = 4 * lax.axis_index("x") + 2 * lax.axis_index("y") + lax.axis_index("c")
    me_arr = jnp.reshape(me, (1,)).astype(jnp.int32)

    small = [c, ln_g, ln_b, conv_w, conv_b]
    pack = jnp.concatenate([a.reshape(-1) for a in small]).reshape(-1, LANES)
    gath = _allgather([pack], "ag_small_params")[0].reshape(N_DEV, -1)
    offs, pos = [], 0
    for a in small:
        offs.append((pos, a.size))
        pos += a.size

    def unshard(idx, lead):
        o, n = offs[idx]
        a = gath[:, o:o + n].reshape((N_DEV,) + lead + (DB,))
        return jnp.moveaxis(a, 0, -2).reshape(lead + (D,))

    c_all = gath[:, :D]
    lng_full, lnb_full = unshard(1, (L, 2)), unshard(2, (L, 2))
    cw_full, cb_full = unshard(3, (LC, 3)), unshard(4, (LC,))

    cond16 = _silu_bf16(jnp.pad(c_all, ((0, 16 - N_DEV), (0, 0))), "silu_cond")
    tn_mod = NC // 3
    bias_mod = lax.dynamic_slice_in_dim(mod_b, me * NC, NC, axis=1).reshape(L, 1, NC)
    mod_part = _matmul(
        "mod_fwd", "nn", (L, NC // tn_mod), 1,
        cond16, pl.BlockSpec((16, D), lambda l, n: (0, 0)),
        mod_w, pl.BlockSpec((None, D, tn_mod), lambda l, n: (l, 0, n)),
        [_sds((L, 16, NC), F32)], [pl.BlockSpec((None, 16, tn_mod), lambda l, n: (l, 0, n))], None,
        extras=[bias_mod], extra_specs=[pl.BlockSpec((None, 1, tn_mod), lambda l, n: (l, 0, n))])[0]
    mod_gath = _allgather([mod_part[:, :N_DEV].reshape(-1, LANES)], "ag_mod")[0].reshape(N_DEV, L, N_DEV, NC)
    mod_me = lax.dynamic_index_in_dim(mod_gath, me, axis=2, keepdims=False)
    mod_me = jnp.moveaxis(mod_me, 0, 1).reshape(L, 6, 1, D)

    def mod_vecs(k):
        l, o = k // 2, 3 * (k % 2)
        return mod_me[l, o], mod_me[l, o + 1], mod_me[l, o + 2]

    def mixer_weights(l):
        if l % 2 == 0:
            return attn_w_qkv, attn_w_o, l // 2
        return conv_w_in, conv_w_out, l // 2

    wg = []
    for l in range(L):
        w_in3, w_out3, j = mixer_weights(l)
        shards = [_cast_bf16(w_in3, j, f"cast_win_{l}"), _cast_bf16(w_out3, j, f"cast_wout_{l}"),
                  _cast_bf16(mlp_w1, l, f"cast_w1_{l}"), _cast_bf16(mlp_w2, l, f"cast_w2_{l}")]
        g_in, g_out, g_w1, g_w2 = _allgather(shards, f"ag_weights_{l}")
        wg.append((g_in, g_out.reshape(D, D), g_w1, g_w2.reshape(F, D)))

    n_ct = 3 * D // CT
    tm_big = _tile(S, 2048)

    def in_proj(l, h, out_dtype):
        return _matmul(
            f"in_proj_{l}", "nn", (S // tm_big, n_ct), 1,
            h, pl.BlockSpec((tm_big, D), lambda i, ct: (i, 0)),
            wg[l][0], pl.BlockSpec((None, D, CT), lambda i, ct: (ct // 3, 0, ct % 3)),
            [_sds((3, S, D), out_dtype)],
            [pl.BlockSpec((None, tm_big, CT), lambda i, ct: (ct // N_DEV, i, ct % N_DEV))], None)[0]

    tm, tn_half = _tile(S, 512), _tile(D, 1024)

    def out_proj(l, a):
        return _matmul(
            f"out_proj_{l}", "nn", (D // tn_half, S // tm), 1,
            a, pl.BlockSpec((tm, D), lambda n, i: (i, 0)),
            wg[l][1], pl.BlockSpec((D, tn_half), lambda n, i: (0, n)),
            [_sds((S, D), F32)], [pl.BlockSpec((tm, tn_half), lambda n, i: (i, n))], None)[0]

    def mlp_up(l, h):
        blk = pl.BlockSpec((tm, FB), lambda j, i: (i, j))
        return _matmul(
            f"mlp_up_{l}", "nn", (N_DEV, S // tm), 1,
            h, pl.BlockSpec((tm, D), lambda j, i: (i, 0)),
            wg[l][2], pl.BlockSpec((None, D, FB), lambda j, i: (j, 0, 0)),
            [_sds((S, F), BF16), _sds((S, F), BF16)], [blk, blk], None, epilogue="relu2",
            extras=[mlp_b1[l].reshape(1, F)], extra_specs=[pl.BlockSpec((1, FB), lambda j, i: (0, j))])

    tk_f = _tile(F, 2048)

    def mlp_down(l, act):
        return _matmul(
            f"mlp_down_{l}", "nn", (D // tn_half, S // tm, F // tk_f), F // tk_f,
            act, pl.BlockSpec((tm, tk_f), lambda n, i, k: (i, k)),
            wg[l][3], pl.BlockSpec((tk_f, tn_half), lambda n, i, k: (k, n)),
            [_sds((S, D), F32)], [pl.BlockSpec((tm, tn_half), lambda n, i, k: (i, n))], (tm, tn_half),
            extras=[mlp_b2[l].reshape(1, D)], extra_specs=[pl.BlockSpec((1, tn_half), lambda n, i, k: (0, n))])[0]

    K = 2 * L
    xin, ys, hs, saved = [xs], [], [], []
    sh0, sc0, _ = mod_vecs(0)
    h = _modulate("modulate_in", xs, sc0, sh0)
    for k in range(K):
        l = k // 2
        hs.append(h)
        if k % 2 == 0 and l % 2 == 0:
            qkv3 = in_proj(l, h, BF16)
            o, totals = _attn_fwd(f"attn_fwd_{l}", qkv3)
            y = out_proj(l, o)
            saved.append((qkv3, o, totals))
        elif k % 2 == 0:
            bcu3 = in_proj(l, h, F32)
            j = l // 2
            p = _conv_fwd(f"conv_fwd_{l}", bcu3, cw_full[j], cb_full[j].reshape(1, D))
            y = out_proj(l, p)
            saved.append((bcu3, p))
        else:
            u, act = mlp_up(l, h)
            y = mlp_down(l, act)
            saved.append((u, act))
        ys.append(y)
        if k + 1 < K:
            _, _, gate = mod_vecs(k)
            sh_n, sc_n, _ = mod_vecs(k + 1)
            x_next, h = _ln_fwd(f"ln_fwd_{k}", xin[k], y, gate, lng_full[l, k % 2].reshape(1, D),
                                lnb_full[l, k % 2].reshape(1, D), sc_n, sh_n)
            xin.append(x_next)

    tk_s = _tile(S, 1024)
    tm_d = _tile(D, 1024)

    def grad_rows(name, a, dy, rows):
        n_m = rows // tm_d
        return _matmul(
            name, "tn", (n_m, D // tn_half, S // tk_s), S // tk_s,
            a, pl.BlockSpec((tk_s, tm_d), lambda m, n, k: (k, m)),
            dy, pl.BlockSpec((tk_s, tn_half), lambda m, n, k: (k, n)),
            [_sds((rows, D), BF16)], [pl.BlockSpec((tm_d, tn_half), lambda m, n, k: (m, n))], (tm_d, tn_half))[0]

    def back_out_proj(l, dy, out_dtype):
        return _matmul(
            f"back_out_proj_{l}", "nt", (D // tn_half, S // tm), 1,
            dy, pl.BlockSpec((tm, D), lambda n, i: (i, 0)),
            wg[l][1], pl.BlockSpec((tn_half, D), lambda n, i: (n, 0)),
            [_sds((S, D), out_dtype)], [pl.BlockSpec((tm, tn_half), lambda n, i: (i, n))], None)[0]

    tm_1k = _tile(S, 1024)

    def back_in_proj(l, d3):
        return _matmul(
            f"back_in_proj_{l}", "nt", (S // tm_1k, n_ct), n_ct,
            d3, pl.BlockSpec((None, tm_1k, CT), lambda i, ct: (ct // N_DEV, i, ct % N_DEV)),
            wg[l][0], pl.BlockSpec((None, D, CT), lambda i, ct: (ct // 3, 0, ct % 3)),
            [_sds((S, D), F32)], [pl.BlockSpec((tm_1k, D), lambda i, ct: (i, 0))], (tm_1k, D))[0]

    def grad_in_proj(l, h, d3):
        return _matmul(
            f"grad_in_proj_{l}", "tn", (n_ct, S // tk_s), S // tk_s,
            h, pl.BlockSpec((tk_s, D), lambda ct, k: (k, 0)),
            d3, pl.BlockSpec((None, tk_s, CT), lambda ct, k: (ct // N_DEV, k, ct % N_DEV)),
            [_sds((N_DEV, D, 3 * CT), BF16)],
            [pl.BlockSpec((None, D, CT), lambda ct, k: (ct // 3, 0, ct % 3))], (D, CT))[0]

    def back_mlp_down(l, dy, u):
        blk = pl.BlockSpec((tm, FB), lambda j, i: (i, j))
        return _matmul(
            f"back_mlp_down_{l}", "nt", (N_DEV, S // tm), 1,
            dy, pl.BlockSpec((tm, D), lambda j, i: (i, 0)),
            wg[l][3], pl.BlockSpec((FB, D), lambda j, i: (j, 0)),
            [_sds((S, F), BF16), _sds((1, F), F32)], [blk, pl.BlockSpec((1, FB), lambda j, i: (0, j))], None,
            epilogue="drelu2", extras=[u], extra_specs=[blk], sem=("parallel", "arbitrary"))

    def back_mlp_up(l, du):
        return _matmul(
            f"back_mlp_up_{l}", "nt", (S // tm_1k, N_DEV), N_DEV,
            du, pl.BlockSpec((tm_1k, FB), lambda i, j: (i, j)),
            wg[l][2], pl.BlockSpec((None, D, FB), lambda i, j: (j, 0, 0)),
            [_sds((S, D), F32)], [pl.BlockSpec((tm_1k, D), lambda i, j: (i, 0))], (tm_1k, D))[0]

    def grad_mlp_up(l, h, du):
        return _matmul(
            f"grad_mlp_up_{l}", "tn", (N_DEV, D // tm_d, S // tk_s), S // tk_s,
            h, pl.BlockSpec((tk_s, tm_d), lambda j, m, k: (k, m)),
            du, pl.BlockSpec((tk_s, FB), lambda j, m, k: (k, j)),
            [_sds((N_DEV, D, FB), BF16)], [pl.BlockSpec((None, tm_d, FB), lambda j, m, k: (j, m, 0))],
            (tm_d, FB))[0]

    dmod = [[None] * 6 for _ in range(L)]
    dlng = [[None, None] for _ in range(L)]
    dlnb = [[None, None] for _ in range(L)]
    db1, db2 = [None] * L, [None] * L
    dcw, dcb = [None] * LC, [None] * LC
    big_grads = [None] * L

    def take_ln_rows(k, vec):
        l, o = k // 2, 3 * (k % 2)
        dlng[l][k % 2], dlnb[l][k % 2] = vec[2], vec[3]
        dmod[l][o + 2] = vec[4]
        if k % 2 == 1:
            db2[l] = vec[5]

    def take_mod_rows(k, vec):
        l, o = k // 2, 3 * (k % 2)
        dmod[l][o + 1], dmod[l][o] = vec[0], vec[1]

    lk = K - 1
    _, _, gate = mod_vecs(lk)
    dy, dxr, vec = _boundary(f"boundary_{lk}", "last", (target, xin[lk], ys[lk]),
                             (gate, lng_full[lk // 2, 1].reshape(1, D), lnb_full[lk // 2, 1].reshape(1, D)))
    take_ln_rows(lk, vec)
    loss_local = 0.5 * vec[6, 0] / D
    grad_x = None
    for k in range(K - 1, -1, -1):
        l = k // 2
        if k % 2 == 1:
            u, act = saved[k]
            g_w2 = grad_rows(f"grad_mlp_down_{l}", act, dy, F).reshape(N_DEV, FB, D)
            du, db1_l = back_mlp_down(l, dy, u)
            db1[l] = db1_l[0]
            g_w1 = grad_mlp_up(l, hs[k], du)
            dh = back_mlp_up(l, du)
            big_grads[l] = [None, None, g_w1, g_w2]
        else:
            if l % 2 == 0:
                qkv3, o, totals = saved[k]
                g_out = grad_rows(f"grad_out_proj_{l}", o, dy, D).reshape(N_DEV, DB, D)
                do = back_out_proj(l, dy, BF16)
                d3 = _attn_bwd(f"attn_bwd_{l}", qkv3, do, totals)
            else:
                bcu3, p = saved[k]
                j = l // 2
                g_out = grad_rows(f"grad_out_proj_{l}", p, dy, D).reshape(N_DEV, DB, D)
                dp = back_out_proj(l, dy, F32)
                d3, cvec = _conv_bwd(f"conv_bwd_{l}", bcu3, dp, cw_full[j], cb_full[j].reshape(1, D))
                dcb[j], dcw[j] = cvec[0], cvec[1:4]
            g_in = grad_in_proj(l, hs[k], d3)
            dh = back_in_proj(l, d3)
            big_grads[l][0], big_grads[l][1] = g_in, g_out
        sh_k, sc_k, _ = mod_vecs(k)
        if k > 0:
            kp = k - 1
            _, _, gate = mod_vecs(kp)
            dy, dxr, vec = _boundary(
                f"boundary_{kp}", "mid", (dxr, dh, xin[kp], ys[kp]),
                (sc_k, gate, lng_full[kp // 2, kp % 2].reshape(1, D), lnb_full[kp // 2, kp % 2].reshape(1, D)))
            take_mod_rows(k, vec)
            take_ln_rows(kp, vec)
        else:
            grad_x, vec = _boundary("boundary_in", "first", (dxr, dh, xs), (sc_k,))
            take_mod_rows(k, vec)

    results = {}
    carries = {"attn_in": None, "attn_out": None, "conv_in": None, "conv_out": None, "w1": None, "w2": None}
    moments = {"attn_in": (attn_w_qkv, m_attn_w_qkv, v_attn_w_qkv), "attn_out": (attn_w_o, m_attn_w_o, v_attn_w_o),
               "conv_in": (conv_w_in, m_conv_w_in, v_conv_w_in), "conv_out": (conv_w_out, m_conv_w_out, v_conv_w_out),
               "w1": (mlp_w1, m_mlp_w1, v_mlp_w1), "w2": (mlp_w2, m_mlp_w2, v_mlp_w2)}
    for l in range(L - 1, -1, -1):
        gs = big_grads[l]
        r1 = _rs_sibling(gs, f"rs_sibling_{l}")
        ps = [_rs_add(g, r, me_arr, f"rs_add_{l}_{t}") for t, (g, r) in enumerate(zip(gs, r1))]
        r2 = _rs_chips(ps, f"rs_chips_{l}")
        kind = "attn" if l % 2 == 0 else "conv"
        for t, key, idx in ((0, kind + "_in", l // 2), (1, kind + "_out", l // 2), (2, "w1", l), (3, "w2", l)):
            w3, m3, v3 = moments[key]
            parts = [(ps[t], 0), (r2[t], 1), (r2[t], 2), (r2[t], 3)]
            carries[key] = _adamw(f"adamw_{key}_{l}", w3, m3, v3, idx, parts, carries[key])
    for key in carries:
        results[key] = carries[key]

    flat = [jnp.stack([jnp.stack(r) for r in dmod]).reshape(-1), jnp.stack(db1).reshape(-1),
            jnp.stack(db2).reshape(-1), jnp.stack([jnp.stack(r) for r in dlng]).reshape(-1),
            jnp.stack([jnp.stack(r) for r in dlnb]).reshape(-1), jnp.stack(dcw).reshape(-1),
            jnp.stack(dcb).reshape(-1)]
    sizes = [a.size for a in flat]
    small_g = _allgather([jnp.concatenate(flat).reshape(-1, LANES)], "ag_small_grads")[0]
    total = _sum_devices(small_g, "sum_small_grads").reshape(-1)
    starts = [sum(sizes[:i]) for i in range(len(sizes))]

    def piece(i, shape):
        return total[starts[i]:starts[i] + sizes[i]].reshape(shape)

    def my_block(a):
        return lax.dynamic_slice_in_dim(a, me * DB, DB, axis=a.ndim - 1)

    g_mod_b = piece(0, (L, 6 * D))
    g_b1, g_b2 = piece(1, (L, F)), piece(2, (L, D))
    g_lng, g_lnb = my_block(piece(3, (L, 2, D))), my_block(piece(4, (L, 2, D)))
    g_cw, g_cb = my_block(piece(5, (LC, 3, D))), my_block(piece(6, (LC, D)))

    dmod_all = small_g.reshape(N_DEV, -1)[:, :sizes[0]].reshape(N_DEV, L, 6 * D)
    dmod_mine = jnp.moveaxis(lax.dynamic_slice_in_dim(dmod_all, me * NC, NC, axis=2), 0, 1)
    dmod16 = jnp.pad(dmod_mine, ((0, 0), (0, 16 - N_DEV), (0, 0)))
    tm_mod = _tile(D, 512)
    g_mod_w = _matmul(
        "grad_mod_w", "tn", (L, D // tm_mod), 1,
        cond16, pl.BlockSpec((16, tm_mod), lambda l, m: (0, m)),
        dmod16, pl.BlockSpec((None, 16, NC), lambda l, m: (l, 0, 0)),
        [_sds((L, D, NC), F32)], [pl.BlockSpec((None, tm_mod, NC), lambda l, m: (l, m, 0))], None)[0]
    res_mod_w = None
    for l in range(L):
        res_mod_w = _adamw(f"adamw_mod_w_{l}", mod_w, m_mod_w, v_mod_w, l, [(g_mod_w, l)], res_mod_w)

    small_w = [mod_b, ln_g, ln_b, conv_w, conv_b, mlp_b1, mlp_b2]
    small_m = [m_mod_b, m_ln_g, m_ln_b, m_conv_w, m_conv_b, m_mlp_b1, m_mlp_b2]
    small_v = [v_mod_b, v_ln_g, v_ln_b, v_conv_w, v_conv_b, v_mlp_b1, v_mlp_b2]
    small_gr = [g_mod_b, g_lng, g_lnb, g_cw, g_cb, g_b1, g_b2]
    n_small = sum(a.size for a in small_w)
    pad = (-n_small) % (8 * LANES)

    def packed(arrs):
        v = jnp.concatenate([a.reshape(-1) for a in arrs])
        return jnp.pad(v, (0, pad)).reshape(1, -1, LANES)

    sg, sd, sm, sv = _adamw("adamw_small", packed(small_w), packed(small_m), packed(small_v), 0,
                            [(packed(small_gr), 0)])

    def unpacked(buf):
        out, pos = [], 0
        flat_buf = buf.reshape(-1)
        for a in small_w:
            out.append(flat_buf[pos:pos + a.size].reshape(a.shape))
            pos += a.size
        return out

    small_res = [unpacked(b) for b in (sg, sd, sm, sv)]
    loss = lax.psum(loss_local, ("x", "y", "c"))

    def leaf(kind):
        s = small_res[kind]
        return [res_mod_w[kind], s[0], s[1], s[2], results["attn_in"][kind], results["attn_out"][kind],
                results["conv_in"][kind], s[3], s[4], results["conv_out"][kind], results["w1"][kind], s[5],
                results["w2"][kind], s[6]]

    return (loss, grad_x[None], *leaf(0), *leaf(1), *leaf(2), *leaf(3))
```

```python
import functools

import jax
import jax.numpy as jnp
from jax import lax
from jax.experimental import pallas as pl
from jax.experimental.pallas import tpu as pltpu

F32 = jnp.float32
BF16 = jnp.bfloat16
N_DEV = 8
HEAD_DIM = 128
LANES = 128
DEEPNORM_ALPHA = 8.0 ** 0.25
LN_EPS = 1e-5
ADAM_LR = 0.001
ADAM_B1 = 0.9
ADAM_B2 = 0.999
ADAM_EPS = 1e-08
ADAM_WD = 0.01
ADAM_STEP = 10
VMEM_LIMIT_BYTES = 52 * 1024 * 1024
MESH = pl.DeviceIdType.MESH
HBM_SPEC = pl.BlockSpec(memory_space=pltpu.HBM)
ANY_SPEC = pl.BlockSpec(memory_space=pl.ANY)
SEM_SPEC = pl.BlockSpec(memory_space=pltpu.SEMAPHORE)
EFFECT = pltpu.SideEffectType.DATAFLOW_SIDE_EFFECTING


def _pcall(body, **kw):
    return pl.pallas_call(body, **kw)


def _cp(sem=None):
    return pltpu.CompilerParams(dimension_semantics=sem, vmem_limit_bytes=VMEM_LIMIT_BYTES)


def _tile(n, pref):
    t = min(n, pref)
    assert n % t == 0, (n, pref)
    return t


def _sds(shape, dtype):
    return jax.ShapeDtypeStruct(tuple(shape), dtype)


def _allgather(xs, name, deps=()):
    n, nd = len(xs), len(deps)

    def body(*refs):
        x_refs = refs[:n]
        out_refs = refs[n + nd:2 * n + nd]
        send_sems, recv_sems, local_sems = refs[2 * n + nd:]
        x, y, c = lax.axis_index("x"), lax.axis_index("y"), lax.axis_index("c")
        me, sibling = (x, y, c), (x, y, 1 - c)
        chips = [(1 - x, y), (x, 1 - y), (1 - x, 1 - y)]

        def slot(t, px, py, pc):
            return out_refs[t].at[4 * px + 2 * py + pc]

        def copy(t, k, block, to, src=None):
            return pltpu.make_async_remote_copy(
                src_ref=slot(t, *block) if src is None else src, dst_ref=slot(t, *block),
                send_sem=send_sems.at[t, k], recv_sem=recv_sems.at[t, k],
                device_id=to, device_id_type=MESH)

        mine = [pltpu.make_async_copy(x_refs[t], slot(t, *me), local_sems.at[t]) for t in range(n)]
        for cp in mine:
            cp.start()
        first = []
        for t in range(n):
            first.append(copy(t, 0, me, sibling, src=x_refs[t]))
            for j, chip in enumerate(chips):
                first.append(copy(t, 1 + j, me, (*chip, c), src=x_refs[t]))
        for cp in first:
            cp.start()
        passed = []
        for j, chip in enumerate(chips):
            for t in range(n):
                copy(t, 1 + j, (*chip, c), me).wait_recv()
                fwd = copy(t, 4 + j, (*chip, c), sibling)
                fwd.start()
                passed.append(fwd)
        for t in range(n):
            copy(t, 0, sibling, me).wait_recv()
            for j, chip in enumerate(chips):
                copy(t, 4 + j, (*chip, 1 - c), me).wait_recv()
        for cp in first + passed:
            cp.wait_send()
        for cp in mine:
            cp.wait()

    outs = _pcall(
        body, name=name,
        out_shape=[_sds((N_DEV,) + a.shape, a.dtype) for a in xs],
        in_specs=[HBM_SPEC] * n + [ANY_SPEC] * nd, out_specs=[HBM_SPEC] * n,
        scratch_shapes=[pltpu.SemaphoreType.DMA((n, 7)), pltpu.SemaphoreType.DMA((n, 7)),
                        pltpu.SemaphoreType.DMA((n,))],
    )(*xs, *deps)
    return list(outs)


def _peers(x, y, c):
    return [(x, y, 1 - c), (1 - x, y, c), (x, 1 - y, c), (1 - x, 1 - y, c)]


def _block_owner(x, y, c, k):
    return ((1 - x) if k >> 1 else x, (1 - y) if k & 1 else y, c)


def _hbm(a):
    return pltpu.with_memory_space_constraint(a, pltpu.HBM)


def _ag_start(xs, deps, name):
    n, nd = len(xs), len(deps)

    def body(*refs):
        x_refs, land_refs = refs[:n], refs[n:2 * n]
        send_sems, recv_sems = refs[2 * n + nd], refs[2 * n + nd + 1]
        token = refs[-1]
        x, y, c = lax.axis_index("x"), lax.axis_index("y"), lax.axis_index("c")
        for t in range(n):
            for k, to in enumerate(_peers(x, y, c)):
                pltpu.make_async_remote_copy(
                    src_ref=x_refs[t], dst_ref=land_refs[t].at[4 * x + 2 * y + c],
                    send_sem=send_sems.at[4 * t + k], recv_sem=recv_sems.at[4 * t + k],
                    device_id=to, device_id_type=MESH).start()
        token[...] = jnp.zeros_like(token)

    lands = [lax.empty((N_DEV,) + a.shape, a.dtype) for a in xs]
    outs = _pcall(
        body, name=name,
        out_shape=(pltpu.SemaphoreType.DMA((4 * n,)), pltpu.SemaphoreType.DMA((4 * n,)),
                   *[pltpu.HBM(a.shape, a.dtype) for a in xs], *[pltpu.HBM(a.shape, a.dtype) for a in lands],
                   _sds((8, LANES), F32)),
        in_specs=[HBM_SPEC] * (2 * n) + [ANY_SPEC] * nd,
        out_specs=(SEM_SPEC, SEM_SPEC, *[HBM_SPEC] * (2 * n), pl.BlockSpec(memory_space=pltpu.VMEM)),
        input_output_aliases={i: 2 + i for i in range(2 * n)},
        compiler_params=pltpu.CompilerParams(has_side_effects=EFFECT),
    )(*[_hbm(a) for a in xs], *[_hbm(a) for a in lands], *deps)
    return outs[0], outs[1], list(outs[2:2 + n]), list(outs[2 + n:2 + 2 * n]), outs[-1]


def _ag_wait(started, after, name):
    send_sems, recv_sems, xs, lands, _ = started
    n = len(xs)

    def body(*refs):
        x_refs, land_refs = refs[:n], refs[n:2 * n]
        send_sems, recv_sems = refs[2 * n], refs[2 * n + 1]
        x, y, c = lax.axis_index("x"), lax.axis_index("y"), lax.axis_index("c")
        for t in range(n):
            for k, (px, py, pc) in enumerate(_peers(x, y, c)):
                cp = pltpu.make_async_remote_copy(
                    src_ref=x_refs[t], dst_ref=land_refs[t].at[4 * px + 2 * py + pc],
                    send_sem=send_sems.at[4 * t + k], recv_sem=recv_sems.at[4 * t + k],
                    device_id=(px, py, pc), device_id_type=MESH)
                cp.wait_send()
                cp.wait_recv()

    outs = _pcall(
        body, name=name,
        out_shape=(*[pltpu.HBM(a.shape, a.dtype) for a in xs], *[pltpu.HBM(a.shape, a.dtype) for a in lands]),
        in_specs=[HBM_SPEC] * (2 * n) + [SEM_SPEC, SEM_SPEC, ANY_SPEC],
        out_specs=[HBM_SPEC] * (2 * n),
        input_output_aliases={i: i for i in range(2 * n)},
        compiler_params=pltpu.CompilerParams(has_side_effects=EFFECT),
    )(*xs, *lands, send_sems, recv_sems, after)
    return list(outs[:n]), list(outs[n:])


def _ag_finish(xs, lands, deps, name):
    n, nd = len(xs), len(deps)

    def body(*refs):
        x_refs = refs[:n]
        out_refs = refs[2 * n + nd:3 * n + nd]
        send_sems, recv_sems, local_sems = refs[3 * n + nd:]
        x, y, c = lax.axis_index("x"), lax.axis_index("y"), lax.axis_index("c")
        chips = [(1 - x, y), (x, 1 - y), (1 - x, 1 - y)]

        def passing(t, j, pc):
            px, py = chips[j]
            rows = out_refs[t].at[4 * px + 2 * py + pc]
            return pltpu.make_async_remote_copy(
                src_ref=rows, dst_ref=rows, send_sem=send_sems.at[t, j], recv_sem=recv_sems.at[t, j],
                device_id=(x, y, 1 - c), device_id_type=MESH)

        mine = [pltpu.make_async_copy(x_refs[t], out_refs[t].at[4 * x + 2 * y + c], local_sems.at[t])
                for t in range(n)]
        sends = [passing(t, j, c) for t in range(n) for j in range(3)]
        for cp in mine + sends:
            cp.start()
        for t in range(n):
            for j in range(3):
                passing(t, j, 1 - c).wait_recv()
        for cp in sends:
            cp.wait_send()
        for cp in mine:
            cp.wait()

    outs = _pcall(
        body, name=name,
        out_shape=[_sds(a.shape, a.dtype) for a in lands],
        in_specs=[HBM_SPEC] * (2 * n) + [ANY_SPEC] * nd, out_specs=[HBM_SPEC] * n,
        input_output_aliases={n + i: i for i in range(n)},
        scratch_shapes=[pltpu.SemaphoreType.DMA((n, 3)), pltpu.SemaphoreType.DMA((n, 3)),
                        pltpu.SemaphoreType.DMA((n,))],
    )(*xs, *lands, *deps)
    return list(outs)


def _rs_chips_start(ps, name):
    n = len(ps)

    def body(*refs):
        p_refs, land_refs = refs[:n], refs[n:2 * n]
        send_sems, recv_sems = refs[2 * n], refs[2 * n + 1]
        token = refs[-1]
        x, y, c = lax.axis_index("x"), lax.axis_index("y"), lax.axis_index("c")
        for t in range(n):
            for k in range(1, 4):
                pltpu.make_async_remote_copy(
                    src_ref=p_refs[t].at[k], dst_ref=land_refs[t].at[k],
                    send_sem=send_sems.at[3 * t + k - 1], recv_sem=recv_sems.at[3 * t + k - 1],
                    device_id=_block_owner(x, y, c, k), device_id_type=MESH).start()
        token[...] = jnp.zeros_like(token)

    lands = [lax.empty(p.shape, p.dtype) for p in ps]
    outs = _pcall(
        body, name=name,
        out_shape=(pltpu.SemaphoreType.DMA((3 * n,)), pltpu.SemaphoreType.DMA((3 * n,)),
                   *[pltpu.HBM(a.shape, a.dtype) for a in ps], *[pltpu.HBM(a.shape, a.dtype) for a in lands],
                   _sds((8, LANES), F32)),
        in_specs=[HBM_SPEC] * (2 * n),
        out_specs=(SEM_SPEC, SEM_SPEC, *[HBM_SPEC] * (2 * n), pl.BlockSpec(memory_space=pltpu.VMEM)),
        input_output_aliases={i: 2 + i for i in range(2 * n)},
        compiler_params=pltpu.CompilerParams(has_side_effects=EFFECT),
    )(*[_hbm(a) for a in ps], *[_hbm(a) for a in lands])
    return outs[0], outs[1], list(outs[2:2 + n]), list(outs[2 + n:2 + 2 * n]), outs[-1]


def _rs_chips_wait(started, after, name):
    send_sems, recv_sems, ps, lands, _ = started
    n = len(ps)

    def body(*refs):
        p_refs, land_refs = refs[:n], refs[n:2 * n]
        send_sems, recv_sems = refs[2 * n], refs[2 * n + 1]
        x, y, c = lax.axis_index("x"), lax.axis_index("y"), lax.axis_index("c")
        for t in range(n):
            for k in range(1, 4):
                cp = pltpu.make_async_remote_copy(
                    src_ref=p_refs[t].at[k], dst_ref=land_refs[t].at[k],
                    send_sem=send_sems.at[3 * t + k - 1], recv_sem=recv_sems.at[3 * t + k - 1],
                    device_id=_block_owner(x, y, c, k), device_id_type=MESH)
                cp.wait_send()
                cp.wait_recv()

    outs = _pcall(
        body, name=name,
        out_shape=(*[pltpu.HBM(a.shape, a.dtype) for a in ps], *[pltpu.HBM(a.shape, a.dtype) for a in lands]),
        in_specs=[HBM_SPEC] * (2 * n) + [SEM_SPEC, SEM_SPEC, ANY_SPEC],
        out_specs=[HBM_SPEC] * (2 * n),
        input_output_aliases={i: i for i in range(2 * n)},
        compiler_params=pltpu.CompilerParams(has_side_effects=EFFECT),
    )(*ps, *lands, send_sems, recv_sems, after)
    return list(outs[:n]), list(outs[n:])


def _rs_sibling(gs, name):
    n = len(gs)

    def body(*refs):
        g_refs = refs[:n]
        r_refs = refs[n:2 * n]
        send_sems, recv_sems = refs[2 * n:]
        x, y, c = lax.axis_index("x"), lax.axis_index("y"), lax.axis_index("c")
        me = 4 * x + 2 * y + c
        cps = []
        for t in range(n):
            for k in range(4):
                cps.append(pltpu.make_async_remote_copy(
                    src_ref=g_refs[t].at[jnp.bitwise_xor(2 * k + 1, me)], dst_ref=r_refs[t].at[k],
                    send_sem=send_sems.at[t, k], recv_sem=recv_sems.at[t, k],
                    device_id=(x, y, 1 - c), device_id_type=MESH))
        for cp in cps:
            cp.start()
        for cp in cps:
            cp.wait()

    outs = _pcall(
        body, name=name,
        out_shape=[_sds((4,) + g.shape[1:], g.dtype) for g in gs],
        in_specs=[HBM_SPEC] * n, out_specs=[HBM_SPEC] * n,
        scratch_shapes=[pltpu.SemaphoreType.DMA((n, 4)), pltpu.SemaphoreType.DMA((n, 4))],
    )(*gs)
    return list(outs)


def _rs_add(g, r1, me_arr, name):
    _, R, C = g.shape
    tr = _tile(R, 256)

    def body(me_ref, g_ref, r_ref, o_ref):
        o_ref[...] = (g_ref[...].astype(F32) + r_ref[...].astype(F32)).astype(BF16)

    grid_spec = pltpu.PrefetchScalarGridSpec(
        num_scalar_prefetch=1, grid=(4, R // tr),
        in_specs=[pl.BlockSpec((None, tr, C), lambda k, r, me: (jnp.bitwise_xor(2 * k, me[0]), r, 0)),
                  pl.BlockSpec((None, tr, C), lambda k, r, me: (k, r, 0))],
        out_specs=pl.BlockSpec((None, tr, C), lambda k, r, me: (k, r, 0)))
    return _pcall(body, name=name, grid_spec=grid_spec, out_shape=_sds((4, R, C), BF16),
                  compiler_params=_cp(("parallel", "parallel")))(me_arr, g, r1)


_DIMS = {"nn": (((1,), (0,)), ((), ())), "nt": (((1,), (1,)), ((), ())), "tn": (((0,), (0,)), ((), ()))}


def _dot(a, b, mode="nn"):
    return lax.dot_general(a, b, _DIMS[mode], preferred_element_type=F32)


def _matmul(name, mode, grid, nk, a, a_spec, b, b_spec, out_shapes, out_specs, acc_shape,
            epilogue="plain", extras=(), extra_specs=(), sem=None):
    n_extra, n_out = len(extras), len(out_shapes)

    def body(*refs):
        a_ref, b_ref = refs[0], refs[1]
        ex = refs[2:2 + n_extra]
        outs = refs[2 + n_extra:2 + n_extra + n_out]
        part = _dot(a_ref[...].astype(BF16), b_ref[...].astype(BF16), mode)

        def finish(acc):
            if epilogue == "plain":
                if n_extra:
                    acc = acc + ex[0][...]
                outs[0][...] = acc.astype(outs[0].dtype)
            elif epilogue == "relu2":
                u = acc + ex[0][...]
                outs[0][...] = u.astype(outs[0].dtype)
                outs[1][...] = jnp.square(jnp.maximum(u, 0.0)).astype(outs[1].dtype)
            else:
                du = acc * (2.0 * jnp.maximum(ex[0][...].astype(F32), 0.0))
                outs[0][...] = du.astype(outs[0].dtype)
                col = jnp.sum(du, axis=0, keepdims=True)
                inner = pl.program_id(1)

                @pl.when(inner == 0)
                def _():
                    outs[1][...] = col

                @pl.when(inner > 0)
                def _():
                    outs[1][...] += col

        if nk == 1:
            finish(part)
        else:
            acc_ref = refs[2 + n_extra + n_out]
            kidx = pl.program_id(len(grid) - 1)

            @pl.when(kidx == 0)
            def _():
                acc_ref[...] = part

            @pl.when(kidx > 0)
            def _():
                acc_ref[...] += part

            @pl.when(kidx == nk - 1)
            def _():
                finish(acc_ref[...])

    if sem is None:
        sem = ("parallel",) * (len(grid) - (1 if nk > 1 else 0)) + (("arbitrary",) if nk > 1 else ())
    outs = _pcall(
        body, name=name, grid=grid,
        in_specs=[a_spec, b_spec] + list(extra_specs), out_specs=list(out_specs),
        out_shape=list(out_shapes),
        scratch_shapes=[pltpu.VMEM(acc_shape, F32)] if nk > 1 else [],
        compiler_params=_cp(sem),
    )(a, b, *extras)
    return list(outs)


def _cast_bf16(w3, l, name):
    _, R, C = w3.shape
    tr = _tile(R, 256)

    def body(x_ref, o_ref):
        o_ref[...] = x_ref[...].astype(BF16)

    return _pcall(body, name=name, grid=(R // tr,),
                  in_specs=[pl.BlockSpec((None, tr, C), lambda r: (l, r, 0))],
                  out_specs=pl.BlockSpec((tr, C), lambda r: (r, 0)),
                  out_shape=_sds((R, C), BF16), compiler_params=_cp(("parallel",)))(w3)


def _silu_bf16(c16, name):
    def body(c_ref, o_ref):
        v = c_ref[...]
        o_ref[...] = (v / (1.0 + jnp.exp(-v))).astype(BF16)

    return _pcall(body, name=name, out_shape=_sds(c16.shape, BF16))(c16)


def _sum_devices(g, name):
    def body(g_ref, o_ref):
        acc = g_ref[0]
        for d in range(1, N_DEV):
            acc = acc + g_ref[d]
        o_ref[...] = acc

    return _pcall(body, name=name, out_shape=_sds(g.shape[1:], F32))(g)


def _modulate(name, x, sc, sh):
    S, D = x.shape
    ts = _tile(S, 256)

    def body(x_ref, sc_ref, sh_ref, h_ref):
        h_ref[...] = (x_ref[...] * (1.0 + sc_ref[...]) + sh_ref[...]).astype(BF16)

    row = pl.BlockSpec((ts, D), lambda i: (i, 0))
    vec = pl.BlockSpec((1, D), lambda i: (0, 0))
    return _pcall(body, name=name, grid=(S // ts,), in_specs=[row, vec, vec], out_specs=row,
                  out_shape=_sds((S, D), BF16), compiler_params=_cp(("parallel",)))(x, sc, sh)


def _ln_stats(r):
    mu = jnp.mean(r, axis=-1, keepdims=True)
    xc = r - mu
    var = jnp.mean(xc * xc, axis=-1, keepdims=True)
    rstd = lax.rsqrt(var + LN_EPS)
    return xc * rstd, rstd


def _ln_fwd(name, xin, y, gate, lng, lnb, sc, sh):
    S, D = xin.shape
    ts = _tile(S, 256)

    def body(x_ref, y_ref, g_ref, lg_ref, lb_ref, sc_ref, sh_ref, xo_ref, h_ref):
        r = DEEPNORM_ALPHA * x_ref[...] + (1.0 + g_ref[...]) * y_ref[...]
        xhat, _ = _ln_stats(r)
        xo = xhat * lg_ref[...] + lb_ref[...]
        xo_ref[...] = xo
        h_ref[...] = (xo * (1.0 + sc_ref[...]) + sh_ref[...]).astype(BF16)

    row = pl.BlockSpec((ts, D), lambda i: (i, 0))
    vec = pl.BlockSpec((1, D), lambda i: (0, 0))
    return _pcall(body, name=name, grid=(S // ts,), in_specs=[row, row] + [vec] * 5,
                  out_specs=[row, row], out_shape=[_sds((S, D), F32), _sds((S, D), BF16)],
                  compiler_params=_cp(("parallel",)))(xin, y, gate, lng, lnb, sc, sh)


def _boundary(name, mode, big, vecs, deps=()):
    S, D = big[0].shape
    ts = _tile(S, 128)
    nb, nv, nd = len(big), len(vecs), len(deps)

    def body(*refs):
        b = [r[...] for r in refs[:nb]]
        v = [r[...] for r in refs[nb:nb + nv]]
        outs = refs[nb + nv + nd:]
        vec_ref = outs[-1]
        i = pl.program_id(0)
        rows = [None] * 8

        def colsum(a):
            return jnp.sum(a, axis=0, keepdims=True)

        if mode == "first":
            dxr, dh, xk = b
            sc = v[0]
            outs[0][...] = dxr + dh * (1.0 + sc)
            rows[0], rows[1] = colsum(dh * xk), colsum(dh)
        else:
            if mode == "mid":
                dxr, dh, xprev, yk = b
                sc, gate, lng, lnb = v
            else:
                target, xprev, yk = b
                gate, lng, lnb = v
            r = DEEPNORM_ALPHA * xprev + (1.0 + gate) * yk
            xhat, rstd = _ln_stats(r)
            if mode == "mid":
                xk = xhat * lng + lnb
                d = dxr + dh * (1.0 + sc)
                rows[0], rows[1] = colsum(dh * xk), colsum(dh)
            else:
                err = xhat * lng + lnb - target
                d = err * (1.0 / D)
                rows[6] = jnp.zeros((1, D), F32) + jnp.sum(err * err)
            rows[2], rows[3] = colsum(d * xhat), colsum(d)
            dxhat = d * lng
            m1 = jnp.mean(dxhat, axis=-1, keepdims=True)
            m2 = jnp.mean(dxhat * xhat, axis=-1, keepdims=True)
            dr = rstd * (dxhat - m1 - xhat * m2)
            dy = (1.0 + gate) * dr
            rows[4], rows[5] = colsum(dr * yk), colsum(dy)
            outs[0][...] = dy.astype(BF16)
            outs[1][...] = DEEPNORM_ALPHA * dr
        @pl.when(i == 0)
        def _():
            vec_ref[...] = jnp.zeros_like(vec_ref)

        for j, row_j in enumerate(rows):
            if row_j is not None:
                vec_ref[j:j + 1, :] += row_j

    row = pl.BlockSpec((ts, D), lambda i: (i, 0))
    vec = pl.BlockSpec((1, D), lambda i: (0, 0))
    vec_out = pl.BlockSpec((8, D), lambda i: (0, 0))
    if mode == "first":
        out_specs, out_shape = [row, vec_out], [_sds((S, D), F32), _sds((8, D), F32)]
    else:
        out_specs = [row, row, vec_out]
        out_shape = [_sds((S, D), BF16), _sds((S, D), F32), _sds((8, D), F32)]
    return _pcall(body, name=name, grid=(S // ts,), in_specs=[row] * nb + [vec] * nv + [ANY_SPEC] * nd,
                  out_specs=out_specs, out_shape=out_shape,
                  compiler_params=_cp(("arbitrary",)))(*big, *vecs, *deps)


def _split_bf16(a):
    hi = a.astype(BF16)
    lo = (a - hi.astype(F32)).astype(BF16)
    return hi, lo


def _cumsum_matrix(tri):
    half = jnp.concatenate([tri.astype(BF16), jnp.ones((LANES, LANES), BF16)], axis=1)
    return jnp.concatenate([half, half], axis=0)


def _chunk_sums(a, mat):
    hi, lo = _split_bf16(a)
    both = _dot(jnp.concatenate([hi, lo], axis=1), mat)
    return both[:, :LANES], both[:, LANES:]


def _log1m_beta(z):
    e = jnp.exp(-jnp.abs(z))
    return e, -(jnp.maximum(z, 0.0) + jnp.log(1.0 + e))


def _attn_fwd(name, qkv3):
    _, S, D = qkv3.shape
    H = D // HEAD_DIM
    tq = _tile(S, 512)
    nch = tq // LANES
    scale = HEAD_DIM ** -0.5

    def body(q_ref, k_ref, v_ref, o_ref, t_ref):
        i = pl.program_id(1)
        q = q_ref[...]
        jj = lax.broadcasted_iota(jnp.int32, (LANES, LANES), 0)
        ss = lax.broadcasted_iota(jnp.int32, (LANES, LANES), 1)
        from_here = _cumsum_matrix(jj >= ss)
        causal = (lax.broadcasted_iota(jnp.int32, (tq, tq), 1) < lax.broadcasted_iota(jnp.int32, (tq, tq), 0))

        def group(k0, masked, tail, acc):
            kb = k_ref[pl.ds(k0, tq), :]
            vb = v_ref[pl.ds(k0, tq), :]
            z = _dot(q, kb, "nt") * scale
            _, lm = _log1m_beta(z)
            if masked:
                lm = jnp.where(causal, lm, 0.0)
            args = [None] * nch
            for ch in reversed(range(nch)):
                sl = slice(ch * LANES, (ch + 1) * LANES)
                cum, tot = _chunk_sums(lm[:, sl], from_here)
                args[ch] = z[:, sl] + (cum + tail)
                tail = tail + tot
            a = jnp.exp(jnp.concatenate(args, axis=1))
            if masked:
                a = jnp.where(causal, a, 0.0)
            return tail, acc + _dot(a.astype(BF16), vb)

        zero = jnp.zeros((tq, HEAD_DIM), F32)
        carry = group(pl.multiple_of(i * tq, tq), True, zero, zero)
        tail, acc = lax.fori_loop(
            0, i, lambda it, cr: group(pl.multiple_of((i - 1 - it) * tq, tq), False, *cr), carry)
        o_ref[...] = acc.astype(BF16)
        t_ref[...] = tail

    return _pcall(
        body, name=name, grid=(H, S // tq),
        in_specs=[pl.BlockSpec((None, tq, HEAD_DIM), lambda h, i: (0, i, h)),
                  pl.BlockSpec((None, S, HEAD_DIM), lambda h, i: (1, 0, h)),
                  pl.BlockSpec((None, S, HEAD_DIM), lambda h, i: (2, 0, h))],
        out_specs=[pl.BlockSpec((tq, HEAD_DIM), lambda h, i: (i, h)),
                   pl.BlockSpec((tq, HEAD_DIM), lambda h, i: (i, h))],
        out_shape=[_sds((S, D), BF16), _sds((S, D), F32)],
        compiler_params=_cp(("parallel", "parallel")),
    )(qkv3, qkv3, qkv3)


def _attn_bwd(name, qkv3, do, totals):
    _, S, D = qkv3.shape
    H = D // HEAD_DIM
    tq = _tile(S, 512)
    nch = tq // LANES
    nq = S // tq
    scale = HEAD_DIM ** -0.5

    def body(q_ref, k_ref, v_ref, do_ref, t_ref, out_ref, dk_acc, dv_acc):
        i = pl.program_id(1)

        @pl.when(i == 0)
        def _():
            dk_acc[...] = jnp.zeros_like(dk_acc)
            dv_acc[...] = jnp.zeros_like(dv_acc)

        q = q_ref[...]
        do_b = do_ref[...]
        total = t_ref[...]
        jj = lax.broadcasted_iota(jnp.int32, (LANES, LANES), 0)
        ss = lax.broadcasted_iota(jnp.int32, (LANES, LANES), 1)
        before = _cumsum_matrix(jj < ss)
        causal = (lax.broadcasted_iota(jnp.int32, (tq, tq), 1) < lax.broadcasted_iota(jnp.int32, (tq, tq), 0))

        def group(k0, masked, head, dsum, dq):
            kb = k_ref[pl.ds(k0, tq), :]
            vb = v_ref[pl.ds(k0, tq), :]
            z = _dot(q, kb, "nt") * scale
            e, lm = _log1m_beta(z)
            if masked:
                lm = jnp.where(causal, lm, 0.0)
            args = []
            for ch in range(nch):
                sl = slice(ch * LANES, (ch + 1) * LANES)
                cum, tot = _chunk_sums(lm[:, sl], before)
                args.append(z[:, sl] + (total - head - cum))
                head = head + tot
            a = jnp.exp(jnp.concatenate(args, axis=1))
            if masked:
                a = jnp.where(causal, a, 0.0)
            de = _dot(do_b, vb, "nt") * a
            dv_acc[pl.ds(k0, tq), :] += _dot(a.astype(BF16), do_b, "tn")
            sums = []
            for ch in range(nch):
                sl = slice(ch * LANES, (ch + 1) * LANES)
                cum, tot = _chunk_sums(de[:, sl], before)
                sums.append(dsum + cum)
                dsum = dsum + tot
            inv = 1.0 / (1.0 + e)
            sig = jnp.where(z >= 0.0, inv, e * inv)
            dz = (de - sig * (de + jnp.concatenate(sums, axis=1))) * scale
            if masked:
                dz = jnp.where(causal, dz, 0.0)
            dzb = dz.astype(BF16)
            dk_acc[pl.ds(k0, tq), :] += _dot(dzb, q, "tn")
            return head, dsum, dq + _dot(dzb, kb)

        zero = jnp.zeros((tq, HEAD_DIM), F32)
        carry = lax.fori_loop(
            0, i, lambda g, cr: group(pl.multiple_of(g * tq, tq), False, *cr), (zero, zero, zero))
        q0 = pl.multiple_of(i * tq, tq)
        _, _, dq = group(q0, True, *carry)
        out_ref[0, pl.ds(q0, tq), :] = dq.astype(BF16)

        @pl.when(i == nq - 1)
        def _():
            out_ref[1] = dk_acc[...].astype(BF16)
            out_ref[2] = dv_acc[...].astype(BF16)

    return _pcall(
        body, name=name, grid=(H, nq),
        in_specs=[pl.BlockSpec((None, tq, HEAD_DIM), lambda h, i: (0, i, h)),
                  pl.BlockSpec((None, S, HEAD_DIM), lambda h, i: (1, 0, h)),
                  pl.BlockSpec((None, S, HEAD_DIM), lambda h, i: (2, 0, h)),
                  pl.BlockSpec((tq, HEAD_DIM), lambda h, i: (i, h)),
                  pl.BlockSpec((tq, HEAD_DIM), lambda h, i: (i, h))],
        out_specs=pl.BlockSpec((3, S, HEAD_DIM), lambda h, i: (0, 0, h)),
        out_shape=_sds((3, S, D), BF16),
        scratch_shapes=[pltpu.VMEM((S, HEAD_DIM), F32), pltpu.VMEM((S, HEAD_DIM), F32)],
        compiler_params=_cp(("parallel", "arbitrary")),
    )(qkv3, qkv3, qkv3, do, totals)


def _conv_taps(g, t):
    S = g.shape[0]
    g1 = jnp.where(t >= 1, pltpu.roll(g, 1, 0), 0.0)
    g2 = jnp.where(t >= 2, pltpu.roll(g, 2, 0), 0.0)
    return g1, g2


def _conv_fwd(name, bcu3, cw, cb):
    _, S, D = bcu3.shape
    tf = LANES

    def body(x_ref, w_ref, b_ref, p_ref):
        t = lax.broadcasted_iota(jnp.int32, (S, tf), 0)
        g = x_ref[1] * x_ref[2]
        g1, g2 = _conv_taps(g, t)
        y = w_ref[0:1, :] * g2 + w_ref[1:2, :] * g1 + w_ref[2:3, :] * g + b_ref[...]
        p_ref[...] = (x_ref[0] * y).astype(BF16)

    return _pcall(
        body, name=name, grid=(D // tf,),
        in_specs=[pl.BlockSpec((3, S, tf), lambda f: (0, 0, f)), pl.BlockSpec((3, tf), lambda f: (0, f)),
                  pl.BlockSpec((1, tf), lambda f: (0, f))],
        out_specs=pl.BlockSpec((S, tf), lambda f: (0, f)), out_shape=_sds((S, D), BF16),
        compiler_params=_cp(("parallel",)),
    )(bcu3, cw, cb)


def _conv_bwd(name, bcu3, dp, cw, cb):
    _, S, D = bcu3.shape
    tf = LANES

    def body(x_ref, dp_ref, w_ref, b_ref, o_ref, vec_ref):
        t = lax.broadcasted_iota(jnp.int32, (S, tf), 0)
        bg, cg, u = x_ref[0], x_ref[1], x_ref[2]
        g = cg * u
        g1, g2 = _conv_taps(g, t)
        w0, w1, w2 = w_ref[0:1, :], w_ref[1:2, :], w_ref[2:3, :]
        y = w0 * g2 + w1 * g1 + w2 * g + b_ref[...]
        dpv = dp_ref[...]
        o_ref[0] = (dpv * y).astype(BF16)
        dy = dpv * bg

        def colsum(a):
            return jnp.sum(a, axis=0, keepdims=True)

        vec_ref[...] = jnp.zeros_like(vec_ref)
        for j, row_j in enumerate((dy, dy * g2, dy * g1, dy * g)):
            vec_ref[j:j + 1, :] = colsum(row_j)
        dy1 = jnp.where(t < S - 1, pltpu.roll(dy, S - 1, 0), 0.0)
        dy2 = jnp.where(t < S - 2, pltpu.roll(dy, S - 2, 0), 0.0)
        dg = w2 * dy + w1 * dy1 + w0 * dy2
        o_ref[1] = (dg * u).astype(BF16)
        o_ref[2] = (dg * cg).astype(BF16)

    return _pcall(
        body, name=name, grid=(D // tf,),
        in_specs=[pl.BlockSpec((3, S, tf), lambda f: (0, 0, f)), pl.BlockSpec((S, tf), lambda f: (0, f)),
                  pl.BlockSpec((3, tf), lambda f: (0, f)), pl.BlockSpec((1, tf), lambda f: (0, f))],
        out_specs=[pl.BlockSpec((3, S, tf), lambda f: (0, 0, f)), pl.BlockSpec((8, tf), lambda f: (0, f))],
        out_shape=[_sds((3, S, D), BF16), _sds((8, D), F32)],
        compiler_params=_cp(("parallel",)),
    )(bcu3, dp, cw, cb)


def _adamw(name, w3, m3, v3, l, parts, carry=None):
    L, R, C = w3.shape
    budget_rows = (1 << 18) // C
    tr = R if R <= budget_rows else _tile(R, 1 << (budget_rows.bit_length() - 1))
    n_parts = len(parts)
    n_carry = 0 if carry is None else 4
    c1 = 1.0 / (1.0 - ADAM_B1 ** ADAM_STEP)
    c2 = 1.0 / (1.0 - ADAM_B2 ** ADAM_STEP)

    def body(*refs):
        w_ref, m_ref, v_ref = refs[:3]
        p_refs = refs[3:3 + n_parts]
        og, od, om, ov = refs[3 + n_parts + n_carry:]
        g = p_refs[0][...].astype(F32)
        for p in p_refs[1:]:
            g = g + p[...].astype(F32)
        m = ADAM_B1 * m_ref[...] + (1.0 - ADAM_B1) * g
        v = ADAM_B2 * v_ref[...] + (1.0 - ADAM_B2) * jnp.square(g)
        og[...] = g
        om[...] = m
        ov[...] = v
        od[...] = -ADAM_LR * ((m * c1) / (jnp.sqrt(v * c2) + ADAM_EPS) + ADAM_WD * w_ref[...])

    lay = pl.BlockSpec((None, tr, C), lambda r: (l, r, 0))
    in_specs = [lay, lay, lay]
    args = [w3, m3, v3]
    for arr, idx in parts:
        in_specs.append(pl.BlockSpec((None, tr, C), functools.partial(lambda r, j: (j, r, 0), j=idx)))
        args.append(arr)
    aliases = {}
    if carry is not None:
        for j, buf in enumerate(carry):
            aliases[len(args)] = j
            in_specs.append(ANY_SPEC)
            args.append(buf)
    return _pcall(body, name=name, grid=(R // tr,), in_specs=in_specs, out_specs=[lay] * 4,
                  out_shape=[_sds((L, R, C), F32)] * 4, input_output_aliases=aliases,
                  compiler_params=_cp(("parallel",)))(*args)


def kernel(x, c, mod_w, mod_b, ln_g, ln_b, attn_w_qkv, attn_w_o, conv_w_in, conv_w, conv_b, conv_w_out, mlp_w1, mlp_b1, mlp_w2, mlp_b2, loss_target, m_mod_w, m_mod_b, m_ln_g, m_ln_b, m_attn_w_qkv, m_attn_w_o, m_conv_w_in, m_conv_w, m_conv_b, m_conv_w_out, m_mlp_w1, m_mlp_b1, m_mlp_w2, m_mlp_b2, v_mod_w, v_mod_b, v_ln_g, v_ln_b, v_attn_w_qkv, v_attn_w_o, v_conv_w_in, v_conv_w, v_conv_b, v_conv_w_out, v_mlp_w1, v_mlp_b1, v_mlp_w2, v_mlp_b2):
    xs, target = x[0], loss_target[0]
    S, D = xs.shape
    L = mod_w.shape[0]
    LA, LC = attn_w_qkv.shape[0], conv_w_in.shape[0]
    F = mlp_b1.shape[1]
    DB, FB, NC = D // N_DEV, F // N_DEV, mod_w.shape[2]
    CT = DB
    me = 4 * lax.axis_index("x") + 2 * lax.axis_index("y") + lax.axis_index("c")
    me_arr = jnp.reshape(me, (1,)).astype(jnp.int32)

    first_gather = _ag_start([_cast_bf16(attn_w_qkv, 0, "cast_win_0"), _cast_bf16(attn_w_o, 0, "cast_wout_0")],
                             [], "ag_start_0")

    small = [c, ln_g, ln_b, conv_w, conv_b]
    pack = jnp.concatenate([a.reshape(-1) for a in small]).reshape(-1, LANES)
    gath = _allgather([pack], "ag_small_params", deps=[first_gather[4]])[0].reshape(N_DEV, -1)
    offs, pos = [], 0
    for a in small:
        offs.append((pos, a.size))
        pos += a.size

    def unshard(idx, lead):
        o, n = offs[idx]
        a = gath[:, o:o + n].reshape((N_DEV,) + lead + (DB,))
        return jnp.moveaxis(a, 0, -2).reshape(lead + (D,))

    c_all = gath[:, :D]
    lng_full, lnb_full = unshard(1, (L, 2)), unshard(2, (L, 2))
    cw_full, cb_full = unshard(3, (LC, 3)), unshard(4, (LC,))

    cond16 = _silu_bf16(jnp.pad(c_all, ((0, 16 - N_DEV), (0, 0))), "silu_cond")
    tn_mod = NC // 3
    bias_mod = lax.dynamic_slice_in_dim(mod_b, me * NC, NC, axis=1).reshape(L, 1, NC)
    mod_part = _matmul(
        "mod_fwd", "nn", (L, NC // tn_mod), 1,
        cond16, pl.BlockSpec((16, D), lambda l, n: (0, 0)),
        mod_w, pl.BlockSpec((None, D, tn_mod), lambda l, n: (l, 0, n)),
        [_sds((L, 16, NC), F32)], [pl.BlockSpec((None, 16, tn_mod), lambda l, n: (l, 0, n))], None,
        extras=[bias_mod], extra_specs=[pl.BlockSpec((None, 1, tn_mod), lambda l, n: (l, 0, n))])[0]
    mod_gath = _allgather([mod_part[:, :N_DEV].reshape(-1, LANES)], "ag_mod")[0].reshape(N_DEV, L, N_DEV, NC)
    mod_me = lax.dynamic_index_in_dim(mod_gath, me, axis=2, keepdims=False)
    mod_me = jnp.moveaxis(mod_me, 0, 1).reshape(L, 6, 1, D)

    def mod_vecs(k):
        l, o = k // 2, 3 * (k % 2)
        return mod_me[l, o], mod_me[l, o + 1], mod_me[l, o + 2]

    def mixer_weights(l):
        if l % 2 == 0:
            return attn_w_qkv, attn_w_o, l // 2
        return conv_w_in, conv_w_out, l // 2

    K = 2 * L

    def group_shards(k):
        l = k // 2
        if k % 2 == 0:
            w_in3, w_out3, j = mixer_weights(l)
            return [_cast_bf16(w_in3, j, f"cast_win_{l}"), _cast_bf16(w_out3, j, f"cast_wout_{l}")]
        return [_cast_bf16(mlp_w1, l, f"cast_w1_{l}"), _cast_bf16(mlp_w2, l, f"cast_w2_{l}")]

    wg = [[None] * 4 for _ in range(L)]
    in_flight = {0: first_gather}

    def fetch_group(k, after):
        shards, lands = _ag_wait(in_flight.pop(k), after, f"ag_wait_{k}")
        deps = []
        if k + 1 < K:
            in_flight[k + 1] = _ag_start(group_shards(k + 1), [shards[0]], f"ag_start_{k + 1}")
            deps = [in_flight[k + 1][4]]
        first, second = _ag_finish(shards, lands, deps, f"ag_finish_{k}")
        rows = D if k % 2 == 0 else F
        wg[k // 2][2 * (k % 2)], wg[k // 2][2 * (k % 2) + 1] = first, second.reshape(rows, D)

    n_ct = 3 * D // CT
    tm_big = _tile(S, 2048)

    def in_proj(l, h, out_dtype):
        return _matmul(
            f"in_proj_{l}", "nn", (S // tm_big, n_ct), 1,
            h, pl.BlockSpec((tm_big, D), lambda i, ct: (i, 0)),
            wg[l][0], pl.BlockSpec((None, D, CT), lambda i, ct: (ct // 3, 0, ct % 3)),
            [_sds((3, S, D), out_dtype)],
            [pl.BlockSpec((None, tm_big, CT), lambda i, ct: (ct // N_DEV, i, ct % N_DEV))], None)[0]

    tm, tn_half = _tile(S, 512), _tile(D, 1024)

    def out_proj(l, a):
        return _matmul(
            f"out_proj_{l}", "nn", (D // tn_half, S // tm), 1,
            a, pl.BlockSpec((tm, D), lambda n, i: (i, 0)),
            wg[l][1], pl.BlockSpec((D, tn_half), lambda n, i: (0, n)),
            [_sds((S, D), F32)], [pl.BlockSpec((tm, tn_half), lambda n, i: (i, n))], None)[0]

    def mlp_up(l, h):
        blk = pl.BlockSpec((tm, FB), lambda j, i: (i, j))
        return _matmul(
            f"mlp_up_{l}", "nn", (N_DEV, S // tm), 1,
            h, pl.BlockSpec((tm, D), lambda j, i: (i, 0)),
            wg[l][2], pl.BlockSpec((None, D, FB), lambda j, i: (j, 0, 0)),
            [_sds((S, F), BF16), _sds((S, F), BF16)], [blk, blk], None, epilogue="relu2",
            extras=[mlp_b1[l].reshape(1, F)], extra_specs=[pl.BlockSpec((1, FB), lambda j, i: (0, j))])

    tk_f = _tile(F, 2048)

    def mlp_down(l, act):
        return _matmul(
            f"mlp_down_{l}", "nn", (D // tn_half, S // tm, F // tk_f), F // tk_f,
            act, pl.BlockSpec((tm, tk_f), lambda n, i, k: (i, k)),
            wg[l][3], pl.BlockSpec((tk_f, tn_half), lambda n, i, k: (k, n)),
            [_sds((S, D), F32)], [pl.BlockSpec((tm, tn_half), lambda n, i, k: (i, n))], (tm, tn_half),
            extras=[mlp_b2[l].reshape(1, D)], extra_specs=[pl.BlockSpec((1, tn_half), lambda n, i, k: (0, n))])[0]

    xin, ys, hs, saved = [xs], [], [], []
    sh0, sc0, _ = mod_vecs(0)
    h = _modulate("modulate_in", xs, sc0, sh0)
    for k in range(K):
        l = k // 2
        hs.append(h)
        fetch_group(k, h)
        if k % 2 == 0 and l % 2 == 0:
            qkv3 = in_proj(l, h, BF16)
            o, totals = _attn_fwd(f"attn_fwd_{l}", qkv3)
            y = out_proj(l, o)
            saved.append((qkv3, o, totals))
        elif k % 2 == 0:
            bcu3 = in_proj(l, h, F32)
            j = l // 2
            p = _conv_fwd(f"conv_fwd_{l}", bcu3, cw_full[j], cb_full[j].reshape(1, D))
            y = out_proj(l, p)
            saved.append((bcu3, p))
        else:
            u, act = mlp_up(l, h)
            y = mlp_down(l, act)
            saved.append((u, act))
        ys.append(y)
        if k + 1 < K:
            _, _, gate = mod_vecs(k)
            sh_n, sc_n, _ = mod_vecs(k + 1)
            x_next, h = _ln_fwd(f"ln_fwd_{k}", xin[k], y, gate, lng_full[l, k % 2].reshape(1, D),
                                lnb_full[l, k % 2].reshape(1, D), sc_n, sh_n)
            xin.append(x_next)

    tk_s = _tile(S, 1024)
    tm_d = _tile(D, 1024)

    def grad_rows(name, a, dy, rows):
        n_m = rows // tm_d
        return _matmul(
            name, "tn", (n_m, D // tn_half, S // tk_s), S // tk_s,
            a, pl.BlockSpec((tk_s, tm_d), lambda m, n, k: (k, m)),
            dy, pl.BlockSpec((tk_s, tn_half), lambda m, n, k: (k, n)),
            [_sds((rows, D), BF16)], [pl.BlockSpec((tm_d, tn_half), lambda m, n, k: (m, n))], (tm_d, tn_half))[0]

    def back_out_proj(l, dy, out_dtype):
        return _matmul(
            f"back_out_proj_{l}", "nt", (D // tn_half, S // tm), 1,
            dy, pl.BlockSpec((tm, D), lambda n, i: (i, 0)),
            wg[l][1], pl.BlockSpec((tn_half, D), lambda n, i: (n, 0)),
            [_sds((S, D), out_dtype)], [pl.BlockSpec((tm, tn_half), lambda n, i: (i, n))], None)[0]

    tm_1k = _tile(S, 1024)

    def back_in_proj(l, d3):
        return _matmul(
            f"back_in_proj_{l}", "nt", (S // tm_1k, n_ct), n_ct,
            d3, pl.BlockSpec((None, tm_1k, CT), lambda i, ct: (ct // N_DEV, i, ct % N_DEV)),
            wg[l][0], pl.BlockSpec((None, D, CT), lambda i, ct: (ct // 3, 0, ct % 3)),
            [_sds((S, D), F32)], [pl.BlockSpec((tm_1k, D), lambda i, ct: (i, 0))], (tm_1k, D))[0]

    def grad_in_proj(l, h, d3):
        return _matmul(
            f"grad_in_proj_{l}", "tn", (n_ct, S // tk_s), S // tk_s,
            h, pl.BlockSpec((tk_s, D), lambda ct, k: (k, 0)),
            d3, pl.BlockSpec((None, tk_s, CT), lambda ct, k: (ct // N_DEV, k, ct % N_DEV)),
            [_sds((N_DEV, D, 3 * CT), BF16)],
            [pl.BlockSpec((None, D, CT), lambda ct, k: (ct // 3, 0, ct % 3))], (D, CT))[0]

    def back_mlp_down(l, dy, u):
        blk = pl.BlockSpec((tm, FB), lambda j, i: (i, j))
        return _matmul(
            f"back_mlp_down_{l}", "nt", (N_DEV, S // tm), 1,
            dy, pl.BlockSpec((tm, D), lambda j, i: (i, 0)),
            wg[l][3], pl.BlockSpec((FB, D), lambda j, i: (j, 0)),
            [_sds((S, F), BF16), _sds((1, F), F32)], [blk, pl.BlockSpec((1, FB), lambda j, i: (0, j))], None,
            epilogue="drelu2", extras=[u], extra_specs=[blk], sem=("parallel", "arbitrary"))

    def back_mlp_up(l, du):
        return _matmul(
            f"back_mlp_up_{l}", "nt", (S // tm_1k, N_DEV), N_DEV,
            du, pl.BlockSpec((tm_1k, FB), lambda i, j: (i, j)),
            wg[l][2], pl.BlockSpec((None, D, FB), lambda i, j: (j, 0, 0)),
            [_sds((S, D), F32)], [pl.BlockSpec((tm_1k, D), lambda i, j: (i, 0))], (tm_1k, D))[0]

    def grad_mlp_up(l, h, du):
        return _matmul(
            f"grad_mlp_up_{l}", "tn", (N_DEV, D // tm_d, S // tk_s), S // tk_s,
            h, pl.BlockSpec((tk_s, tm_d), lambda j, m, k: (k, m)),
            du, pl.BlockSpec((tk_s, FB), lambda j, m, k: (k, j)),
            [_sds((N_DEV, D, FB), BF16)], [pl.BlockSpec((None, tm_d, FB), lambda j, m, k: (j, m, 0))],
            (tm_d, FB))[0]

    dmod = [[None] * 6 for _ in range(L)]
    dlng = [[None, None] for _ in range(L)]
    dlnb = [[None, None] for _ in range(L)]
    db1, db2 = [None] * L, [None] * L
    dcw, dcb = [None] * LC, [None] * LC

    carries = {"attn_in": None, "attn_out": None, "conv_in": None, "conv_out": None, "w1": None, "w2": None}
    moments = {"attn_in": (attn_w_qkv, m_attn_w_qkv, v_attn_w_qkv), "attn_out": (attn_w_o, m_attn_w_o, v_attn_w_o),
               "conv_in": (conv_w_in, m_conv_w_in, v_conv_w_in), "conv_out": (conv_w_out, m_conv_w_out, v_conv_w_out),
               "w1": (mlp_w1, m_mlp_w1, v_mlp_w1), "w2": (mlp_w2, m_mlp_w2, v_mlp_w2)}
    pending = []

    def finish_pending(after):
        if not pending:
            return
        k, started = pending.pop()
        ps, r2 = _rs_chips_wait(started, after, f"rs_wait_{k}")
        l = k // 2
        if k % 2 == 1:
            keys = [("w1", l), ("w2", l)]
        else:
            kind = "attn" if l % 2 == 0 else "conv"
            keys = [(kind + "_in", l // 2), (kind + "_out", l // 2)]
        for t, (key, idx) in enumerate(keys):
            w3, m3, v3 = moments[key]
            parts = [(ps[t], 0), (r2[t], 1), (r2[t], 2), (r2[t], 3)]
            carries[key] = _adamw(f"adamw_{key}_{l}", w3, m3, v3, idx, parts, carries[key])

    def reduce_group(k, gs, after):
        finish_pending(after)
        r1 = _rs_sibling(gs, f"rs_sibling_{k}")
        ps = [_rs_add(g, r, me_arr, f"rs_add_{k}_{t}") for t, (g, r) in enumerate(zip(gs, r1))]
        started = _rs_chips_start(ps, f"rs_start_{k}")
        pending.append((k, started))
        return started[4]

    def take_ln_rows(k, vec):
        l, o = k // 2, 3 * (k % 2)
        dlng[l][k % 2], dlnb[l][k % 2] = vec[2], vec[3]
        dmod[l][o + 2] = vec[4]
        if k % 2 == 1:
            db2[l] = vec[5]

    def take_mod_rows(k, vec):
        l, o = k // 2, 3 * (k % 2)
        dmod[l][o + 1], dmod[l][o] = vec[0], vec[1]

    lk = K - 1
    _, _, gate = mod_vecs(lk)
    dy, dxr, vec = _boundary(f"boundary_{lk}", "last", (target, xin[lk], ys[lk]),
                             (gate, lng_full[lk // 2, 1].reshape(1, D), lnb_full[lk // 2, 1].reshape(1, D)))
    take_ln_rows(lk, vec)
    loss_local = 0.5 * vec[6, 0] / D
    grad_x = None
    for k in range(K - 1, -1, -1):
        l = k // 2
        if k % 2 == 1:
            u, act = saved[k]
            g_w2 = grad_rows(f"grad_mlp_down_{l}", act, dy, F).reshape(N_DEV, FB, D)
            du, db1_l = back_mlp_down(l, dy, u)
            db1[l] = db1_l[0]
            g_w1 = grad_mlp_up(l, hs[k], du)
            dh = back_mlp_up(l, du)
            token = reduce_group(k, [g_w1, g_w2], dh)
        else:
            if l % 2 == 0:
                qkv3, o, totals = saved[k]
                g_out = grad_rows(f"grad_out_proj_{l}", o, dy, D).reshape(N_DEV, DB, D)
                do = back_out_proj(l, dy, BF16)
                d3 = _attn_bwd(f"attn_bwd_{l}", qkv3, do, totals)
            else:
                bcu3, p = saved[k]
                j = l // 2
                g_out = grad_rows(f"grad_out_proj_{l}", p, dy, D).reshape(N_DEV, DB, D)
                dp = back_out_proj(l, dy, F32)
                d3, cvec = _conv_bwd(f"conv_bwd_{l}", bcu3, dp, cw_full[j], cb_full[j].reshape(1, D))
                dcb[j], dcw[j] = cvec[0], cvec[1:4]
            g_in = grad_in_proj(l, hs[k], d3)
            dh = back_in_proj(l, d3)
            token = reduce_group(k, [g_in, g_out], dh)
        sh_k, sc_k, _ = mod_vecs(k)
        if k > 0:
            kp = k - 1
            _, _, gate = mod_vecs(kp)
            dy, dxr, vec = _boundary(
                f"boundary_{kp}", "mid", (dxr, dh, xin[kp], ys[kp]),
                (sc_k, gate, lng_full[kp // 2, kp % 2].reshape(1, D), lnb_full[kp // 2, kp % 2].reshape(1, D)),
                deps=[token])
            take_mod_rows(k, vec)
            take_ln_rows(kp, vec)
        else:
            grad_x, vec = _boundary("boundary_in", "first", (dxr, dh, xs), (sc_k,), deps=[token])
            take_mod_rows(k, vec)
    finish_pending(grad_x)
    results = carries

    flat = [jnp.stack([jnp.stack(r) for r in dmod]).reshape(-1), jnp.stack(db1).reshape(-1),
            jnp.stack(db2).reshape(-1), jnp.stack([jnp.stack(r) for r in dlng]).reshape(-1),
            jnp.stack([jnp.stack(r) for r in dlnb]).reshape(-1), jnp.stack(dcw).reshape(-1),
            jnp.stack(dcb).reshape(-1)]
    sizes = [a.size for a in flat]
    small_g = _allgather([jnp.concatenate(flat).reshape(-1, LANES)], "ag_small_grads")[0]
    total = _sum_devices(small_g, "sum_small_grads").reshape(-1)
    starts = [sum(sizes[:i]) for i in range(len(sizes))]

    def piece(i, shape):
        return total[starts[i]:starts[i] + sizes[i]].reshape(shape)

    def my_block(a):
        return lax.dynamic_slice_in_dim(a, me * DB, DB, axis=a.ndim - 1)

    g_mod_b = piece(0, (L, 6 * D))
    g_b1, g_b2 = piece(1, (L, F)), piece(2, (L, D))
    g_lng, g_lnb = my_block(piece(3, (L, 2, D))), my_block(piece(4, (L, 2, D)))
    g_cw, g_cb = my_block(piece(5, (LC, 3, D))), my_block(piece(6, (LC, D)))

    dmod_all = small_g.reshape(N_DEV, -1)[:, :sizes[0]].reshape(N_DEV, L, 6 * D)
    dmod_mine = jnp.moveaxis(lax.dynamic_slice_in_dim(dmod_all, me * NC, NC, axis=2), 0, 1)
    dmod16 = jnp.pad(dmod_mine, ((0, 0), (0, 16 - N_DEV), (0, 0)))
    tm_mod = _tile(D, 512)
    g_mod_w = _matmul(
        "grad_mod_w", "tn", (L, D // tm_mod), 1,
        cond16, pl.BlockSpec((16, tm_mod), lambda l, m: (0, m)),
        dmod16, pl.BlockSpec((None, 16, NC), lambda l, m: (l, 0, 0)),
        [_sds((L, D, NC), F32)], [pl.BlockSpec((None, tm_mod, NC), lambda l, m: (l, m, 0))], None)[0]
    res_mod_w = None
    for l in range(L):
        res_mod_w = _adamw(f"adamw_mod_w_{l}", mod_w, m_mod_w, v_mod_w, l, [(g_mod_w, l)], res_mod_w)

    small_w = [mod_b, ln_g, ln_b, conv_w, conv_b, mlp_b1, mlp_b2]
    small_m = [m_mod_b, m_ln_g, m_ln_b, m_conv_w, m_conv_b, m_mlp_b1, m_mlp_b2]
    small_v = [v_mod_b, v_ln_g, v_ln_b, v_conv_w, v_conv_b, v_mlp_b1, v_mlp_b2]
    small_gr = [g_mod_b, g_lng, g_lnb, g_cw, g_cb, g_b1, g_b2]
    n_small = sum(a.size for a in small_w)
    pad = (-n_small) % (8 * LANES)

    def packed(arrs):
        v = jnp.concatenate([a.reshape(-1) for a in arrs])
        return jnp.pad(v, (0, pad)).reshape(1, -1, LANES)

    sg, sd, sm, sv = _adamw("adamw_small", packed(small_w), packed(small_m), packed(small_v), 0,
                            [(packed(small_gr), 0)])

    def unpacked(buf):
        out, pos = [], 0
        flat_buf = buf.reshape(-1)
        for a in small_w:
            out.append(flat_buf[pos:pos + a.size].reshape(a.shape))
            pos += a.size
        return out

    small_res = [unpacked(b) for b in (sg, sd, sm, sv)]
    loss = lax.psum(loss_local, ("x", "y", "c"))

    def leaf(kind):
        s = small_res[kind]
        return [res_mod_w[kind], s[0], s[1], s[2], results["attn_in"][kind], results["attn_out"][kind],
                results["conv_in"][kind], s[3], s[4], results["conv_out"][kind], results["w1"][kind], s[5],
                results["w2"][kind], s[6]]

    return (loss, grad_x[None], *leaf(0), *leaf(1), *leaf(2), *leaf(3))
```

```python
import functools

import jax
import jax.numpy as jnp
from jax import lax
from jax.experimental import pallas as pl
from jax.experimental.pallas import tpu as pltpu

F32 = jnp.float32
BF16 = jnp.bfloat16
N_DEV = 8
HEAD_DIM = 128
LANES = 128
DEEPNORM_ALPHA = 8.0 ** 0.25
LN_EPS = 1e-5
ADAM_LR = 0.001
ADAM_B1 = 0.9
ADAM_B2 = 0.999
ADAM_EPS = 1e-08
ADAM_WD = 0.01
ADAM_STEP = 10
VMEM_LIMIT_BYTES = 52 * 1024 * 1024
MESH = pl.DeviceIdType.MESH
HBM_SPEC = pl.BlockSpec(memory_space=pltpu.HBM)
ANY_SPEC = pl.BlockSpec(memory_space=pl.ANY)
SEM_SPEC = pl.BlockSpec(memory_space=pltpu.SEMAPHORE)
EFFECT = pltpu.SideEffectType.DATAFLOW_SIDE_EFFECTING


def _pcall(body, **kw):
    return pl.pallas_call(body, **kw)


def _cp(sem=None):
    return pltpu.CompilerParams(dimension_semantics=sem, vmem_limit_bytes=VMEM_LIMIT_BYTES)


def _tile(n, pref):
    t = min(n, pref)
    assert n % t == 0, (n, pref)
    return t


def _sds(shape, dtype):
    return jax.ShapeDtypeStruct(tuple(shape), dtype)


def _allgather(xs, name, deps=()):
    n, nd = len(xs), len(deps)

    def body(*refs):
        x_refs = refs[:n]
        out_refs = refs[n + nd:2 * n + nd]
        send_sems, recv_sems, local_sems = refs[2 * n + nd:]
        x, y, c = lax.axis_index("x"), lax.axis_index("y"), lax.axis_index("c")
        me, sibling = (x, y, c), (x, y, 1 - c)
        chips = [(1 - x, y), (x, 1 - y), (1 - x, 1 - y)]

        def slot(t, px, py, pc):
            return out_refs[t].at[4 * px + 2 * py + pc]

        def copy(t, k, block, to, src=None):
            return pltpu.make_async_remote_copy(
                src_ref=slot(t, *block) if src is None else src, dst_ref=slot(t, *block),
                send_sem=send_sems.at[t, k], recv_sem=recv_sems.at[t, k],
                device_id=to, device_id_type=MESH)

        mine = [pltpu.make_async_copy(x_refs[t], slot(t, *me), local_sems.at[t]) for t in range(n)]
        for cp in mine:
            cp.start()
        first = []
        for t in range(n):
            first.append(copy(t, 0, me, sibling, src=x_refs[t]))
            for j, chip in enumerate(chips):
                first.append(copy(t, 1 + j, me, (*chip, c), src=x_refs[t]))
        for cp in first:
            cp.start()
        passed = []
        for j, chip in enumerate(chips):
            for t in range(n):
                copy(t, 1 + j, (*chip, c), me).wait_recv()
                fwd = copy(t, 4 + j, (*chip, c), sibling)
                fwd.start()
                passed.append(fwd)
        for t in range(n):
            copy(t, 0, sibling, me).wait_recv()
            for j, chip in enumerate(chips):
                copy(t, 4 + j, (*chip, 1 - c), me).wait_recv()
        for cp in first + passed:
            cp.wait_send()
        for cp in mine:
            cp.wait()

    outs = _pcall(
        body, name=name,
        out_shape=[_sds((N_DEV,) + a.shape, a.dtype) for a in xs],
        in_specs=[HBM_SPEC] * n + [ANY_SPEC] * nd, out_specs=[HBM_SPEC] * n,
        scratch_shapes=[pltpu.SemaphoreType.DMA((n, 7)), pltpu.SemaphoreType.DMA((n, 7)),
                        pltpu.SemaphoreType.DMA((n,))],
    )(*xs, *deps)
    return list(outs)


def _peers(x, y, c):
    return [(x, y, 1 - c), (1 - x, y, c), (x, 1 - y, c), (1 - x, 1 - y, c)]


def _block_owner(x, y, c, k):
    return ((1 - x) if k >> 1 else x, (1 - y) if k & 1 else y, c)


def _hbm(a):
    return pltpu.with_memory_space_constraint(a, pltpu.HBM)


def _ag_start(xs, deps, name):
    n, nd = len(xs), len(deps)

    def body(*refs):
        x_refs, land_refs = refs[:n], refs[n:2 * n]
        send_sems, recv_sems = refs[2 * n + nd], refs[2 * n + nd + 1]
        token = refs[-1]
        x, y, c = lax.axis_index("x"), lax.axis_index("y"), lax.axis_index("c")
        for t in range(n):
            for k, to in enumerate(_peers(x, y, c)):
                pltpu.make_async_remote_copy(
                    src_ref=x_refs[t], dst_ref=land_refs[t].at[4 * x + 2 * y + c],
                    send_sem=send_sems.at[4 * t + k], recv_sem=recv_sems.at[4 * t + k],
                    device_id=to, device_id_type=MESH).start()
        token[...] = jnp.zeros_like(token)

    lands = [lax.empty((N_DEV,) + a.shape, a.dtype) for a in xs]
    outs = _pcall(
        body, name=name,
        out_shape=(pltpu.SemaphoreType.DMA((4 * n,)), pltpu.SemaphoreType.DMA((4 * n,)),
                   *[pltpu.HBM(a.shape, a.dtype) for a in xs], *[pltpu.HBM(a.shape, a.dtype) for a in lands],
                   _sds((8, LANES), F32)),
        in_specs=[HBM_SPEC] * (2 * n) + [ANY_SPEC] * nd,
        out_specs=(SEM_SPEC, SEM_SPEC, *[HBM_SPEC] * (2 * n), pl.BlockSpec(memory_space=pltpu.VMEM)),
        input_output_aliases={i: 2 + i for i in range(2 * n)},
        compiler_params=pltpu.CompilerParams(has_side_effects=EFFECT),
    )(*[_hbm(a) for a in xs], *[_hbm(a) for a in lands], *deps)
    return outs[0], outs[1], list(outs[2:2 + n]), list(outs[2 + n:2 + 2 * n]), outs[-1]


def _ag_wait(started, after, name):
    send_sems, recv_sems, xs, lands, _ = started
    n = len(xs)

    def body(*refs):
        x_refs, land_refs = refs[:n], refs[n:2 * n]
        send_sems, recv_sems = refs[2 * n], refs[2 * n + 1]
        x, y, c = lax.axis_index("x"), lax.axis_index("y"), lax.axis_index("c")
        for t in range(n):
            for k, (px, py, pc) in enumerate(_peers(x, y, c)):
                cp = pltpu.make_async_remote_copy(
                    src_ref=x_refs[t], dst_ref=land_refs[t].at[4 * px + 2 * py + pc],
                    send_sem=send_sems.at[4 * t + k], recv_sem=recv_sems.at[4 * t + k],
                    device_id=(px, py, pc), device_id_type=MESH)
                cp.wait_send()
                cp.wait_recv()

    outs = _pcall(
        body, name=name,
        out_shape=(*[pltpu.HBM(a.shape, a.dtype) for a in xs], *[pltpu.HBM(a.shape, a.dtype) for a in lands]),
        in_specs=[HBM_SPEC] * (2 * n) + [SEM_SPEC, SEM_SPEC, ANY_SPEC],
        out_specs=[HBM_SPEC] * (2 * n),
        input_output_aliases={i: i for i in range(2 * n)},
        compiler_params=pltpu.CompilerParams(has_side_effects=EFFECT),
    )(*xs, *lands, send_sems, recv_sems, after)
    return list(outs[:n]), list(outs[n:])


def _ag_finish(xs, lands, deps, name):
    n, nd = len(xs), len(deps)

    def body(*refs):
        x_refs = refs[:n]
        out_refs = refs[2 * n + nd:3 * n + nd]
        send_sems, recv_sems, local_sems = refs[3 * n + nd:]
        x, y, c = lax.axis_index("x"), lax.axis_index("y"), lax.axis_index("c")
        chips = [(1 - x, y), (x, 1 - y), (1 - x, 1 - y)]

        def passing(t, j, pc):
            px, py = chips[j]
            rows = out_refs[t].at[4 * px + 2 * py + pc]
            return pltpu.make_async_remote_copy(
                src_ref=rows, dst_ref=rows, send_sem=send_sems.at[t, j], recv_sem=recv_sems.at[t, j],
                device_id=(x, y, 1 - c), device_id_type=MESH)

        mine = [pltpu.make_async_copy(x_refs[t], out_refs[t].at[4 * x + 2 * y + c], local_sems.at[t])
                for t in range(n)]
        sends = [passing(t, j, c) for t in range(n) for j in range(3)]
        for cp in mine + sends:
            cp.start()
        for t in range(n):
            for j in range(3):
                passing(t, j, 1 - c).wait_recv()
        for cp in sends:
            cp.wait_send()
        for cp in mine:
            cp.wait()

    outs = _pcall(
        body, name=name,
        out_shape=[_sds(a.shape, a.dtype) for a in lands],
        in_specs=[HBM_SPEC] * (2 * n) + [ANY_SPEC] * nd, out_specs=[HBM_SPEC] * n,
        input_output_aliases={n + i: i for i in range(n)},
        scratch_shapes=[pltpu.SemaphoreType.DMA((n, 3)), pltpu.SemaphoreType.DMA((n, 3)),
                        pltpu.SemaphoreType.DMA((n,))],
    )(*xs, *lands, *deps)
    return list(outs)


def _rs_chips_start(ps, name):
    n = len(ps)

    def body(*refs):
        p_refs, land_refs = refs[:n], refs[n:2 * n]
        send_sems, recv_sems = refs[2 * n], refs[2 * n + 1]
        token = refs[-1]
        x, y, c = lax.axis_index("x"), lax.axis_index("y"), lax.axis_index("c")
        for t in range(n):
            for k in range(1, 4):
                pltpu.make_async_remote_copy(
                    src_ref=p_refs[t].at[k], dst_ref=land_refs[t].at[k],
                    send_sem=send_sems.at[3 * t + k - 1], recv_sem=recv_sems.at[3 * t + k - 1],
                    device_id=_block_owner(x, y, c, k), device_id_type=MESH).start()
        token[...] = jnp.zeros_like(token)

    lands = [lax.empty(p.shape, p.dtype) for p in ps]
    outs = _pcall(
        body, name=name,
        out_shape=(pltpu.SemaphoreType.DMA((3 * n,)), pltpu.SemaphoreType.DMA((3 * n,)),
                   *[pltpu.HBM(a.shape, a.dtype) for a in ps], *[pltpu.HBM(a.shape, a.dtype) for a in lands],
                   _sds((8, LANES), F32)),
        in_specs=[HBM_SPEC] * (2 * n),
        out_specs=(SEM_SPEC, SEM_SPEC, *[HBM_SPEC] * (2 * n), pl.BlockSpec(memory_space=pltpu.VMEM)),
        input_output_aliases={i: 2 + i for i in range(2 * n)},
        compiler_params=pltpu.CompilerParams(has_side_effects=EFFECT),
    )(*[_hbm(a) for a in ps], *[_hbm(a) for a in lands])
    return outs[0], outs[1], list(outs[2:2 + n]), list(outs[2 + n:2 + 2 * n]), outs[-1]


def _rs_chips_wait(started, after, name):
    send_sems, recv_sems, ps, lands, _ = started
    n = len(ps)

    def body(*refs):
        p_refs, land_refs = refs[:n], refs[n:2 * n]
        send_sems, recv_sems = refs[2 * n], refs[2 * n + 1]
        x, y, c = lax.axis_index("x"), lax.axis_index("y"), lax.axis_index("c")
        for t in range(n):
            for k in range(1, 4):
                cp = pltpu.make_async_remote_copy(
                    src_ref=p_refs[t].at[k], dst_ref=land_refs[t].at[k],
                    send_sem=send_sems.at[3 * t + k - 1], recv_sem=recv_sems.at[3 * t + k - 1],
                    device_id=_block_owner(x, y, c, k), device_id_type=MESH)
                cp.wait_send()
                cp.wait_recv()

    outs = _pcall(
        body, name=name,
        out_shape=(*[pltpu.HBM(a.shape, a.dtype) for a in ps], *[pltpu.HBM(a.shape, a.dtype) for a in lands]),
        in_specs=[HBM_SPEC] * (2 * n) + [SEM_SPEC, SEM_SPEC, ANY_SPEC],
        out_specs=[HBM_SPEC] * (2 * n),
        input_output_aliases={i: i for i in range(2 * n)},
        compiler_params=pltpu.CompilerParams(has_side_effects=EFFECT),
    )(*ps, *lands, send_sems, recv_sems, after)
    return list(outs[:n]), list(outs[n:])


def _rs_sibling(gs, name):
    n = len(gs)

    def body(*refs):
        g_refs = refs[:n]
        r_refs = refs[n:2 * n]
        send_sems, recv_sems = refs[2 * n:]
        x, y, c = lax.axis_index("x"), lax.axis_index("y"), lax.axis_index("c")
        me = 4 * x + 2 * y + c
        cps = []
        for t in range(n):
            for k in range(4):
                cps.append(pltpu.make_async_remote_copy(
                    src_ref=g_refs[t].at[jnp.bitwise_xor(2 * k + 1, me)], dst_ref=r_refs[t].at[k],
                    send_sem=send_sems.at[t, k], recv_sem=recv_sems.at[t, k],
                    device_id=(x, y, 1 - c), device_id_type=MESH))
        for cp in cps:
            cp.start()
        for cp in cps:
            cp.wait()

    outs = _pcall(
        body, name=name,
        out_shape=[_sds((4,) + g.shape[1:], g.dtype) for g in gs],
        in_specs=[HBM_SPEC] * n, out_specs=[HBM_SPEC] * n,
        scratch_shapes=[pltpu.SemaphoreType.DMA((n, 4)), pltpu.SemaphoreType.DMA((n, 4))],
    )(*gs)
    return list(outs)


def _rs_add(g, r1, me_arr, name):
    _, R, C = g.shape
    tr = _tile(R, 256)

    def body(me_ref, g_ref, r_ref, o_ref):
        o_ref[...] = (g_ref[...].astype(F32) + r_ref[...].astype(F32)).astype(BF16)

    grid_spec = pltpu.PrefetchScalarGridSpec(
        num_scalar_prefetch=1, grid=(4, R // tr),
        in_specs=[pl.BlockSpec((None, tr, C), lambda k, r, me: (jnp.bitwise_xor(2 * k, me[0]), r, 0)),
                  pl.BlockSpec((None, tr, C), lambda k, r, me: (k, r, 0))],
        out_specs=pl.BlockSpec((None, tr, C), lambda k, r, me: (k, r, 0)))
    return _pcall(body, name=name, grid_spec=grid_spec, out_shape=_sds((4, R, C), BF16),
                  compiler_params=_cp(("parallel", "parallel")))(me_arr, g, r1)


_DIMS = {"nn": (((1,), (0,)), ((), ())), "nt": (((1,), (1,)), ((), ())), "tn": (((0,), (0,)), ((), ()))}


def _dot(a, b, mode="nn"):
    return lax.dot_general(a, b, _DIMS[mode], preferred_element_type=F32)


def _matmul(name, mode, grid, nk, a, a_spec, b, b_spec, out_shapes, out_specs, acc_shape,
            epilogue="plain", extras=(), extra_specs=(), sem=None, widen=None, deps=()):
    n_extra, n_out = len(extras) + len(deps), len(out_shapes)
    extras, extra_specs = list(extras) + list(deps), list(extra_specs) + [ANY_SPEC] * len(deps)

    def body(*refs):
        a_ref, b_ref = refs[0], refs[1]
        ex = refs[2:2 + n_extra - len(deps)]
        outs = refs[2 + n_extra:2 + n_extra + n_out]
        a_tile, b_tile = a_ref[...].astype(BF16), b_ref[...].astype(BF16)
        if widen == "a":
            a_tile = jnp.concatenate([a_tile] + [e[...].astype(BF16) for e in ex], axis=1)
        elif widen == "b":
            b_tile = jnp.concatenate([b_tile] + [e[...].astype(BF16) for e in ex], axis=1)
        part = _dot(a_tile, b_tile, mode)

        def finish(acc):
            if epilogue == "plain":
                if len(ex) and widen is None:
                    acc = acc + ex[0][...]
                outs[0][...] = acc.astype(outs[0].dtype)
            elif epilogue == "relu2":
                u = acc + ex[0][...]
                outs[0][...] = u.astype(outs[0].dtype)
                outs[1][...] = jnp.square(jnp.maximum(u, 0.0)).astype(outs[1].dtype)
            else:
                du = acc * (2.0 * jnp.maximum(ex[0][...].astype(F32), 0.0))
                outs[0][...] = du.astype(outs[0].dtype)
                col = jnp.sum(du, axis=0, keepdims=True)
                inner = pl.program_id(1)

                @pl.when(inner == 0)
                def _():
                    outs[1][...] = col

                @pl.when(inner > 0)
                def _():
                    outs[1][...] += col

        if nk == 1:
            finish(part)
        else:
            acc_ref = refs[2 + n_extra + n_out]
            kidx = pl.program_id(len(grid) - 1)

            @pl.when(kidx == 0)
            def _():
                acc_ref[...] = part

            @pl.when(kidx > 0)
            def _():
                acc_ref[...] += part

            @pl.when(kidx == nk - 1)
            def _():
                finish(acc_ref[...])

    if sem is None:
        sem = ("parallel",) * (len(grid) - (1 if nk > 1 else 0)) + (("arbitrary",) if nk > 1 else ())
    outs = _pcall(
        body, name=name, grid=grid,
        in_specs=[a_spec, b_spec] + list(extra_specs), out_specs=list(out_specs),
        out_shape=list(out_shapes),
        scratch_shapes=[pltpu.VMEM(acc_shape, F32)] if nk > 1 else [],
        compiler_params=_cp(sem),
    )(a, b, *extras)
    return list(outs)


def _cast_bf16(w3, l, name):
    _, R, C = w3.shape
    tr = _tile(R, 256)

    def body(x_ref, o_ref):
        o_ref[...] = x_ref[...].astype(BF16)

    return _pcall(body, name=name, grid=(R // tr,),
                  in_specs=[pl.BlockSpec((None, tr, C), lambda r: (l, r, 0))],
                  out_specs=pl.BlockSpec((tr, C), lambda r: (r, 0)),
                  out_shape=_sds((R, C), BF16), compiler_params=_cp(("parallel",)))(w3)


def _unblock(wb, name):
    n, R, C = wb.shape

    def body(x_ref, o_ref):
        o_ref[...] = x_ref[...]

    return _pcall(body, name=name, grid=(n,),
                  in_specs=[pl.BlockSpec((None, R, C), lambda j: (j, 0, 0))],
                  out_specs=pl.BlockSpec((R, C), lambda j: (0, j)),
                  out_shape=_sds((R, n * C), wb.dtype), compiler_params=_cp(("parallel",)))(wb)


def _silu_bf16(c16, name):
    def body(c_ref, o_ref):
        v = c_ref[...]
        o_ref[...] = (v / (1.0 + jnp.exp(-v))).astype(BF16)

    return _pcall(body, name=name, out_shape=_sds(c16.shape, BF16))(c16)


def _sum_devices(g, name):
    def body(g_ref, o_ref):
        acc = g_ref[0]
        for d in range(1, N_DEV):
            acc = acc + g_ref[d]
        o_ref[...] = acc

    return _pcall(body, name=name, out_shape=_sds(g.shape[1:], F32))(g)


def _modulate(name, x, sc, sh):
    S, D = x.shape
    ts = _tile(S, 256)

    def body(x_ref, sc_ref, sh_ref, h_ref):
        h_ref[...] = (x_ref[...] * (1.0 + sc_ref[...]) + sh_ref[...]).astype(BF16)

    row = pl.BlockSpec((ts, D), lambda i: (i, 0))
    vec = pl.BlockSpec((1, D), lambda i: (0, 0))
    return _pcall(body, name=name, grid=(S // ts,), in_specs=[row, vec, vec], out_specs=row,
                  out_shape=_sds((S, D), BF16), compiler_params=_cp(("parallel",)))(x, sc, sh)


def _ln_stats(r):
    mu = jnp.mean(r, axis=-1, keepdims=True)
    xc = r - mu
    var = jnp.mean(xc * xc, axis=-1, keepdims=True)
    rstd = lax.rsqrt(var + LN_EPS)
    return xc * rstd, rstd


def _ln_fwd(name, xin, y, gate, lng, lnb, sc, sh):
    S, D = xin.shape
    ts = _tile(S, 256)

    def body(x_ref, y_ref, g_ref, lg_ref, lb_ref, sc_ref, sh_ref, xo_ref, h_ref):
        r = DEEPNORM_ALPHA * x_ref[...] + (1.0 + g_ref[...]) * y_ref[...]
        xhat, _ = _ln_stats(r)
        xo = xhat * lg_ref[...] + lb_ref[...]
        xo_ref[...] = xo
        h_ref[...] = (xo * (1.0 + sc_ref[...]) + sh_ref[...]).astype(BF16)

    row = pl.BlockSpec((ts, D), lambda i: (i, 0))
    vec = pl.BlockSpec((1, D), lambda i: (0, 0))
    return _pcall(body, name=name, grid=(S // ts,), in_specs=[row, row] + [vec] * 5,
                  out_specs=[row, row], out_shape=[_sds((S, D), F32), _sds((S, D), BF16)],
                  compiler_params=_cp(("parallel",)))(xin, y, gate, lng, lnb, sc, sh)


def _boundary(name, mode, big, vecs, deps=()):
    S, D = big[0].shape
    ts = _tile(S, 128)
    nb, nv, nd = len(big), len(vecs), len(deps)

    def body(*refs):
        b = [r[...] for r in refs[:nb]]
        v = [r[...] for r in refs[nb:nb + nv]]
        outs = refs[nb + nv + nd:]
        vec_ref = outs[-1]
        i = pl.program_id(0)
        rows = [None] * 8

        def colsum(a):
            return jnp.sum(a, axis=0, keepdims=True)

        if mode == "first":
            dxr, dh, xk = b
            sc = v[0]
            outs[0][...] = dxr + dh * (1.0 + sc)
            rows[0], rows[1] = colsum(dh * xk), colsum(dh)
        else:
            if mode == "mid":
                dxr, dh, xprev, yk = b
                sc, gate, lng, lnb = v
            else:
                target, xprev, yk = b
                gate, lng, lnb = v
            r = DEEPNORM_ALPHA * xprev + (1.0 + gate) * yk
            xhat, rstd = _ln_stats(r)
            if mode == "mid":
                xk = xhat * lng + lnb
                d = dxr + dh * (1.0 + sc)
                rows[0], rows[1] = colsum(dh * xk), colsum(dh)
            else:
                err = xhat * lng + lnb - target
                d = err * (1.0 / D)
                rows[6] = jnp.zeros((1, D), F32) + jnp.sum(err * err)
            rows[2], rows[3] = colsum(d * xhat), colsum(d)
            dxhat = d * lng
            m1 = jnp.mean(dxhat, axis=-1, keepdims=True)
            m2 = jnp.mean(dxhat * xhat, axis=-1, keepdims=True)
            dr = rstd * (dxhat - m1 - xhat * m2)
            dy = (1.0 + gate) * dr
            rows[4], rows[5] = colsum(dr * yk), colsum(dy)
            outs[0][...] = dy.astype(BF16)
            outs[1][...] = DEEPNORM_ALPHA * dr
        @pl.when(i == 0)
        def _():
            vec_ref[...] = jnp.zeros_like(vec_ref)

        for j, row_j in enumerate(rows):
            if row_j is not None:
                vec_ref[j:j + 1, :] += row_j

    row = pl.BlockSpec((ts, D), lambda i: (i, 0))
    vec = pl.BlockSpec((1, D), lambda i: (0, 0))
    vec_out = pl.BlockSpec((8, D), lambda i: (0, 0))
    if mode == "first":
        out_specs, out_shape = [row, vec_out], [_sds((S, D), F32), _sds((8, D), F32)]
    else:
        out_specs = [row, row, vec_out]
        out_shape = [_sds((S, D), BF16), _sds((S, D), F32), _sds((8, D), F32)]
    return _pcall(body, name=name, grid=(S // ts,), in_specs=[row] * nb + [vec] * nv + [ANY_SPEC] * nd,
                  out_specs=out_specs, out_shape=out_shape,
                  compiler_params=_cp(("arbitrary",)))(*big, *vecs, *deps)


def _split_bf16(a):
    hi = a.astype(BF16)
    lo = (a - hi.astype(F32)).astype(BF16)
    return hi, lo


def _cumsum_matrix(tri):
    half = jnp.concatenate([tri.astype(BF16), jnp.ones((LANES, LANES), BF16)], axis=1)
    return jnp.concatenate([half, half], axis=0)


def _chunk_sums(a, mat):
    hi, lo = _split_bf16(a)
    both = _dot(jnp.concatenate([hi, lo], axis=1), mat)
    return both[:, :LANES], both[:, LANES:]


def _log1m_beta(z):
    e = jnp.exp(-jnp.abs(z))
    return e, -(jnp.maximum(z, 0.0) + jnp.log(1.0 + e))


def _attn_fwd(name, qkv3):
    _, S, D = qkv3.shape
    H = D // HEAD_DIM
    tq = _tile(S, 512)
    nch = tq // LANES
    scale = HEAD_DIM ** -0.5

    def body(q_ref, k_ref, v_ref, o_ref, t_ref):
        i = pl.program_id(1)
        q = q_ref[...]
        jj = lax.broadcasted_iota(jnp.int32, (LANES, LANES), 0)
        ss = lax.broadcasted_iota(jnp.int32, (LANES, LANES), 1)
        from_here = _cumsum_matrix(jj >= ss)
        causal = (lax.broadcasted_iota(jnp.int32, (tq, tq), 1) < lax.broadcasted_iota(jnp.int32, (tq, tq), 0))

        def group(k0, masked, tail, acc):
            kb = k_ref[pl.ds(k0, tq), :]
            vb = v_ref[pl.ds(k0, tq), :]
            z = _dot(q, kb, "nt") * scale
            _, lm = _log1m_beta(z)
            if masked:
                lm = jnp.where(causal, lm, 0.0)
            args = [None] * nch
            for ch in reversed(range(nch)):
                sl = slice(ch * LANES, (ch + 1) * LANES)
                cum, tot = _chunk_sums(lm[:, sl], from_here)
                args[ch] = z[:, sl] + (cum + tail)
                tail = tail + tot
            a = jnp.exp(jnp.concatenate(args, axis=1))
            if masked:
                a = jnp.where(causal, a, 0.0)
            return tail, acc + _dot(a.astype(BF16), vb)

        zero = jnp.zeros((tq, HEAD_DIM), F32)
        carry = group(pl.multiple_of(i * tq, tq), True, zero, zero)
        tail, acc = lax.fori_loop(
            0, i, lambda it, cr: group(pl.multiple_of((i - 1 - it) * tq, tq), False, *cr), carry)
        o_ref[...] = acc.astype(BF16)
        t_ref[...] = tail

    return _pcall(
        body, name=name, grid=(H, S // tq),
        in_specs=[pl.BlockSpec((None, tq, HEAD_DIM), lambda h, i: (0, i, h)),
                  pl.BlockSpec((None, S, HEAD_DIM), lambda h, i: (1, 0, h)),
                  pl.BlockSpec((None, S, HEAD_DIM), lambda h, i: (2, 0, h))],
        out_specs=[pl.BlockSpec((tq, HEAD_DIM), lambda h, i: (i, h)),
                   pl.BlockSpec((tq, HEAD_DIM), lambda h, i: (i, h))],
        out_shape=[_sds((S, D), BF16), _sds((S, D), F32)],
        compiler_params=_cp(("parallel", "parallel")),
    )(qkv3, qkv3, qkv3)


def _attn_bwd(name, qkv3, do, totals):
    _, S, D = qkv3.shape
    H = D // HEAD_DIM
    tq = _tile(S, 512)
    nch = tq // LANES
    nq = S // tq
    scale = HEAD_DIM ** -0.5

    def body(q_ref, k_ref, v_ref, do_ref, t_ref, out_ref, dk_acc, dv_acc):
        i = pl.program_id(1)

        @pl.when(i == 0)
        def _():
            dk_acc[...] = jnp.zeros_like(dk_acc)
            dv_acc[...] = jnp.zeros_like(dv_acc)

        q = q_ref[...]
        do_b = do_ref[...]
        total = t_ref[...]
        jj = lax.broadcasted_iota(jnp.int32, (LANES, LANES), 0)
        ss = lax.broadcasted_iota(jnp.int32, (LANES, LANES), 1)
        before = _cumsum_matrix(jj < ss)
        causal = (lax.broadcasted_iota(jnp.int32, (tq, tq), 1) < lax.broadcasted_iota(jnp.int32, (tq, tq), 0))

        def group(k0, masked, head, dsum, dq):
            kb = k_ref[pl.ds(k0, tq), :]
            vb = v_ref[pl.ds(k0, tq), :]
            z = _dot(q, kb, "nt") * scale
            e, lm = _log1m_beta(z)
            if masked:
                lm = jnp.where(causal, lm, 0.0)
            args = []
            for ch in range(nch):
                sl = slice(ch * LANES, (ch + 1) * LANES)
                cum, tot = _chunk_sums(lm[:, sl], before)
                args.append(z[:, sl] + (total - head - cum))
                head = head + tot
            a = jnp.exp(jnp.concatenate(args, axis=1))
            if masked:
                a = jnp.where(causal, a, 0.0)
            de = _dot(do_b, vb, "nt") * a
            dv_acc[pl.ds(k0, tq), :] += _dot(a.astype(BF16), do_b, "tn")
            sums = []
            for ch in range(nch):
                sl = slice(ch * LANES, (ch + 1) * LANES)
                cum, tot = _chunk_sums(de[:, sl], before)
                sums.append(dsum + cum)
                dsum = dsum + tot
            inv = 1.0 / (1.0 + e)
            sig = jnp.where(z >= 0.0, inv, e * inv)
            dz = (de - sig * (de + jnp.concatenate(sums, axis=1))) * scale
            if masked:
                dz = jnp.where(causal, dz, 0.0)
            dzb = dz.astype(BF16)
            dk_acc[pl.ds(k0, tq), :] += _dot(dzb, q, "tn")
            return head, dsum, dq + _dot(dzb, kb)

        zero = jnp.zeros((tq, HEAD_DIM), F32)
        carry = lax.fori_loop(
            0, i, lambda g, cr: group(pl.multiple_of(g * tq, tq), False, *cr), (zero, zero, zero))
        q0 = pl.multiple_of(i * tq, tq)
        _, _, dq = group(q0, True, *carry)
        out_ref[0, pl.ds(q0, tq), :] = dq.astype(BF16)

        @pl.when(i == nq - 1)
        def _():
            out_ref[1] = dk_acc[...].astype(BF16)
            out_ref[2] = dv_acc[...].astype(BF16)

    return _pcall(
        body, name=name, grid=(H, nq),
        in_specs=[pl.BlockSpec((None, tq, HEAD_DIM), lambda h, i: (0, i, h)),
                  pl.BlockSpec((None, S, HEAD_DIM), lambda h, i: (1, 0, h)),
                  pl.BlockSpec((None, S, HEAD_DIM), lambda h, i: (2, 0, h)),
                  pl.BlockSpec((tq, HEAD_DIM), lambda h, i: (i, h)),
                  pl.BlockSpec((tq, HEAD_DIM), lambda h, i: (i, h))],
        out_specs=pl.BlockSpec((3, S, HEAD_DIM), lambda h, i: (0, 0, h)),
        out_shape=_sds((3, S, D), BF16),
        scratch_shapes=[pltpu.VMEM((S, HEAD_DIM), F32), pltpu.VMEM((S, HEAD_DIM), F32)],
        compiler_params=_cp(("parallel", "arbitrary")),
    )(qkv3, qkv3, qkv3, do, totals)


def _conv_taps(g, t):
    S = g.shape[0]
    g1 = jnp.where(t >= 1, pltpu.roll(g, 1, 0), 0.0)
    g2 = jnp.where(t >= 2, pltpu.roll(g, 2, 0), 0.0)
    return g1, g2


def _conv_fwd(name, bcu3, cw, cb):
    _, S, D = bcu3.shape
    tf = LANES

    def body(x_ref, w_ref, b_ref, p_ref):
        t = lax.broadcasted_iota(jnp.int32, (S, tf), 0)
        g = x_ref[1] * x_ref[2]
        g1, g2 = _conv_taps(g, t)
        y = w_ref[0:1, :] * g2 + w_ref[1:2, :] * g1 + w_ref[2:3, :] * g + b_ref[...]
        p_ref[...] = (x_ref[0] * y).astype(BF16)

    return _pcall(
        body, name=name, grid=(D // tf,),
        in_specs=[pl.BlockSpec((3, S, tf), lambda f: (0, 0, f)), pl.BlockSpec((3, tf), lambda f: (0, f)),
                  pl.BlockSpec((1, tf), lambda f: (0, f))],
        out_specs=pl.BlockSpec((S, tf), lambda f: (0, f)), out_shape=_sds((S, D), BF16),
        compiler_params=_cp(("parallel",)),
    )(bcu3, cw, cb)


def _conv_bwd(name, bcu3, dp, cw, cb):
    _, S, D = bcu3.shape
    tf = LANES

    def body(x_ref, dp_ref, w_ref, b_ref, o_ref, vec_ref):
        t = lax.broadcasted_iota(jnp.int32, (S, tf), 0)
        bg, cg, u = x_ref[0], x_ref[1], x_ref[2]
        g = cg * u
        g1, g2 = _conv_taps(g, t)
        w0, w1, w2 = w_ref[0:1, :], w_ref[1:2, :], w_ref[2:3, :]
        y = w0 * g2 + w1 * g1 + w2 * g + b_ref[...]
        dpv = dp_ref[...]
        o_ref[0] = (dpv * y).astype(BF16)
        dy = dpv * bg

        def colsum(a):
            return jnp.sum(a, axis=0, keepdims=True)

        vec_ref[...] = jnp.zeros_like(vec_ref)
        for j, row_j in enumerate((dy, dy * g2, dy * g1, dy * g)):
            vec_ref[j:j + 1, :] = colsum(row_j)
        dy1 = jnp.where(t < S - 1, pltpu.roll(dy, S - 1, 0), 0.0)
        dy2 = jnp.where(t < S - 2, pltpu.roll(dy, S - 2, 0), 0.0)
        dg = w2 * dy + w1 * dy1 + w0 * dy2
        o_ref[1] = (dg * u).astype(BF16)
        o_ref[2] = (dg * cg).astype(BF16)

    return _pcall(
        body, name=name, grid=(D // tf,),
        in_specs=[pl.BlockSpec((3, S, tf), lambda f: (0, 0, f)), pl.BlockSpec((S, tf), lambda f: (0, f)),
                  pl.BlockSpec((3, tf), lambda f: (0, f)), pl.BlockSpec((1, tf), lambda f: (0, f))],
        out_specs=[pl.BlockSpec((3, S, tf), lambda f: (0, 0, f)), pl.BlockSpec((8, tf), lambda f: (0, f))],
        out_shape=[_sds((3, S, D), BF16), _sds((8, D), F32)],
        compiler_params=_cp(("parallel",)),
    )(bcu3, dp, cw, cb)


def _adamw(name, w3, m3, v3, l, parts, carry=None):
    L, R, C = w3.shape
    budget_rows = (1 << 18) // C
    tr = R if R <= budget_rows else _tile(R, 1 << (budget_rows.bit_length() - 1))
    n_parts = len(parts)
    n_carry = 0 if carry is None else 4
    c1 = 1.0 / (1.0 - ADAM_B1 ** ADAM_STEP)
    c2 = 1.0 / (1.0 - ADAM_B2 ** ADAM_STEP)

    def body(*refs):
        w_ref, m_ref, v_ref = refs[:3]
        p_refs = refs[3:3 + n_parts]
        og, od, om, ov = refs[3 + n_parts + n_carry:]
        g = p_refs[0][...].astype(F32)
        for p in p_refs[1:]:
            g = g + p[...].astype(F32)
        m = ADAM_B1 * m_ref[...] + (1.0 - ADAM_B1) * g
        v = ADAM_B2 * v_ref[...] + (1.0 - ADAM_B2) * jnp.square(g)
        og[...] = g
        om[...] = m
        ov[...] = v
        od[...] = -ADAM_LR * ((m * c1) / (jnp.sqrt(v * c2) + ADAM_EPS) + ADAM_WD * w_ref[...])

    lay = pl.BlockSpec((None, tr, C), lambda r: (l, r, 0))
    in_specs = [lay, lay, lay]
    args = [w3, m3, v3]
    for arr, idx in parts:
        in_specs.append(pl.BlockSpec((None, tr, C), functools.partial(lambda r, j: (j, r, 0), j=idx)))
        args.append(arr)
    aliases = {}
    if carry is not None:
        for j, buf in enumerate(carry):
            aliases[len(args)] = j
            in_specs.append(ANY_SPEC)
            args.append(buf)
    return _pcall(body, name=name, grid=(R // tr,), in_specs=in_specs, out_specs=[lay] * 4,
                  out_shape=[_sds((L, R, C), F32)] * 4, input_output_aliases=aliases,
                  compiler_params=_cp(("parallel",)))(*args)


def kernel(x, c, mod_w, mod_b, ln_g, ln_b, attn_w_qkv, attn_w_o, conv_w_in, conv_w, conv_b, conv_w_out, mlp_w1, mlp_b1, mlp_w2, mlp_b2, loss_target, m_mod_w, m_mod_b, m_ln_g, m_ln_b, m_attn_w_qkv, m_attn_w_o, m_conv_w_in, m_conv_w, m_conv_b, m_conv_w_out, m_mlp_w1, m_mlp_b1, m_mlp_w2, m_mlp_b2, v_mod_w, v_mod_b, v_ln_g, v_ln_b, v_attn_w_qkv, v_attn_w_o, v_conv_w_in, v_conv_w, v_conv_b, v_conv_w_out, v_mlp_w1, v_mlp_b1, v_mlp_w2, v_mlp_b2):
    xs, target = x[0], loss_target[0]
    S, D = xs.shape
    L = mod_w.shape[0]
    LA, LC = attn_w_qkv.shape[0], conv_w_in.shape[0]
    F = mlp_b1.shape[1]
    DB, FB, NC = D // N_DEV, F // N_DEV, mod_w.shape[2]
    CT = DB
    me = 4 * lax.axis_index("x") + 2 * lax.axis_index("y") + lax.axis_index("c")
    me_arr = jnp.reshape(me, (1,)).astype(jnp.int32)

    first_gather = _ag_start([_cast_bf16(attn_w_qkv, 0, "cast_win_0"), _cast_bf16(attn_w_o, 0, "cast_wout_0")],
                             [], "ag_start_0")

    small = [c, ln_g, ln_b, conv_w, conv_b]
    pack = jnp.concatenate([a.reshape(-1) for a in small]).reshape(-1, LANES)
    gath = _allgather([pack], "ag_small_params", deps=[first_gather[4]])[0].reshape(N_DEV, -1)
    offs, pos = [], 0
    for a in small:
        offs.append((pos, a.size))
        pos += a.size

    def unshard(idx, lead):
        o, n = offs[idx]
        a = gath[:, o:o + n].reshape((N_DEV,) + lead + (DB,))
        return jnp.moveaxis(a, 0, -2).reshape(lead + (D,))

    c_all = gath[:, :D]
    lng_full, lnb_full = unshard(1, (L, 2)), unshard(2, (L, 2))
    cw_full, cb_full = unshard(3, (LC, 3)), unshard(4, (LC,))

    cond16 = _silu_bf16(jnp.pad(c_all, ((0, 16 - N_DEV), (0, 0))), "silu_cond")
    tn_mod = NC // 3
    bias_mod = lax.dynamic_slice_in_dim(mod_b, me * NC, NC, axis=1).reshape(L, 1, NC)
    mod_part = _matmul(
        "mod_fwd", "nn", (L, NC // tn_mod), 1,
        cond16, pl.BlockSpec((16, D), lambda l, n: (0, 0)),
        mod_w, pl.BlockSpec((None, D, tn_mod), lambda l, n: (l, 0, n)),
        [_sds((L, 16, NC), F32)], [pl.BlockSpec((None, 16, tn_mod), lambda l, n: (l, 0, n))], None,
        extras=[bias_mod], extra_specs=[pl.BlockSpec((None, 1, tn_mod), lambda l, n: (l, 0, n))])[0]
    mod_gath = _allgather([mod_part[:, :N_DEV].reshape(-1, LANES)], "ag_mod")[0].reshape(N_DEV, L, N_DEV, NC)
    mod_me = lax.dynamic_index_in_dim(mod_gath, me, axis=2, keepdims=False)
    mod_me = jnp.moveaxis(mod_me, 0, 1).reshape(L, 6, 1, D)

    def mod_vecs(k):
        l, o = k // 2, 3 * (k % 2)
        return mod_me[l, o], mod_me[l, o + 1], mod_me[l, o + 2]

    def mixer_weights(l):
        if l % 2 == 0:
            return attn_w_qkv, attn_w_o, l // 2
        return conv_w_in, conv_w_out, l // 2

    K = 2 * L

    def group_shards(k):
        l = k // 2
        if k % 2 == 0:
            w_in3, w_out3, j = mixer_weights(l)
            return [_cast_bf16(w_in3, j, f"cast_win_{l}"), _cast_bf16(w_out3, j, f"cast_wout_{l}")]
        return [_cast_bf16(mlp_w1, l, f"cast_w1_{l}"), _cast_bf16(mlp_w2, l, f"cast_w2_{l}")]

    wg = [[None] * 4 for _ in range(L)]
    in_flight = {0: first_gather}

    def fetch_group(k, after):
        shards, lands = _ag_wait(in_flight.pop(k), after, f"ag_wait_{k}")
        first, second = _ag_finish(shards, lands, [], f"ag_finish_{k}")
        if k + 1 < K:
            in_flight[k + 1] = _ag_start(group_shards(k + 1), [first], f"ag_start_{k + 1}")
        rows = D if k % 2 == 0 else F
        wg[k // 2][2 * (k % 2)], wg[k // 2][2 * (k % 2) + 1] = first, second.reshape(rows, D)

    n_ct = 3 * D // CT
    tm_big = _tile(S, 2048)

    def in_proj(l, h, out_dtype, deps):
        return _matmul(
            f"in_proj_{l}", "nn", (S // tm_big, n_ct), 1,
            h, pl.BlockSpec((tm_big, D), lambda i, ct: (i, 0)),
            wg[l][0], pl.BlockSpec((None, D, CT), lambda i, ct: (ct // 3, 0, ct % 3)),
            [_sds((3, S, D), out_dtype)],
            [pl.BlockSpec((None, tm_big, CT), lambda i, ct: (ct // N_DEV, i, ct % N_DEV))], None, deps=deps)[0]

    tm, tn_half = _tile(S, 512), _tile(D, 1024)

    def out_proj(l, a):
        return _matmul(
            f"out_proj_{l}", "nn", (D // tn_half, S // tm), 1,
            a, pl.BlockSpec((tm, D), lambda n, i: (i, 0)),
            wg[l][1], pl.BlockSpec((D, tn_half), lambda n, i: (0, n)),
            [_sds((S, D), F32)], [pl.BlockSpec((tm, tn_half), lambda n, i: (i, n))], None)[0]

    def mlp_up(l, h, deps):
        blk = pl.BlockSpec((tm, FB), lambda j, i: (i, j))
        return _matmul(
            f"mlp_up_{l}", "nn", (N_DEV, S // tm), 1,
            h, pl.BlockSpec((tm, D), lambda j, i: (i, 0)),
            wg[l][2], pl.BlockSpec((None, D, FB), lambda j, i: (j, 0, 0)),
            [_sds((S, F), BF16), _sds((S, F), BF16)], [blk, blk], None, epilogue="relu2",
            extras=[mlp_b1[l].reshape(1, F)], extra_specs=[pl.BlockSpec((1, FB), lambda j, i: (0, j))], deps=deps)

    tn_q = _tile(D, 512)

    def mlp_down(l, act):
        return _matmul(
            f"mlp_down_{l}", "nn", (D // tn_q, S // tm), 1,
            act, pl.BlockSpec((tm, F), lambda n, i: (i, 0)),
            wg[l][3], pl.BlockSpec((F, tn_q), lambda n, i: (0, n)),
            [_sds((S, D), F32)], [pl.BlockSpec((tm, tn_q), lambda n, i: (i, n))], None,
            extras=[mlp_b2[l].reshape(1, D)], extra_specs=[pl.BlockSpec((1, tn_q), lambda n, i: (0, n))])[0]

    xin, ys, hs, saved = [xs], [], [], []
    sh0, sc0, _ = mod_vecs(0)
    h = _modulate("modulate_in", xs, sc0, sh0)
    for k in range(K):
        l = k // 2
        hs.append(h)
        fetch_group(k, h)
        started = [in_flight[k + 1][4]] if k + 1 < K else []
        if k % 2 == 0 and l % 2 == 0:
            qkv3 = in_proj(l, h, BF16, started)
            o, totals = _attn_fwd(f"attn_fwd_{l}", qkv3)
            y = out_proj(l, o)
            saved.append((qkv3, o, totals))
        elif k % 2 == 0:
            bcu3 = in_proj(l, h, F32, started)
            j = l // 2
            p = _conv_fwd(f"conv_fwd_{l}", bcu3, cw_full[j], cb_full[j].reshape(1, D))
            y = out_proj(l, p)
            saved.append((bcu3, p))
        else:
            u, act = mlp_up(l, h, started)
            y = mlp_down(l, act)
            saved.append((u, act))
        ys.append(y)
        if k + 1 < K:
            _, _, gate = mod_vecs(k)
            sh_n, sc_n, _ = mod_vecs(k + 1)
            x_next, h = _ln_fwd(f"ln_fwd_{k}", xin[k], y, gate, lng_full[l, k % 2].reshape(1, D),
                                lnb_full[l, k % 2].reshape(1, D), sc_n, sh_n)
            xin.append(x_next)

    tm_g = _tile(D, 512)

    def grad_rows(name, a, dy, rows):
        return _matmul(
            name, "tn", (rows // tm_g, D // tn_half), 1,
            a, pl.BlockSpec((S, tm_g), lambda m, n: (0, m)),
            dy, pl.BlockSpec((S, tn_half), lambda m, n: (0, n)),
            [_sds((rows, D), BF16)], [pl.BlockSpec((tm_g, tn_half), lambda m, n: (m, n))], None)[0]

    def back_out_proj(l, dy, out_dtype):
        return _matmul(
            f"back_out_proj_{l}", "nt", (D // tn_half, S // tm), 1,
            dy, pl.BlockSpec((tm, D), lambda n, i: (i, 0)),
            wg[l][1], pl.BlockSpec((tn_half, D), lambda n, i: (n, 0)),
            [_sds((S, D), out_dtype)], [pl.BlockSpec((tm, tn_half), lambda n, i: (i, n))], None)[0]

    def back_in_proj(l, d3, deps):
        def part(p):
            return pl.BlockSpec((None, tm, D), lambda i, n: (p, i, 0))
        return _matmul(
            f"back_in_proj_{l}", "nt", (S // tm, D // tn_q), 1,
            d3, part(0),
            _unblock(wg[l][0], f"unblock_win_{l}"), pl.BlockSpec((tn_q, 3 * D), lambda i, n: (n, 0)),
            [_sds((S, D), F32)], [pl.BlockSpec((tm, tn_q), lambda i, n: (i, n))], None,
            extras=[d3, d3], extra_specs=[part(1), part(2)], widen="a", deps=deps)[0]

    def grad_in_proj(l, h, d3):
        def tile(r):
            return pl.BlockSpec((None, S, CT), lambda j, m: ((3 * j + r) // N_DEV, 0, (3 * j + r) % N_DEV))
        return _matmul(
            f"grad_in_proj_{l}", "tn", (N_DEV, D // tm_g), 1,
            h, pl.BlockSpec((S, tm_g), lambda j, m: (0, m)),
            d3, tile(0),
            [_sds((N_DEV, D, 3 * CT), BF16)],
            [pl.BlockSpec((None, tm_g, 3 * CT), lambda j, m: (j, m, 0))], None,
            extras=[d3, d3], extra_specs=[tile(1), tile(2)], widen="b")[0]

    def back_mlp_down(l, dy, u):
        blk = pl.BlockSpec((tm, FB), lambda j, i: (i, j))
        return _matmul(
            f"back_mlp_down_{l}", "nt", (N_DEV, S // tm), 1,
            dy, pl.BlockSpec((tm, D), lambda j, i: (i, 0)),
            wg[l][3], pl.BlockSpec((FB, D), lambda j, i: (j, 0)),
            [_sds((S, F), BF16), _sds((1, F), F32)], [blk, pl.BlockSpec((1, FB), lambda j, i: (0, j))], None,
            epilogue="drelu2", extras=[u], extra_specs=[blk], sem=("parallel", "arbitrary"))

    def back_mlp_up(l, du, deps):
        return _matmul(
            f"back_mlp_up_{l}", "nt", (S // tm, D // tn_q), 1,
            du, pl.BlockSpec((tm, F), lambda i, n: (i, 0)),
            _unblock(wg[l][2], f"unblock_w1_{l}"), pl.BlockSpec((tn_q, F), lambda i, n: (n, 0)),
            [_sds((S, D), F32)], [pl.BlockSpec((tm, tn_q), lambda i, n: (i, n))], None, deps=deps)[0]

    def grad_mlp_up(l, h, du):
        return _matmul(
            f"grad_mlp_up_{l}", "tn", (N_DEV, D // tm_g), 1,
            h, pl.BlockSpec((S, tm_g), lambda j, m: (0, m)),
            du, pl.BlockSpec((S, FB), lambda j, m: (0, j)),
            [_sds((N_DEV, D, FB), BF16)], [pl.BlockSpec((None, tm_g, FB), lambda j, m: (j, m, 0))], None)[0]

    dmod = [[None] * 6 for _ in range(L)]
    dlng = [[None, None] for _ in range(L)]
    dlnb = [[None, None] for _ in range(L)]
    db1, db2 = [None] * L, [None] * L
    dcw, dcb = [None] * LC, [None] * LC

    carries = {"attn_in": None, "attn_out": None, "conv_in": None, "conv_out": None, "w1": None, "w2": None}
    moments = {"attn_in": (attn_w_qkv, m_attn_w_qkv, v_attn_w_qkv), "attn_out": (attn_w_o, m_attn_w_o, v_attn_w_o),
               "conv_in": (conv_w_in, m_conv_w_in, v_conv_w_in), "conv_out": (conv_w_out, m_conv_w_out, v_conv_w_out),
               "w1": (mlp_w1, m_mlp_w1, v_mlp_w1), "w2": (mlp_w2, m_mlp_w2, v_mlp_w2)}
    pending = []

    def finish_pending(after):
        if not pending:
            return
        k, started = pending.pop()
        ps, r2 = _rs_chips_wait(started, after, f"rs_wait_{k}")
        l = k // 2
        if k % 2 == 1:
            keys = [("w1", l), ("w2", l)]
        else:
            kind = "attn" if l % 2 == 0 else "conv"
            keys = [(kind + "_in", l // 2), (kind + "_out", l // 2)]
        for t, (key, idx) in enumerate(keys):
            w3, m3, v3 = moments[key]
            parts = [(ps[t], 0), (r2[t], 1), (r2[t], 2), (r2[t], 3)]
            carries[key] = _adamw(f"adamw_{key}_{l}", w3, m3, v3, idx, parts, carries[key])

    def reduce_group(k, gs, after):
        finish_pending(after)
        r1 = _rs_sibling(gs, f"rs_sibling_{k}")
        ps = [_rs_add(g, r, me_arr, f"rs_add_{k}_{t}") for t, (g, r) in enumerate(zip(gs, r1))]
        started = _rs_chips_start(ps, f"rs_start_{k}")
        pending.append((k, started))
        return started[4]

    def take_ln_rows(k, vec):
        l, o = k // 2, 3 * (k % 2)
        dlng[l][k % 2], dlnb[l][k % 2] = vec[2], vec[3]
        dmod[l][o + 2] = vec[4]
        if k % 2 == 1:
            db2[l] = vec[5]

    def take_mod_rows(k, vec):
        l, o = k // 2, 3 * (k % 2)
        dmod[l][o + 1], dmod[l][o] = vec[0], vec[1]

    lk = K - 1
    _, _, gate = mod_vecs(lk)
    dy, dxr, vec = _boundary(f"boundary_{lk}", "last", (target, xin[lk], ys[lk]),
                             (gate, lng_full[lk // 2, 1].reshape(1, D), lnb_full[lk // 2, 1].reshape(1, D)))
    take_ln_rows(lk, vec)
    loss_local = 0.5 * vec[6, 0] / D
    grad_x = None
    for k in range(K - 1, -1, -1):
        l = k // 2
        if k % 2 == 1:
            u, act = saved[k]
            g_w2 = grad_rows(f"grad_mlp_down_{l}", act, dy, F).reshape(N_DEV, FB, D)
            du, db1_l = back_mlp_down(l, dy, u)
            db1[l] = db1_l[0]
            g_w1 = grad_mlp_up(l, hs[k], du)
            token = reduce_group(k, [g_w1, g_w2], du)
            dh = back_mlp_up(l, du, [token])
        else:
            if l % 2 == 0:
                qkv3, o, totals = saved[k]
                g_out = grad_rows(f"grad_out_proj_{l}", o, dy, D).reshape(N_DEV, DB, D)
                do = back_out_proj(l, dy, BF16)
                d3 = _attn_bwd(f"attn_bwd_{l}", qkv3, do, totals)
            else:
                bcu3, p = saved[k]
                j = l // 2
                g_out = grad_rows(f"grad_out_proj_{l}", p, dy, D).reshape(N_DEV, DB, D)
                dp = back_out_proj(l, dy, F32)
                d3, cvec = _conv_bwd(f"conv_bwd_{l}", bcu3, dp, cw_full[j], cb_full[j].reshape(1, D))
                dcb[j], dcw[j] = cvec[0], cvec[1:4]
            g_in = grad_in_proj(l, hs[k], d3)
            token = reduce_group(k, [g_in, g_out], d3)
            dh = back_in_proj(l, d3, [token])
        sh_k, sc_k, _ = mod_vecs(k)
        if k > 0:
            kp = k - 1
            _, _, gate = mod_vecs(kp)
            dy, dxr, vec = _boundary(
                f"boundary_{kp}", "mid", (dxr, dh, xin[kp], ys[kp]),
                (sc_k, gate, lng_full[kp // 2, kp % 2].reshape(1, D), lnb_full[kp // 2, kp % 2].reshape(1, D)),
                deps=[token])
            take_mod_rows(k, vec)
            take_ln_rows(kp, vec)
        else:
            grad_x, vec = _boundary("boundary_in", "first", (dxr, dh, xs), (sc_k,), deps=[token])
            take_mod_rows(k, vec)
    finish_pending(grad_x)
    results = carries

    flat = [jnp.stack([jnp.stack(r) for r in dmod]).reshape(-1), jnp.stack(db1).reshape(-1),
            jnp.stack(db2).reshape(-1), jnp.stack([jnp.stack(r) for r in dlng]).reshape(-1),
            jnp.stack([jnp.stack(r) for r in dlnb]).reshape(-1), jnp.stack(dcw).reshape(-1),
            jnp.stack(dcb).reshape(-1)]
    sizes = [a.size for a in flat]
    small_g = _allgather([jnp.concatenate(flat).reshape(-1, LANES)], "ag_small_grads")[0]
    total = _sum_devices(small_g, "sum_small_grads").reshape(-1)
    starts = [sum(sizes[:i]) for i in range(len(sizes))]

    def piece(i, shape):
        return total[starts[i]:starts[i] + sizes[i]].reshape(shape)

    def my_block(a):
        return lax.dynamic_slice_in_dim(a, me * DB, DB, axis=a.ndim - 1)

    g_mod_b = piece(0, (L, 6 * D))
    g_b1, g_b2 = piece(1, (L, F)), piece(2, (L, D))
    g_lng, g_lnb = my_block(piece(3, (L, 2, D))), my_block(piece(4, (L, 2, D)))
    g_cw, g_cb = my_block(piece(5, (LC, 3, D))), my_block(piece(6, (LC, D)))

    dmod_all = small_g.reshape(N_DEV, -1)[:, :sizes[0]].reshape(N_DEV, L, 6 * D)
    dmod_mine = jnp.moveaxis(lax.dynamic_slice_in_dim(dmod_all, me * NC, NC, axis=2), 0, 1)
    dmod16 = jnp.pad(dmod_mine, ((0, 0), (0, 16 - N_DEV), (0, 0)))
    tm_mod = _tile(D, 512)
    g_mod_w = _matmul(
        "grad_mod_w", "tn", (L, D // tm_mod), 1,
        cond16, pl.BlockSpec((16, tm_mod), lambda l, m: (0, m)),
        dmod16, pl.BlockSpec((None, 16, NC), lambda l, m: (l, 0, 0)),
        [_sds((L, D, NC), F32)], [pl.BlockSpec((None, tm_mod, NC), lambda l, m: (l, m, 0))], None)[0]
    res_mod_w = None
    for l in range(L):
        res_mod_w = _adamw(f"adamw_mod_w_{l}", mod_w, m_mod_w, v_mod_w, l, [(g_mod_w, l)], res_mod_w)

    small_w = [mod_b, ln_g, ln_b, conv_w, conv_b, mlp_b1, mlp_b2]
    small_m = [m_mod_b, m_ln_g, m_ln_b, m_conv_w, m_conv_b, m_mlp_b1, m_mlp_b2]
    small_v = [v_mod_b, v_ln_g, v_ln_b, v_conv_w, v_conv_b, v_mlp_b1, v_mlp_b2]
    small_gr = [g_mod_b, g_lng, g_lnb, g_cw, g_cb, g_b1, g_b2]
    n_small = sum(a.size for a in small_w)
    pad = (-n_small) % (8 * LANES)

    def packed(arrs):
        v = jnp.concatenate([a.reshape(-1) for a in arrs])
        return jnp.pad(v, (0, pad)).reshape(1, -1, LANES)

    sg, sd, sm, sv = _adamw("adamw_small", packed(small_w), packed(small_m), packed(small_v), 0,
                            [(packed(small_gr), 0)])

    def unpacked(buf):
        out, pos = [], 0
        flat_buf = buf.reshape(-1)
        for a in small_w:
            out.append(flat_buf[pos:pos + a.size].reshape(a.shape))
            pos += a.size
        return out

    small_res = [unpacked(b) for b in (sg, sd, sm, sv)]
    loss = lax.psum(loss_local, ("x", "y", "c"))

    def leaf(kind):
        s = small_res[kind]
        return [res_mod_w[kind], s[0], s[1], s[2], results["attn_in"][kind], results["attn_out"][kind],
                results["conv_in"][kind], s[3], s[4], results["conv_out"][kind], results["w1"][kind], s[5],
                results["w2"][kind], s[6]]

    return (loss, grad_x[None], *leaf(0), *leaf(1), *leaf(2), *leaf(3))
```

```python
import functools

import jax
import jax.numpy as jnp
from jax import lax
from jax.experimental import pallas as pl
from jax.experimental.pallas import tpu as pltpu

F32 = jnp.float32
BF16 = jnp.bfloat16
N_DEV = 8
HEAD_DIM = 128
LANES = 128
DEEPNORM_ALPHA = 8.0 ** 0.25
LN_EPS = 1e-5
ADAM_LR = 0.001
ADAM_B1 = 0.9
ADAM_B2 = 0.999
ADAM_EPS = 1e-08
ADAM_WD = 0.01
ADAM_STEP = 10
VMEM_LIMIT_BYTES = 52 * 1024 * 1024
MESH = pl.DeviceIdType.MESH
HBM_SPEC = pl.BlockSpec(memory_space=pltpu.HBM)
ANY_SPEC = pl.BlockSpec(memory_space=pl.ANY)
SEM_SPEC = pl.BlockSpec(memory_space=pltpu.SEMAPHORE)
EFFECT = pltpu.SideEffectType.DATAFLOW_SIDE_EFFECTING


def _pcall(body, **kw):
    return pl.pallas_call(body, **kw)


def _cp(sem=None):
    return pltpu.CompilerParams(dimension_semantics=sem, vmem_limit_bytes=VMEM_LIMIT_BYTES)


def _tile(n, pref):
    t = min(n, pref)
    assert n % t == 0, (n, pref)
    return t


def _sds(shape, dtype):
    return jax.ShapeDtypeStruct(tuple(shape), dtype)


def _allgather(xs, name, deps=()):
    n, nd = len(xs), len(deps)

    def body(*refs):
        x_refs = refs[:n]
        out_refs = refs[n + nd:2 * n + nd]
        send_sems, recv_sems, local_sems = refs[2 * n + nd:]
        x, y, c = lax.axis_index("x"), lax.axis_index("y"), lax.axis_index("c")
        me, sibling = (x, y, c), (x, y, 1 - c)
        chips = [(1 - x, y), (x, 1 - y), (1 - x, 1 - y)]

        def slot(t, px, py, pc):
            return out_refs[t].at[4 * px + 2 * py + pc]

        def copy(t, k, block, to, src=None):
            return pltpu.make_async_remote_copy(
                src_ref=slot(t, *block) if src is None else src, dst_ref=slot(t, *block),
                send_sem=send_sems.at[t, k], recv_sem=recv_sems.at[t, k],
                device_id=to, device_id_type=MESH)

        mine = [pltpu.make_async_copy(x_refs[t], slot(t, *me), local_sems.at[t]) for t in range(n)]
        for cp in mine:
            cp.start()
        first = []
        for t in range(n):
            first.append(copy(t, 0, me, sibling, src=x_refs[t]))
            for j, chip in enumerate(chips):
                first.append(copy(t, 1 + j, me, (*chip, c), src=x_refs[t]))
        for cp in first:
            cp.start()
        passed = []
        for j, chip in enumerate(chips):
            for t in range(n):
                copy(t, 1 + j, (*chip, c), me).wait_recv()
                fwd = copy(t, 4 + j, (*chip, c), sibling)
                fwd.start()
                passed.append(fwd)
        for t in range(n):
            copy(t, 0, sibling, me).wait_recv()
            for j, chip in enumerate(chips):
                copy(t, 4 + j, (*chip, 1 - c), me).wait_recv()
        for cp in first + passed:
            cp.wait_send()
        for cp in mine:
            cp.wait()

    outs = _pcall(
        body, name=name,
        out_shape=[_sds((N_DEV,) + a.shape, a.dtype) for a in xs],
        in_specs=[HBM_SPEC] * n + [ANY_SPEC] * nd, out_specs=[HBM_SPEC] * n,
        scratch_shapes=[pltpu.SemaphoreType.DMA((n, 7)), pltpu.SemaphoreType.DMA((n, 7)),
                        pltpu.SemaphoreType.DMA((n,))],
    )(*xs, *deps)
    return list(outs)


def _peers(x, y, c):
    return [(x, y, 1 - c), (1 - x, y, c), (x, 1 - y, c), (1 - x, 1 - y, c)]


def _block_owner(x, y, c, k):
    return ((1 - x) if k >> 1 else x, (1 - y) if k & 1 else y, c)


def _hbm(a):
    return pltpu.with_memory_space_constraint(a, pltpu.HBM)


def _ag_start(lands, deps, name):
    n, nd = len(lands), len(deps)

    def body(*refs):
        land_refs = refs[:n]
        send_sems, recv_sems = refs[n + nd], refs[n + nd + 1]
        token = refs[-1]
        x, y, c = lax.axis_index("x"), lax.axis_index("y"), lax.axis_index("c")
        for t in range(n):
            mine = land_refs[t].at[4 * x + 2 * y + c]
            for k, to in enumerate(_peers(x, y, c)):
                pltpu.make_async_remote_copy(
                    src_ref=mine, dst_ref=mine,
                    send_sem=send_sems.at[4 * t + k], recv_sem=recv_sems.at[4 * t + k],
                    device_id=to, device_id_type=MESH).start()
        token[...] = jnp.zeros_like(token)

    outs = _pcall(
        body, name=name,
        out_shape=(pltpu.SemaphoreType.DMA((4 * n,)), pltpu.SemaphoreType.DMA((4 * n,)),
                   *[pltpu.HBM(a.shape, a.dtype) for a in lands], _sds((8, LANES), F32)),
        in_specs=[HBM_SPEC] * n + [ANY_SPEC] * nd,
        out_specs=(SEM_SPEC, SEM_SPEC, *[HBM_SPEC] * n, pl.BlockSpec(memory_space=pltpu.VMEM)),
        input_output_aliases={i: 2 + i for i in range(n)},
        compiler_params=pltpu.CompilerParams(has_side_effects=EFFECT),
    )(*[_hbm(a) for a in lands], *deps)
    return outs[0], outs[1], list(outs[2:2 + n]), outs[-1]


def _ag_wait(started, after, name):
    send_sems, recv_sems, lands, _ = started
    n = len(lands)

    def body(*refs):
        land_refs = refs[:n]
        send_sems, recv_sems = refs[n], refs[n + 1]
        x, y, c = lax.axis_index("x"), lax.axis_index("y"), lax.axis_index("c")
        for t in range(n):
            for k, (px, py, pc) in enumerate(_peers(x, y, c)):
                cp = pltpu.make_async_remote_copy(
                    src_ref=land_refs[t].at[4 * x + 2 * y + c], dst_ref=land_refs[t].at[4 * px + 2 * py + pc],
                    send_sem=send_sems.at[4 * t + k], recv_sem=recv_sems.at[4 * t + k],
                    device_id=(px, py, pc), device_id_type=MESH)
                cp.wait_send()
                cp.wait_recv()

    outs = _pcall(
        body, name=name,
        out_shape=tuple(pltpu.HBM(a.shape, a.dtype) for a in lands),
        in_specs=[HBM_SPEC] * n + [SEM_SPEC, SEM_SPEC, ANY_SPEC],
        out_specs=[HBM_SPEC] * n,
        input_output_aliases={i: i for i in range(n)},
        compiler_params=pltpu.CompilerParams(has_side_effects=EFFECT),
    )(*lands, send_sems, recv_sems, after)
    return list(outs)


def _ag_finish(lands, name):
    n = len(lands)

    def body(*refs):
        out_refs = refs[n:2 * n]
        send_sems, recv_sems = refs[2 * n:]
        x, y, c = lax.axis_index("x"), lax.axis_index("y"), lax.axis_index("c")
        chips = [(1 - x, y), (x, 1 - y), (1 - x, 1 - y)]

        def passing(t, j, pc):
            px, py = chips[j]
            rows = out_refs[t].at[4 * px + 2 * py + pc]
            return pltpu.make_async_remote_copy(
                src_ref=rows, dst_ref=rows, send_sem=send_sems.at[t, j], recv_sem=recv_sems.at[t, j],
                device_id=(x, y, 1 - c), device_id_type=MESH)

        sends = [passing(t, j, c) for t in range(n) for j in range(3)]
        for cp in sends:
            cp.start()
        for t in range(n):
            for j in range(3):
                passing(t, j, 1 - c).wait_recv()
        for cp in sends:
            cp.wait_send()

    outs = _pcall(
        body, name=name,
        out_shape=[_sds(a.shape, a.dtype) for a in lands],
        in_specs=[HBM_SPEC] * n, out_specs=[HBM_SPEC] * n,
        input_output_aliases={i: i for i in range(n)},
        scratch_shapes=[pltpu.SemaphoreType.DMA((n, 3)), pltpu.SemaphoreType.DMA((n, 3))],
    )(*lands)
    return list(outs)


def _rs_chips_start(ps, name):
    n = len(ps)

    def body(*refs):
        p_refs, land_refs = refs[:n], refs[n:2 * n]
        send_sems, recv_sems = refs[2 * n], refs[2 * n + 1]
        token = refs[-1]
        x, y, c = lax.axis_index("x"), lax.axis_index("y"), lax.axis_index("c")
        for t in range(n):
            for k in range(1, 4):
                pltpu.make_async_remote_copy(
                    src_ref=p_refs[t].at[k], dst_ref=land_refs[t].at[k],
                    send_sem=send_sems.at[3 * t + k - 1], recv_sem=recv_sems.at[3 * t + k - 1],
                    device_id=_block_owner(x, y, c, k), device_id_type=MESH).start()
        token[...] = jnp.zeros_like(token)

    lands = [lax.empty(p.shape, p.dtype) for p in ps]
    outs = _pcall(
        body, name=name,
        out_shape=(pltpu.SemaphoreType.DMA((3 * n,)), pltpu.SemaphoreType.DMA((3 * n,)),
                   *[pltpu.HBM(a.shape, a.dtype) for a in ps], *[pltpu.HBM(a.shape, a.dtype) for a in lands],
                   _sds((8, LANES), F32)),
        in_specs=[HBM_SPEC] * (2 * n),
        out_specs=(SEM_SPEC, SEM_SPEC, *[HBM_SPEC] * (2 * n), pl.BlockSpec(memory_space=pltpu.VMEM)),
        input_output_aliases={i: 2 + i for i in range(2 * n)},
        compiler_params=pltpu.CompilerParams(has_side_effects=EFFECT),
    )(*[_hbm(a) for a in ps], *[_hbm(a) for a in lands])
    return outs[0], outs[1], list(outs[2:2 + n]), list(outs[2 + n:2 + 2 * n]), outs[-1]


def _rs_chips_wait(started, after, name):
    send_sems, recv_sems, ps, lands, _ = started
    n = len(ps)

    def body(*refs):
        p_refs, land_refs = refs[:n], refs[n:2 * n]
        send_sems, recv_sems = refs[2 * n], refs[2 * n + 1]
        x, y, c = lax.axis_index("x"), lax.axis_index("y"), lax.axis_index("c")
        for t in range(n):
            for k in range(1, 4):
                cp = pltpu.make_async_remote_copy(
                    src_ref=p_refs[t].at[k], dst_ref=land_refs[t].at[k],
                    send_sem=send_sems.at[3 * t + k - 1], recv_sem=recv_sems.at[3 * t + k - 1],
                    device_id=_block_owner(x, y, c, k), device_id_type=MESH)
                cp.wait_send()
                cp.wait_recv()

    outs = _pcall(
        body, name=name,
        out_shape=(*[pltpu.HBM(a.shape, a.dtype) for a in ps], *[pltpu.HBM(a.shape, a.dtype) for a in lands]),
        in_specs=[HBM_SPEC] * (2 * n) + [SEM_SPEC, SEM_SPEC, ANY_SPEC],
        out_specs=[HBM_SPEC] * (2 * n),
        input_output_aliases={i: i for i in range(2 * n)},
        compiler_params=pltpu.CompilerParams(has_side_effects=EFFECT),
    )(*ps, *lands, send_sems, recv_sems, after)
    return list(outs[:n]), list(outs[n:])


def _rs_sibling(gs, name):
    n = len(gs)

    def body(*refs):
        g_refs = refs[:n]
        r_refs = refs[n:2 * n]
        send_sems, recv_sems = refs[2 * n:]
        x, y, c = lax.axis_index("x"), lax.axis_index("y"), lax.axis_index("c")
        me = 4 * x + 2 * y + c
        cps = []
        for t in range(n):
            for k in range(4):
                cps.append(pltpu.make_async_remote_copy(
                    src_ref=g_refs[t].at[jnp.bitwise_xor(2 * k + 1, me)], dst_ref=r_refs[t].at[k],
                    send_sem=send_sems.at[t, k], recv_sem=recv_sems.at[t, k],
                    device_id=(x, y, 1 - c), device_id_type=MESH))
        for cp in cps:
            cp.start()
        for cp in cps:
            cp.wait()

    outs = _pcall(
        body, name=name,
        out_shape=[_sds((4,) + g.shape[1:], g.dtype) for g in gs],
        in_specs=[HBM_SPEC] * n, out_specs=[HBM_SPEC] * n,
        scratch_shapes=[pltpu.SemaphoreType.DMA((n, 4)), pltpu.SemaphoreType.DMA((n, 4))],
    )(*gs)
    return list(outs)


def _rs_add(g, r1, me_arr, name):
    _, R, C = g.shape
    tr = _tile(R, 256)

    def body(me_ref, g_ref, r_ref, o_ref):
        o_ref[...] = (g_ref[...].astype(F32) + r_ref[...].astype(F32)).astype(BF16)

    grid_spec = pltpu.PrefetchScalarGridSpec(
        num_scalar_prefetch=1, grid=(4, R // tr),
        in_specs=[pl.BlockSpec((None, tr, C), lambda k, r, me: (jnp.bitwise_xor(2 * k, me[0]), r, 0)),
                  pl.BlockSpec((None, tr, C), lambda k, r, me: (k, r, 0))],
        out_specs=pl.BlockSpec((None, tr, C), lambda k, r, me: (k, r, 0)))
    return _pcall(body, name=name, grid_spec=grid_spec, out_shape=_sds((4, R, C), BF16),
                  compiler_params=_cp(("parallel", "parallel")))(me_arr, g, r1)


_DIMS = {"nn": (((1,), (0,)), ((), ())), "nt": (((1,), (1,)), ((), ())), "tn": (((0,), (0,)), ((), ()))}


def _dot(a, b, mode="nn"):
    return lax.dot_general(a, b, _DIMS[mode], preferred_element_type=F32)


def _matmul(name, mode, grid, nk, a, a_spec, b, b_spec, out_shapes, out_specs, acc_shape,
            epilogue="plain", extras=(), extra_specs=(), sem=None, widen=None, deps=()):
    n_extra, n_out = len(extras) + len(deps), len(out_shapes)
    extras, extra_specs = list(extras) + list(deps), list(extra_specs) + [ANY_SPEC] * len(deps)

    def body(*refs):
        a_ref, b_ref = refs[0], refs[1]
        ex = refs[2:2 + n_extra - len(deps)]
        outs = refs[2 + n_extra:2 + n_extra + n_out]
        a_tile, b_tile = a_ref[...].astype(BF16), b_ref[...].astype(BF16)
        if widen == "a":
            a_tile = jnp.concatenate([a_tile] + [e[...].astype(BF16) for e in ex], axis=1)
        elif widen == "b":
            b_tile = jnp.concatenate([b_tile] + [e[...].astype(BF16) for e in ex], axis=1)
        part = _dot(a_tile, b_tile, mode)

        def finish(acc):
            if epilogue == "plain":
                if len(ex) and widen is None:
                    acc = acc + ex[0][...]
                outs[0][...] = acc.astype(outs[0].dtype)
            elif epilogue == "relu2":
                u = acc + ex[0][...]
                outs[0][...] = u.astype(outs[0].dtype)
                outs[1][...] = jnp.square(jnp.maximum(u, 0.0)).astype(outs[1].dtype)
            else:
                du = acc * (2.0 * jnp.maximum(ex[0][...].astype(F32), 0.0))
                outs[0][...] = du.astype(outs[0].dtype)
                col = jnp.sum(du, axis=0, keepdims=True)
                inner = pl.program_id(1)

                @pl.when(inner == 0)
                def _():
                    outs[1][...] = col

                @pl.when(inner > 0)
                def _():
                    outs[1][...] += col

        if nk == 1:
            finish(part)
        else:
            acc_ref = refs[2 + n_extra + n_out]
            kidx = pl.program_id(len(grid) - 1)

            @pl.when(kidx == 0)
            def _():
                acc_ref[...] = part

            @pl.when(kidx > 0)
            def _():
                acc_ref[...] += part

            @pl.when(kidx == nk - 1)
            def _():
                finish(acc_ref[...])

    if sem is None:
        sem = ("parallel",) * (len(grid) - (1 if nk > 1 else 0)) + (("arbitrary",) if nk > 1 else ())
    outs = _pcall(
        body, name=name, grid=grid,
        in_specs=[a_spec, b_spec] + list(extra_specs), out_specs=list(out_specs),
        out_shape=list(out_shapes),
        scratch_shapes=[pltpu.VMEM(acc_shape, F32)] if nk > 1 else [],
        compiler_params=_cp(sem),
    )(a, b, *extras)
    return list(outs)


def _cast_bf16(w3, l, me_arr, name):
    _, R, C = w3.shape
    tr = _tile(R, 256)

    def body(me_ref, x_ref, o_ref):
        o_ref[...] = x_ref[...].astype(BF16)

    grid_spec = pltpu.PrefetchScalarGridSpec(
        num_scalar_prefetch=1, grid=(R // tr,),
        in_specs=[pl.BlockSpec((None, tr, C), lambda r, me: (l, r, 0))],
        out_specs=pl.BlockSpec((None, tr, C), lambda r, me: (me[0], r, 0)))
    return _pcall(body, name=name, grid_spec=grid_spec, out_shape=_sds((N_DEV, R, C), BF16),
                  compiler_params=_cp(("parallel",)))(me_arr, w3)


def _unblock(wb, name):
    n, R, C = wb.shape

    def body(x_ref, o_ref):
        o_ref[...] = x_ref[...]

    return _pcall(body, name=name, grid=(n,),
                  in_specs=[pl.BlockSpec((None, R, C), lambda j: (j, 0, 0))],
                  out_specs=pl.BlockSpec((R, C), lambda j: (0, j)),
                  out_shape=_sds((R, n * C), wb.dtype), compiler_params=_cp(("parallel",)))(wb)


def _silu_bf16(c16, name):
    def body(c_ref, o_ref):
        v = c_ref[...]
        o_ref[...] = (v / (1.0 + jnp.exp(-v))).astype(BF16)

    return _pcall(body, name=name, out_shape=_sds(c16.shape, BF16))(c16)


def _sum_devices(g, name):
    def body(g_ref, o_ref):
        acc = g_ref[0]
        for d in range(1, N_DEV):
            acc = acc + g_ref[d]
        o_ref[...] = acc

    return _pcall(body, name=name, out_shape=_sds(g.shape[1:], F32))(g)


def _modulate(name, x, sc, sh):
    S, D = x.shape
    ts = _tile(S, 256)

    def body(x_ref, sc_ref, sh_ref, h_ref):
        h_ref[...] = (x_ref[...] * (1.0 + sc_ref[...]) + sh_ref[...]).astype(BF16)

    row = pl.BlockSpec((ts, D), lambda i: (i, 0))
    vec = pl.BlockSpec((1, D), lambda i: (0, 0))
    return _pcall(body, name=name, grid=(S // ts,), in_specs=[row, vec, vec], out_specs=row,
                  out_shape=_sds((S, D), BF16), compiler_params=_cp(("parallel",)))(x, sc, sh)


def _ln_stats(r):
    mu = jnp.mean(r, axis=-1, keepdims=True)
    xc = r - mu
    var = jnp.mean(xc * xc, axis=-1, keepdims=True)
    rstd = lax.rsqrt(var + LN_EPS)
    return xc * rstd, rstd


def _ln_fwd(name, xin, y, gate, lng, lnb, sc, sh):
    S, D = xin.shape
    ts = _tile(S, 256)

    def body(x_ref, y_ref, g_ref, lg_ref, lb_ref, sc_ref, sh_ref, xo_ref, h_ref):
        r = DEEPNORM_ALPHA * x_ref[...] + (1.0 + g_ref[...]) * y_ref[...]
        xhat, _ = _ln_stats(r)
        xo = xhat * lg_ref[...] + lb_ref[...]
        xo_ref[...] = xo
        h_ref[...] = (xo * (1.0 + sc_ref[...]) + sh_ref[...]).astype(BF16)

    row = pl.BlockSpec((ts, D), lambda i: (i, 0))
    vec = pl.BlockSpec((1, D), lambda i: (0, 0))
    return _pcall(body, name=name, grid=(S // ts,), in_specs=[row, row] + [vec] * 5,
                  out_specs=[row, row], out_shape=[_sds((S, D), F32), _sds((S, D), BF16)],
                  compiler_params=_cp(("parallel",)))(xin, y, gate, lng, lnb, sc, sh)


def _boundary(name, mode, big, vecs, deps=()):
    S, D = big[0].shape
    ts = _tile(S, 128)
    nb, nv, nd = len(big), len(vecs), len(deps)

    def body(*refs):
        b = [r[...] for r in refs[:nb]]
        v = [r[...] for r in refs[nb:nb + nv]]
        outs = refs[nb + nv + nd:]
        vec_ref = outs[-1]
        i = pl.program_id(0)
        rows = [None] * 8

        def colsum(a):
            return jnp.sum(a, axis=0, keepdims=True)

        if mode == "first":
            dxr, dh, xk = b
            sc = v[0]
            outs[0][...] = dxr + dh * (1.0 + sc)
            rows[0], rows[1] = colsum(dh * xk), colsum(dh)
        else:
            if mode == "mid":
                dxr, dh, xprev, yk = b
                sc, gate, lng, lnb = v
            else:
                target, xprev, yk = b
                gate, lng, lnb = v
            r = DEEPNORM_ALPHA * xprev + (1.0 + gate) * yk
            xhat, rstd = _ln_stats(r)
            if mode == "mid":
                xk = xhat * lng + lnb
                d = dxr + dh * (1.0 + sc)
                rows[0], rows[1] = colsum(dh * xk), colsum(dh)
            else:
                err = xhat * lng + lnb - target
                d = err * (1.0 / D)
                rows[6] = jnp.zeros((1, D), F32) + jnp.sum(err * err)
            rows[2], rows[3] = colsum(d * xhat), colsum(d)
            dxhat = d * lng
            m1 = jnp.mean(dxhat, axis=-1, keepdims=True)
            m2 = jnp.mean(dxhat * xhat, axis=-1, keepdims=True)
            dr = rstd * (dxhat - m1 - xhat * m2)
            dy = (1.0 + gate) * dr
            rows[4], rows[5] = colsum(dr * yk), colsum(dy)
            outs[0][...] = dy.astype(BF16)
            outs[1][...] = DEEPNORM_ALPHA * dr
        @pl.when(i == 0)
        def _():
            vec_ref[...] = jnp.zeros_like(vec_ref)

        for j, row_j in enumerate(rows):
            if row_j is not None:
                vec_ref[j:j + 1, :] += row_j

    row = pl.BlockSpec((ts, D), lambda i: (i, 0))
    vec = pl.BlockSpec((1, D), lambda i: (0, 0))
    vec_out = pl.BlockSpec((8, D), lambda i: (0, 0))
    if mode == "first":
        out_specs, out_shape = [row, vec_out], [_sds((S, D), F32), _sds((8, D), F32)]
    else:
        out_specs = [row, row, vec_out]
        out_shape = [_sds((S, D), BF16), _sds((S, D), F32), _sds((8, D), F32)]
    return _pcall(body, name=name, grid=(S // ts,), in_specs=[row] * nb + [vec] * nv + [ANY_SPEC] * nd,
                  out_specs=out_specs, out_shape=out_shape,
                  compiler_params=_cp(("arbitrary",)))(*big, *vecs, *deps)


def _split_bf16(a):
    hi = a.astype(BF16)
    lo = (a - hi.astype(F32)).astype(BF16)
    return hi, lo


def _cumsum_matrix(tri):
    half = jnp.concatenate([tri.astype(BF16), jnp.ones((LANES, LANES), BF16)], axis=1)
    return jnp.concatenate([half, half], axis=0)


def _chunk_sums(a, mat):
    hi, lo = _split_bf16(a)
    both = _dot(jnp.concatenate([hi, lo], axis=1), mat)
    return both[:, :LANES], both[:, LANES:]


LOG2_E = 1.4426950408889634


def _log2_1m_beta(z2):
    return -(jnp.maximum(z2, 0.0) + jnp.log2(1.0 + jnp.exp2(-jnp.abs(z2))))


def _attn_fwd(name, qkv3):
    _, S, D = qkv3.shape
    H = D // HEAD_DIM
    tq = _tile(S, 512)
    nch = tq // LANES
    scale = HEAD_DIM ** -0.5

    def body(q_ref, k_ref, v_ref, o_ref, t_ref):
        i = pl.program_id(1)
        q = q_ref[...]
        jj = lax.broadcasted_iota(jnp.int32, (LANES, LANES), 0)
        ss = lax.broadcasted_iota(jnp.int32, (LANES, LANES), 1)
        from_here = _cumsum_matrix(jj >= ss)
        causal = (lax.broadcasted_iota(jnp.int32, (tq, tq), 1) < lax.broadcasted_iota(jnp.int32, (tq, tq), 0))

        def group(k0, masked, tail, acc):
            kb = k_ref[pl.ds(k0, tq), :]
            vb = v_ref[pl.ds(k0, tq), :]
            z = _dot(q, kb, "nt") * (scale * LOG2_E)
            lm = _log2_1m_beta(z)
            if masked:
                lm = jnp.where(causal, lm, 0.0)
            args = [None] * nch
            for ch in reversed(range(nch)):
                sl = slice(ch * LANES, (ch + 1) * LANES)
                cum, tot = _chunk_sums(lm[:, sl], from_here)
                args[ch] = z[:, sl] + (cum + tail)
                tail = tail + tot
            a = jnp.exp2(jnp.concatenate(args, axis=1))
            if masked:
                a = jnp.where(causal, a, 0.0)
            return tail, acc + _dot(a.astype(BF16), vb)

        zero = jnp.zeros((tq, HEAD_DIM), F32)
        carry = group(pl.multiple_of(i * tq, tq), True, zero, zero)
        tail, acc = lax.fori_loop(
            0, i, lambda it, cr: group(pl.multiple_of((i - 1 - it) * tq, tq), False, *cr), carry)
        o_ref[...] = acc.astype(BF16)
        t_ref[...] = tail

    return _pcall(
        body, name=name, grid=(H, S // tq),
        in_specs=[pl.BlockSpec((None, tq, HEAD_DIM), lambda h, i: (0, i, h)),
                  pl.BlockSpec((None, S, HEAD_DIM), lambda h, i: (1, 0, h)),
                  pl.BlockSpec((None, S, HEAD_DIM), lambda h, i: (2, 0, h))],
        out_specs=[pl.BlockSpec((tq, HEAD_DIM), lambda h, i: (i, h)),
                   pl.BlockSpec((tq, HEAD_DIM), lambda h, i: (i, h))],
        out_shape=[_sds((S, D), BF16), _sds((S, D), F32)],
        compiler_params=_cp(("parallel", "parallel")),
    )(qkv3, qkv3, qkv3)


def _attn_bwd(name, qkv3, do, totals):
    _, S, D = qkv3.shape
    H = D // HEAD_DIM
    tq = _tile(S, 512)
    nch = tq // LANES
    nq = S // tq
    scale = HEAD_DIM ** -0.5

    def body(q_ref, k_ref, v_ref, do_ref, t_ref, out_ref, dk_acc, dv_acc):
        i = pl.program_id(1)

        @pl.when(i == 0)
        def _():
            dk_acc[...] = jnp.zeros_like(dk_acc)
            dv_acc[...] = jnp.zeros_like(dv_acc)

        q = q_ref[...]
        do_b = do_ref[...]
        total = t_ref[...]
        jj = lax.broadcasted_iota(jnp.int32, (LANES, LANES), 0)
        ss = lax.broadcasted_iota(jnp.int32, (LANES, LANES), 1)
        before = _cumsum_matrix(jj < ss)
        causal = (lax.broadcasted_iota(jnp.int32, (tq, tq), 1) < lax.broadcasted_iota(jnp.int32, (tq, tq), 0))

        def group(k0, masked, head, dsum, dq):
            kb = k_ref[pl.ds(k0, tq), :]
            vb = v_ref[pl.ds(k0, tq), :]
            z = _dot(q, kb, "nt") * (scale * LOG2_E)
            lm = _log2_1m_beta(z)
            sig = jnp.exp2(z + lm)
            if masked:
                lm = jnp.where(causal, lm, 0.0)
            args = []
            for ch in range(nch):
                sl = slice(ch * LANES, (ch + 1) * LANES)
                cum, tot = _chunk_sums(lm[:, sl], before)
                args.append(z[:, sl] + (total - head - cum))
                head = head + tot
            a = jnp.exp2(jnp.concatenate(args, axis=1))
            if masked:
                a = jnp.where(causal, a, 0.0)
            de = _dot(do_b, vb, "nt") * a
            dv_acc[pl.ds(k0, tq), :] += _dot(a.astype(BF16), do_b, "tn")
            sums = []
            for ch in range(nch):
                sl = slice(ch * LANES, (ch + 1) * LANES)
                cum, tot = _chunk_sums(de[:, sl], before)
                sums.append(dsum + cum)
                dsum = dsum + tot
            dz = de - sig * (de + jnp.concatenate(sums, axis=1))
            if masked:
                dz = jnp.where(causal, dz, 0.0)
            dzb = dz.astype(BF16)
            dk_acc[pl.ds(k0, tq), :] += _dot(dzb, q, "tn")
            return head, dsum, dq + _dot(dzb, kb)

        zero = jnp.zeros((tq, HEAD_DIM), F32)
        carry = lax.fori_loop(
            0, i, lambda g, cr: group(pl.multiple_of(g * tq, tq), False, *cr), (zero, zero, zero))
        q0 = pl.multiple_of(i * tq, tq)
        _, _, dq = group(q0, True, *carry)
        out_ref[0, pl.ds(q0, tq), :] = (dq * scale).astype(BF16)

        @pl.when(i == nq - 1)
        def _():
            out_ref[1] = (dk_acc[...] * scale).astype(BF16)
            out_ref[2] = dv_acc[...].astype(BF16)

    return _pcall(
        body, name=name, grid=(H, nq),
        in_specs=[pl.BlockSpec((None, tq, HEAD_DIM), lambda h, i: (0, i, h)),
                  pl.BlockSpec((None, S, HEAD_DIM), lambda h, i: (1, 0, h)),
                  pl.BlockSpec((None, S, HEAD_DIM), lambda h, i: (2, 0, h)),
                  pl.BlockSpec((tq, HEAD_DIM), lambda h, i: (i, h)),
                  pl.BlockSpec((tq, HEAD_DIM), lambda h, i: (i, h))],
        out_specs=pl.BlockSpec((3, S, HEAD_DIM), lambda h, i: (0, 0, h)),
        out_shape=_sds((3, S, D), BF16),
        scratch_shapes=[pltpu.VMEM((S, HEAD_DIM), F32), pltpu.VMEM((S, HEAD_DIM), F32)],
        compiler_params=_cp(("parallel", "arbitrary")),
    )(qkv3, qkv3, qkv3, do, totals)


def _conv_taps(g, t):
    S = g.shape[0]
    g1 = jnp.where(t >= 1, pltpu.roll(g, 1, 0), 0.0)
    g2 = jnp.where(t >= 2, pltpu.roll(g, 2, 0), 0.0)
    return g1, g2


def _conv_fwd(name, bcu3, cw, cb):
    _, S, D = bcu3.shape
    tf = LANES

    def body(x_ref, w_ref, b_ref, p_ref):
        t = lax.broadcasted_iota(jnp.int32, (S, tf), 0)
        g = x_ref[1] * x_ref[2]
        g1, g2 = _conv_taps(g, t)
        y = w_ref[0:1, :] * g2 + w_ref[1:2, :] * g1 + w_ref[2:3, :] * g + b_ref[...]
        p_ref[...] = (x_ref[0] * y).astype(BF16)

    return _pcall(
        body, name=name, grid=(D // tf,),
        in_specs=[pl.BlockSpec((3, S, tf), lambda f: (0, 0, f)), pl.BlockSpec((3, tf), lambda f: (0, f)),
                  pl.BlockSpec((1, tf), lambda f: (0, f))],
        out_specs=pl.BlockSpec((S, tf), lambda f: (0, f)), out_shape=_sds((S, D), BF16),
        compiler_params=_cp(("parallel",)),
    )(bcu3, cw, cb)


def _conv_bwd(name, bcu3, dp, cw, cb):
    _, S, D = bcu3.shape
    tf = LANES

    def body(x_ref, dp_ref, w_ref, b_ref, o_ref, vec_ref):
        t = lax.broadcasted_iota(jnp.int32, (S, tf), 0)
        bg, cg, u = x_ref[0], x_ref[1], x_ref[2]
        g = cg * u
        g1, g2 = _conv_taps(g, t)
        w0, w1, w2 = w_ref[0:1, :], w_ref[1:2, :], w_ref[2:3, :]
        y = w0 * g2 + w1 * g1 + w2 * g + b_ref[...]
        dpv = dp_ref[...]
        o_ref[0] = (dpv * y).astype(BF16)
        dy = dpv * bg

        def colsum(a):
            return jnp.sum(a, axis=0, keepdims=True)

        vec_ref[...] = jnp.zeros_like(vec_ref)
        for j, row_j in enumerate((dy, dy * g2, dy * g1, dy * g)):
            vec_ref[j:j + 1, :] = colsum(row_j)
        dy1 = jnp.where(t < S - 1, pltpu.roll(dy, S - 1, 0), 0.0)
        dy2 = jnp.where(t < S - 2, pltpu.roll(dy, S - 2, 0), 0.0)
        dg = w2 * dy + w1 * dy1 + w0 * dy2
        o_ref[1] = (dg * u).astype(BF16)
        o_ref[2] = (dg * cg).astype(BF16)

    return _pcall(
        body, name=name, grid=(D // tf,),
        in_specs=[pl.BlockSpec((3, S, tf), lambda f: (0, 0, f)), pl.BlockSpec((S, tf), lambda f: (0, f)),
                  pl.BlockSpec((3, tf), lambda f: (0, f)), pl.BlockSpec((1, tf), lambda f: (0, f))],
        out_specs=[pl.BlockSpec((3, S, tf), lambda f: (0, 0, f)), pl.BlockSpec((8, tf), lambda f: (0, f))],
        out_shape=[_sds((3, S, D), BF16), _sds((8, D), F32)],
        compiler_params=_cp(("parallel",)),
    )(bcu3, dp, cw, cb)


def _adamw(name, w3, m3, v3, l, parts, carry=None):
    L, R, C = w3.shape
    budget_rows = (1 << 18) // C
    tr = R if R <= budget_rows else _tile(R, 1 << (budget_rows.bit_length() - 1))
    n_parts = len(parts)
    n_carry = 0 if carry is None else 4
    c1 = 1.0 / (1.0 - ADAM_B1 ** ADAM_STEP)
    c2 = 1.0 / (1.0 - ADAM_B2 ** ADAM_STEP)

    def body(*refs):
        w_ref, m_ref, v_ref = refs[:3]
        p_refs = refs[3:3 + n_parts]
        og, od, om, ov = refs[3 + n_parts + n_carry:]
        g = p_refs[0][...].astype(F32)
        for p in p_refs[1:]:
            g = g + p[...].astype(F32)
        m = ADAM_B1 * m_ref[...] + (1.0 - ADAM_B1) * g
        v = ADAM_B2 * v_ref[...] + (1.0 - ADAM_B2) * jnp.square(g)
        og[...] = g
        om[...] = m
        ov[...] = v
        od[...] = -ADAM_LR * ((m * c1) / (jnp.sqrt(v * c2) + ADAM_EPS) + ADAM_WD * w_ref[...])

    lay = pl.BlockSpec((None, tr, C), lambda r: (l, r, 0))
    in_specs = [lay, lay, lay]
    args = [w3, m3, v3]
    for arr, idx in parts:
        in_specs.append(pl.BlockSpec((None, tr, C), functools.partial(lambda r, j: (j, r, 0), j=idx)))
        args.append(arr)
    aliases = {}
    if carry is not None:
        for j, buf in enumerate(carry):
            aliases[len(args)] = j
            in_specs.append(ANY_SPEC)
            args.append(buf)
    return _pcall(body, name=name, grid=(R // tr,), in_specs=in_specs, out_specs=[lay] * 4,
                  out_shape=[_sds((L, R, C), F32)] * 4, input_output_aliases=aliases,
                  compiler_params=_cp(("parallel",)))(*args)


def kernel(x, c, mod_w, mod_b, ln_g, ln_b, attn_w_qkv, attn_w_o, conv_w_in, conv_w, conv_b, conv_w_out, mlp_w1, mlp_b1, mlp_w2, mlp_b2, loss_target, m_mod_w, m_mod_b, m_ln_g, m_ln_b, m_attn_w_qkv, m_attn_w_o, m_conv_w_in, m_conv_w, m_conv_b, m_conv_w_out, m_mlp_w1, m_mlp_b1, m_mlp_w2, m_mlp_b2, v_mod_w, v_mod_b, v_ln_g, v_ln_b, v_attn_w_qkv, v_attn_w_o, v_conv_w_in, v_conv_w, v_conv_b, v_conv_w_out, v_mlp_w1, v_mlp_b1, v_mlp_w2, v_mlp_b2):
    xs, target = x[0], loss_target[0]
    S, D = xs.shape
    L = mod_w.shape[0]
    LA, LC = attn_w_qkv.shape[0], conv_w_in.shape[0]
    F = mlp_b1.shape[1]
    DB, FB, NC = D // N_DEV, F // N_DEV, mod_w.shape[2]
    CT = DB
    me = 4 * lax.axis_index("x") + 2 * lax.axis_index("y") + lax.axis_index("c")
    me_arr = jnp.reshape(me, (1,)).astype(jnp.int32)

    first_gather = _ag_start([_cast_bf16(attn_w_qkv, 0, me_arr, "cast_win_0"),
                              _cast_bf16(attn_w_o, 0, me_arr, "cast_wout_0")], [], "ag_start_0")

    small = [c, ln_g, ln_b, conv_w, conv_b]
    pack = jnp.concatenate([a.reshape(-1) for a in small]).reshape(-1, LANES)
    gath = _allgather([pack], "ag_small_params", deps=[first_gather[3]])[0].reshape(N_DEV, -1)
    offs, pos = [], 0
    for a in small:
        offs.append((pos, a.size))
        pos += a.size

    def unshard(idx, lead):
        o, n = offs[idx]
        a = gath[:, o:o + n].reshape((N_DEV,) + lead + (DB,))
        return jnp.moveaxis(a, 0, -2).reshape(lead + (D,))

    c_all = gath[:, :D]
    lng_full, lnb_full = unshard(1, (L, 2)), unshard(2, (L, 2))
    cw_full, cb_full = unshard(3, (LC, 3)), unshard(4, (LC,))

    cond16 = _silu_bf16(jnp.pad(c_all, ((0, 16 - N_DEV), (0, 0))), "silu_cond")
    tn_mod = NC // 3
    bias_mod = lax.dynamic_slice_in_dim(mod_b, me * NC, NC, axis=1).reshape(L, 1, NC)
    mod_part = _matmul(
        "mod_fwd", "nn", (L, NC // tn_mod), 1,
        cond16, pl.BlockSpec((16, D), lambda l, n: (0, 0)),
        mod_w, pl.BlockSpec((None, D, tn_mod), lambda l, n: (l, 0, n)),
        [_sds((L, 16, NC), F32)], [pl.BlockSpec((None, 16, tn_mod), lambda l, n: (l, 0, n))], None,
        extras=[bias_mod], extra_specs=[pl.BlockSpec((None, 1, tn_mod), lambda l, n: (l, 0, n))])[0]
    mod_gath = _allgather([mod_part[:, :N_DEV].reshape(-1, LANES)], "ag_mod")[0].reshape(N_DEV, L, N_DEV, NC)
    mod_me = lax.dynamic_index_in_dim(mod_gath, me, axis=2, keepdims=False)
    mod_me = jnp.moveaxis(mod_me, 0, 1).reshape(L, 6, 1, D)

    def mod_vecs(k):
        l, o = k // 2, 3 * (k % 2)
        return mod_me[l, o], mod_me[l, o + 1], mod_me[l, o + 2]

    def mixer_weights(l):
        if l % 2 == 0:
            return attn_w_qkv, attn_w_o, l // 2
        return conv_w_in, conv_w_out, l // 2

    K = 2 * L

    def group_shards(k):
        l = k // 2
        if k % 2 == 0:
            w_in3, w_out3, j = mixer_weights(l)
            return [_cast_bf16(w_in3, j, me_arr, f"cast_win_{l}"), _cast_bf16(w_out3, j, me_arr, f"cast_wout_{l}")]
        return [_cast_bf16(mlp_w1, l, me_arr, f"cast_w1_{l}"), _cast_bf16(mlp_w2, l, me_arr, f"cast_w2_{l}")]

    wg = [[None] * 4 for _ in range(L)]
    in_flight = {0: first_gather}

    def fetch_group(k, after):
        first, second = _ag_finish(_ag_wait(in_flight.pop(k), after, f"ag_wait_{k}"), f"ag_finish_{k}")
        if k + 1 < K:
            in_flight[k + 1] = _ag_start(group_shards(k + 1), [first], f"ag_start_{k + 1}")
        rows = D if k % 2 == 0 else F
        wg[k // 2][2 * (k % 2)], wg[k // 2][2 * (k % 2) + 1] = first, second.reshape(rows, D)

    n_ct = 3 * D // CT
    tm_big = _tile(S, 2048)

    def in_proj(l, h, out_dtype, deps):
        return _matmul(
            f"in_proj_{l}", "nn", (S // tm_big, n_ct), 1,
            h, pl.BlockSpec((tm_big, D), lambda i, ct: (i, 0)),
            wg[l][0], pl.BlockSpec((None, D, CT), lambda i, ct: (ct // 3, 0, ct % 3)),
            [_sds((3, S, D), out_dtype)],
            [pl.BlockSpec((None, tm_big, CT), lambda i, ct: (ct // N_DEV, i, ct % N_DEV))], None, deps=deps)[0]

    tm, tn_half = _tile(S, 512), _tile(D, 1024)

    def out_proj(l, a):
        return _matmul(
            f"out_proj_{l}", "nn", (D // tn_half, S // tm), 1,
            a, pl.BlockSpec((tm, D), lambda n, i: (i, 0)),
            wg[l][1], pl.BlockSpec((D, tn_half), lambda n, i: (0, n)),
            [_sds((S, D), F32)], [pl.BlockSpec((tm, tn_half), lambda n, i: (i, n))], None)[0]

    def mlp_up(l, h, deps):
        blk = pl.BlockSpec((tm, FB), lambda j, i: (i, j))
        return _matmul(
            f"mlp_up_{l}", "nn", (N_DEV, S // tm), 1,
            h, pl.BlockSpec((tm, D), lambda j, i: (i, 0)),
            wg[l][2], pl.BlockSpec((None, D, FB), lambda j, i: (j, 0, 0)),
            [_sds((S, F), BF16), _sds((S, F), BF16)], [blk, blk], None, epilogue="relu2",
            extras=[mlp_b1[l].reshape(1, F)], extra_specs=[pl.BlockSpec((1, FB), lambda j, i: (0, j))], deps=deps)

    tn_q = _tile(D, 512)

    def mlp_down(l, act):
        return _matmul(
            f"mlp_down_{l}", "nn", (D // tn_q, S // tm), 1,
            act, pl.BlockSpec((tm, F), lambda n, i: (i, 0)),
            wg[l][3], pl.BlockSpec((F, tn_q), lambda n, i: (0, n)),
            [_sds((S, D), F32)], [pl.BlockSpec((tm, tn_q), lambda n, i: (i, n))], None,
            extras=[mlp_b2[l].reshape(1, D)], extra_specs=[pl.BlockSpec((1, tn_q), lambda n, i: (0, n))])[0]

    xin, ys, hs, saved = [xs], [], [], []
    sh0, sc0, _ = mod_vecs(0)
    h = _modulate("modulate_in", xs, sc0, sh0)
    for k in range(K):
        l = k // 2
        hs.append(h)
        fetch_group(k, h)
        started = [in_flight[k + 1][3]] if k + 1 < K else []
        if k % 2 == 0 and l % 2 == 0:
            qkv3 = in_proj(l, h, BF16, started)
            o, totals = _attn_fwd(f"attn_fwd_{l}", qkv3)
            y = out_proj(l, o)
            saved.append((qkv3, o, totals))
        elif k % 2 == 0:
            bcu3 = in_proj(l, h, F32, started)
            j = l // 2
            p = _conv_fwd(f"conv_fwd_{l}", bcu3, cw_full[j], cb_full[j].reshape(1, D))
            y = out_proj(l, p)
            saved.append((bcu3, p))
        else:
            u, act = mlp_up(l, h, started)
            y = mlp_down(l, act)
            saved.append((u, act))
        ys.append(y)
        if k + 1 < K:
            _, _, gate = mod_vecs(k)
            sh_n, sc_n, _ = mod_vecs(k + 1)
            x_next, h = _ln_fwd(f"ln_fwd_{k}", xin[k], y, gate, lng_full[l, k % 2].reshape(1, D),
                                lnb_full[l, k % 2].reshape(1, D), sc_n, sh_n)
            xin.append(x_next)

    tm_g = _tile(D, 512)

    def grad_rows(name, a, dy, rows):
        return _matmul(
            name, "tn", (rows // tm_g, D // tn_half), 1,
            a, pl.BlockSpec((S, tm_g), lambda m, n: (0, m)),
            dy, pl.BlockSpec((S, tn_half), lambda m, n: (0, n)),
            [_sds((rows, D), BF16)], [pl.BlockSpec((tm_g, tn_half), lambda m, n: (m, n))], None)[0]

    def back_out_proj(l, dy, out_dtype):
        return _matmul(
            f"back_out_proj_{l}", "nt", (D // tn_half, S // tm), 1,
            dy, pl.BlockSpec((tm, D), lambda n, i: (i, 0)),
            wg[l][1], pl.BlockSpec((tn_half, D), lambda n, i: (n, 0)),
            [_sds((S, D), out_dtype)], [pl.BlockSpec((tm, tn_half), lambda n, i: (i, n))], None)[0]

    def back_in_proj(l, d3, deps):
        def part(p):
            return pl.BlockSpec((None, tm, D), lambda i, n: (p, i, 0))
        return _matmul(
            f"back_in_proj_{l}", "nt", (S // tm, D // tn_q), 1,
            d3, part(0),
            _unblock(wg[l][0], f"unblock_win_{l}"), pl.BlockSpec((tn_q, 3 * D), lambda i, n: (n, 0)),
            [_sds((S, D), F32)], [pl.BlockSpec((tm, tn_q), lambda i, n: (i, n))], None,
            extras=[d3, d3], extra_specs=[part(1), part(2)], widen="a", deps=deps)[0]

    def grad_in_proj(l, h, d3):
        def tile(r):
            return pl.BlockSpec((None, S, CT), lambda j, m: ((3 * j + r) // N_DEV, 0, (3 * j + r) % N_DEV))
        return _matmul(
            f"grad_in_proj_{l}", "tn", (N_DEV, D // tm_g), 1,
            h, pl.BlockSpec((S, tm_g), lambda j, m: (0, m)),
            d3, tile(0),
            [_sds((N_DEV, D, 3 * CT), BF16)],
            [pl.BlockSpec((None, tm_g, 3 * CT), lambda j, m: (j, m, 0))], None,
            extras=[d3, d3], extra_specs=[tile(1), tile(2)], widen="b")[0]

    def back_mlp_down(l, dy, u):
        blk = pl.BlockSpec((tm, FB), lambda j, i: (i, j))
        return _matmul(
            f"back_mlp_down_{l}", "nt", (N_DEV, S // tm), 1,
            dy, pl.BlockSpec((tm, D), lambda j, i: (i, 0)),
            wg[l][3], pl.BlockSpec((FB, D), lambda j, i: (j, 0)),
            [_sds((S, F), BF16), _sds((1, F), F32)], [blk, pl.BlockSpec((1, FB), lambda j, i: (0, j))], None,
            epilogue="drelu2", extras=[u], extra_specs=[blk], sem=("parallel", "arbitrary"))

    def back_mlp_up(l, du, deps):
        return _matmul(
            f"back_mlp_up_{l}", "nt", (S // tm, D // tn_q), 1,
            du, pl.BlockSpec((tm, F), lambda i, n: (i, 0)),
            _unblock(wg[l][2], f"unblock_w1_{l}"), pl.BlockSpec((tn_q, F), lambda i, n: (n, 0)),
            [_sds((S, D), F32)], [pl.BlockSpec((tm, tn_q), lambda i, n: (i, n))], None, deps=deps)[0]

    def grad_mlp_up(l, h, du):
        return _matmul(
            f"grad_mlp_up_{l}", "tn", (N_DEV, D // tm_g), 1,
            h, pl.BlockSpec((S, tm_g), lambda j, m: (0, m)),
            du, pl.BlockSpec((S, FB), lambda j, m: (0, j)),
            [_sds((N_DEV, D, FB), BF16)], [pl.BlockSpec((None, tm_g, FB), lambda j, m: (j, m, 0))], None)[0]

    dmod = [[None] * 6 for _ in range(L)]
    dlng = [[None, None] for _ in range(L)]
    dlnb = [[None, None] for _ in range(L)]
    db1, db2 = [None] * L, [None] * L
    dcw, dcb = [None] * LC, [None] * LC

    carries = {"attn_in": None, "attn_out": None, "conv_in": None, "conv_out": None, "w1": None, "w2": None}
    moments = {"attn_in": (attn_w_qkv, m_attn_w_qkv, v_attn_w_qkv), "attn_out": (attn_w_o, m_attn_w_o, v_attn_w_o),
               "conv_in": (conv_w_in, m_conv_w_in, v_conv_w_in), "conv_out": (conv_w_out, m_conv_w_out, v_conv_w_out),
               "w1": (mlp_w1, m_mlp_w1, v_mlp_w1), "w2": (mlp_w2, m_mlp_w2, v_mlp_w2)}
    pending = []

    def finish_pending(after):
        if not pending:
            return
        k, started = pending.pop()
        ps, r2 = _rs_chips_wait(started, after, f"rs_wait_{k}")
        l = k // 2
        if k % 2 == 1:
            keys = [("w1", l), ("w2", l)]
        else:
            kind = "attn" if l % 2 == 0 else "conv"
            keys = [(kind + "_in", l // 2), (kind + "_out", l // 2)]
        for t, (key, idx) in enumerate(keys):
            w3, m3, v3 = moments[key]
            parts = [(ps[t], 0), (r2[t], 1), (r2[t], 2), (r2[t], 3)]
            carries[key] = _adamw(f"adamw_{key}_{l}", w3, m3, v3, idx, parts, carries[key])

    def reduce_group(k, gs, after):
        finish_pending(after)
        r1 = _rs_sibling(gs, f"rs_sibling_{k}")
        ps = [_rs_add(g, r, me_arr, f"rs_add_{k}_{t}") for t, (g, r) in enumerate(zip(gs, r1))]
        started = _rs_chips_start(ps, f"rs_start_{k}")
        pending.append((k, started))
        return started[4]

    def take_ln_rows(k, vec):
        l, o = k // 2, 3 * (k % 2)
        dlng[l][k % 2], dlnb[l][k % 2] = vec[2], vec[3]
        dmod[l][o + 2] = vec[4]
        if k % 2 == 1:
            db2[l] = vec[5]

    def take_mod_rows(k, vec):
        l, o = k // 2, 3 * (k % 2)
        dmod[l][o + 1], dmod[l][o] = vec[0], vec[1]

    lk = K - 1
    _, _, gate = mod_vecs(lk)
    dy, dxr, vec = _boundary(f"boundary_{lk}", "last", (target, xin[lk], ys[lk]),
                             (gate, lng_full[lk // 2, 1].reshape(1, D), lnb_full[lk // 2, 1].reshape(1, D)))
    take_ln_rows(lk, vec)
    loss_local = 0.5 * vec[6, 0] / D
    grad_x = None
    for k in range(K - 1, -1, -1):
        l = k // 2
        if k % 2 == 1:
            u, act = saved[k]
            g_w2 = grad_rows(f"grad_mlp_down_{l}", act, dy, F).reshape(N_DEV, FB, D)
            du, db1_l = back_mlp_down(l, dy, u)
            db1[l] = db1_l[0]
            g_w1 = grad_mlp_up(l, hs[k], du)
            token = reduce_group(k, [g_w1, g_w2], du)
            dh = back_mlp_up(l, du, [token])
        else:
            if l % 2 == 0:
                qkv3, o, totals = saved[k]
                g_out = grad_rows(f"grad_out_proj_{l}", o, dy, D).reshape(N_DEV, DB, D)
                do = back_out_proj(l, dy, BF16)
                d3 = _attn_bwd(f"attn_bwd_{l}", qkv3, do, totals)
            else:
                bcu3, p = saved[k]
                j = l // 2
                g_out = grad_rows(f"grad_out_proj_{l}", p, dy, D).reshape(N_DEV, DB, D)
                dp = back_out_proj(l, dy, F32)
                d3, cvec = _conv_bwd(f"conv_bwd_{l}", bcu3, dp, cw_full[j], cb_full[j].reshape(1, D))
                dcb[j], dcw[j] = cvec[0], cvec[1:4]
            g_in = grad_in_proj(l, hs[k], d3)
            token = reduce_group(k, [g_in, g_out], d3)
            dh = back_in_proj(l, d3, [token])
        sh_k, sc_k, _ = mod_vecs(k)
        if k > 0:
            kp = k - 1
            _, _, gate = mod_vecs(kp)
            dy, dxr, vec = _boundary(
                f"boundary_{kp}", "mid", (dxr, dh, xin[kp], ys[kp]),
                (sc_k, gate, lng_full[kp // 2, kp % 2].reshape(1, D), lnb_full[kp // 2, kp % 2].reshape(1, D)),
                deps=[token])
            take_mod_rows(k, vec)
            take_ln_rows(kp, vec)
        else:
            grad_x, vec = _boundary("boundary_in", "first", (dxr, dh, xs), (sc_k,), deps=[token])
            take_mod_rows(k, vec)
    finish_pending(grad_x)
    results = carries

    flat = [jnp.stack([jnp.stack(r) for r in dmod]).reshape(-1), jnp.stack(db1).reshape(-1),
            jnp.stack(db2).reshape(-1), jnp.stack([jnp.stack(r) for r in dlng]).reshape(-1),
            jnp.stack([jnp.stack(r) for r in dlnb]).reshape(-1), jnp.stack(dcw).reshape(-1),
            jnp.stack(dcb).reshape(-1)]
    sizes = [a.size for a in flat]
    small_g = _allgather([jnp.concatenate(flat).reshape(-1, LANES)], "ag_small_grads")[0]
    total = _sum_devices(small_g, "sum_small_grads").reshape(-1)
    starts = [sum(sizes[:i]) for i in range(len(sizes))]

    def piece(i, shape):
        return total[starts[i]:starts[i] + sizes[i]].reshape(shape)

    def my_block(a):
        return lax.dynamic_slice_in_dim(a, me * DB, DB, axis=a.ndim - 1)

    g_mod_b = piece(0, (L, 6 * D))
    g_b1, g_b2 = piece(1, (L, F)), piece(2, (L, D))
    g_lng, g_lnb = my_block(piece(3, (L, 2, D))), my_block(piece(4, (L, 2, D)))
    g_cw, g_cb = my_block(piece(5, (LC, 3, D))), my_block(piece(6, (LC, D)))

    dmod_all = small_g.reshape(N_DEV, -1)[:, :sizes[0]].reshape(N_DEV, L, 6 * D)
    dmod_mine = jnp.moveaxis(lax.dynamic_slice_in_dim(dmod_all, me * NC, NC, axis=2), 0, 1)
    dmod16 = jnp.pad(dmod_mine, ((0, 0), (0, 16 - N_DEV), (0, 0)))
    tm_mod = _tile(D, 512)
    g_mod_w = _matmul(
        "grad_mod_w", "tn", (L, D // tm_mod), 1,
        cond16, pl.BlockSpec((16, tm_mod), lambda l, m: (0, m)),
        dmod16, pl.BlockSpec((None, 16, NC), lambda l, m: (l, 0, 0)),
        [_sds((L, D, NC), F32)], [pl.BlockSpec((None, tm_mod, NC), lambda l, m: (l, m, 0))], None)[0]
    res_mod_w = None
    for l in range(L):
        res_mod_w = _adamw(f"adamw_mod_w_{l}", mod_w, m_mod_w, v_mod_w, l, [(g_mod_w, l)], res_mod_w)

    small_w = [mod_b, ln_g, ln_b, conv_w, conv_b, mlp_b1, mlp_b2]
    small_m = [m_mod_b, m_ln_g, m_ln_b, m_conv_w, m_conv_b, m_mlp_b1, m_mlp_b2]
    small_v = [v_mod_b, v_ln_g, v_ln_b, v_conv_w, v_conv_b, v_mlp_b1, v_mlp_b2]
    small_gr = [g_mod_b, g_lng, g_lnb, g_cw, g_cb, g_b1, g_b2]
    n_small = sum(a.size for a in small_w)
    pad = (-n_small) % (8 * LANES)

    def packed(arrs):
        v = jnp.concatenate([a.reshape(-1) for a in arrs])
        return jnp.pad(v, (0, pad)).reshape(1, -1, LANES)

    sg, sd, sm, sv = _adamw("adamw_small", packed(small_w), packed(small_m), packed(small_v), 0,
                            [(packed(small_gr), 0)])

    def unpacked(buf):
        out, pos = [], 0
        flat_buf = buf.reshape(-1)
        for a in small_w:
            out.append(flat_buf[pos:pos + a.size].reshape(a.shape))
            pos += a.size
        return out

    small_res = [unpacked(b) for b in (sg, sd, sm, sv)]
    loss = lax.psum(loss_local, ("x", "y", "c"))

    def leaf(kind):
        s = small_res[kind]
        return [res_mod_w[kind], s[0], s[1], s[2], results["attn_in"][kind], results["attn_out"][kind],
                results["conv_in"][kind], s[3], s[4], results["conv_out"][kind], results["w1"][kind], s[5],
                results["w2"][kind], s[6]]

    return (loss, grad_x[None], *leaf(0), *leaf(1), *leaf(2), *leaf(3))
```

```python
import functools

import jax
import jax.numpy as jnp
from jax import lax
from jax.experimental import pallas as pl
from jax.experimental.pallas import tpu as pltpu

F32 = jnp.float32
BF16 = jnp.bfloat16
N_DEV = 8
HEAD_DIM = 128
LANES = 128
DEEPNORM_ALPHA = 8.0 ** 0.25
LN_EPS = 1e-5
ADAM_LR = 0.001
ADAM_B1 = 0.9
ADAM_B2 = 0.999
ADAM_EPS = 1e-08
ADAM_WD = 0.01
ADAM_STEP = 10
VMEM_LIMIT_BYTES = 52 * 1024 * 1024
MESH = pl.DeviceIdType.MESH
HBM_SPEC = pl.BlockSpec(memory_space=pltpu.HBM)
ANY_SPEC = pl.BlockSpec(memory_space=pl.ANY)
SEM_SPEC = pl.BlockSpec(memory_space=pltpu.SEMAPHORE)
EFFECT = pltpu.SideEffectType.DATAFLOW_SIDE_EFFECTING


def _pcall(body, **kw):
    return pl.pallas_call(body, **kw)


def _cp(sem=None):
    return pltpu.CompilerParams(dimension_semantics=sem, vmem_limit_bytes=VMEM_LIMIT_BYTES)


def _tile(n, pref):
    t = min(n, pref)
    assert n % t == 0, (n, pref)
    return t


def _sds(shape, dtype):
    return jax.ShapeDtypeStruct(tuple(shape), dtype)


def _allgather(xs, name, deps=()):
    n, nd = len(xs), len(deps)

    def body(*refs):
        x_refs = refs[:n]
        out_refs = refs[n + nd:2 * n + nd]
        send_sems, recv_sems, local_sems = refs[2 * n + nd:]
        x, y, c = lax.axis_index("x"), lax.axis_index("y"), lax.axis_index("c")
        me, sibling = (x, y, c), (x, y, 1 - c)
        chips = [(1 - x, y), (x, 1 - y), (1 - x, 1 - y)]

        def slot(t, px, py, pc):
            return out_refs[t].at[4 * px + 2 * py + pc]

        def copy(t, k, block, to, src=None):
            return pltpu.make_async_remote_copy(
                src_ref=slot(t, *block) if src is None else src, dst_ref=slot(t, *block),
                send_sem=send_sems.at[t, k], recv_sem=recv_sems.at[t, k],
                device_id=to, device_id_type=MESH)

        mine = [pltpu.make_async_copy(x_refs[t], slot(t, *me), local_sems.at[t]) for t in range(n)]
        for cp in mine:
            cp.start()
        first = []
        for t in range(n):
            first.append(copy(t, 0, me, sibling, src=x_refs[t]))
            for j, chip in enumerate(chips):
                first.append(copy(t, 1 + j, me, (*chip, c), src=x_refs[t]))
        for cp in first:
            cp.start()
        passed = []
        for j, chip in enumerate(chips):
            for t in range(n):
                copy(t, 1 + j, (*chip, c), me).wait_recv()
                fwd = copy(t, 4 + j, (*chip, c), sibling)
                fwd.start()
                passed.append(fwd)
        for t in range(n):
            copy(t, 0, sibling, me).wait_recv()
            for j, chip in enumerate(chips):
                copy(t, 4 + j, (*chip, 1 - c), me).wait_recv()
        for cp in first + passed:
            cp.wait_send()
        for cp in mine:
            cp.wait()

    outs = _pcall(
        body, name=name,
        out_shape=[_sds((N_DEV,) + a.shape, a.dtype) for a in xs],
        in_specs=[HBM_SPEC] * n + [ANY_SPEC] * nd, out_specs=[HBM_SPEC] * n,
        scratch_shapes=[pltpu.SemaphoreType.DMA((n, 7)), pltpu.SemaphoreType.DMA((n, 7)),
                        pltpu.SemaphoreType.DMA((n,))],
    )(*xs, *deps)
    return list(outs)


def _peers(x, y, c):
    return [(x, y, 1 - c), (1 - x, y, c), (x, 1 - y, c), (1 - x, 1 - y, c)]


def _block_owner(x, y, c, k):
    return ((1 - x) if k >> 1 else x, (1 - y) if k & 1 else y, c)


def _hbm(a):
    return pltpu.with_memory_space_constraint(a, pltpu.HBM)


def _ag_start(lands, deps, name):
    n, nd = len(lands), len(deps)

    def body(*refs):
        land_refs = refs[:n]
        send_sems, recv_sems = refs[n + nd], refs[n + nd + 1]
        token = refs[-1]
        x, y, c = lax.axis_index("x"), lax.axis_index("y"), lax.axis_index("c")
        for t in range(n):
            mine = land_refs[t].at[4 * x + 2 * y + c]
            for k, to in enumerate(_peers(x, y, c)):
                pltpu.make_async_remote_copy(
                    src_ref=mine, dst_ref=mine,
                    send_sem=send_sems.at[4 * t + k], recv_sem=recv_sems.at[4 * t + k],
                    device_id=to, device_id_type=MESH).start()
        token[...] = jnp.zeros_like(token)

    outs = _pcall(
        body, name=name,
        out_shape=(pltpu.SemaphoreType.DMA((4 * n,)), pltpu.SemaphoreType.DMA((4 * n,)),
                   *[pltpu.HBM(a.shape, a.dtype) for a in lands], _sds((8, LANES), F32)),
        in_specs=[HBM_SPEC] * n + [ANY_SPEC] * nd,
        out_specs=(SEM_SPEC, SEM_SPEC, *[HBM_SPEC] * n, pl.BlockSpec(memory_space=pltpu.VMEM)),
        input_output_aliases={i: 2 + i for i in range(n)},
        compiler_params=pltpu.CompilerParams(has_side_effects=EFFECT),
    )(*[_hbm(a) for a in lands], *deps)
    return outs[0], outs[1], list(outs[2:2 + n]), outs[-1]


def _ag_wait(started, after, name):
    send_sems, recv_sems, lands, _ = started
    n = len(lands)

    def body(*refs):
        land_refs = refs[:n]
        send_sems, recv_sems = refs[n], refs[n + 1]
        x, y, c = lax.axis_index("x"), lax.axis_index("y"), lax.axis_index("c")
        for t in range(n):
            for k, (px, py, pc) in enumerate(_peers(x, y, c)):
                cp = pltpu.make_async_remote_copy(
                    src_ref=land_refs[t].at[4 * x + 2 * y + c], dst_ref=land_refs[t].at[4 * px + 2 * py + pc],
                    send_sem=send_sems.at[4 * t + k], recv_sem=recv_sems.at[4 * t + k],
                    device_id=(px, py, pc), device_id_type=MESH)
                cp.wait_send()
                cp.wait_recv()

    outs = _pcall(
        body, name=name,
        out_shape=tuple(pltpu.HBM(a.shape, a.dtype) for a in lands),
        in_specs=[HBM_SPEC] * n + [SEM_SPEC, SEM_SPEC, ANY_SPEC],
        out_specs=[HBM_SPEC] * n,
        input_output_aliases={i: i for i in range(n)},
        compiler_params=pltpu.CompilerParams(has_side_effects=EFFECT),
    )(*lands, send_sems, recv_sems, after)
    return list(outs)


def _ag_finish(lands, name):
    n = len(lands)

    def body(*refs):
        out_refs = refs[n:2 * n]
        send_sems, recv_sems = refs[2 * n:]
        x, y, c = lax.axis_index("x"), lax.axis_index("y"), lax.axis_index("c")
        chips = [(1 - x, y), (x, 1 - y), (1 - x, 1 - y)]

        def passing(t, j, pc):
            px, py = chips[j]
            rows = out_refs[t].at[4 * px + 2 * py + pc]
            return pltpu.make_async_remote_copy(
                src_ref=rows, dst_ref=rows, send_sem=send_sems.at[t, j], recv_sem=recv_sems.at[t, j],
                device_id=(x, y, 1 - c), device_id_type=MESH)

        sends = [passing(t, j, c) for t in range(n) for j in range(3)]
        for cp in sends:
            cp.start()
        for t in range(n):
            for j in range(3):
                passing(t, j, 1 - c).wait_recv()
        for cp in sends:
            cp.wait_send()

    outs = _pcall(
        body, name=name,
        out_shape=[_sds(a.shape, a.dtype) for a in lands],
        in_specs=[HBM_SPEC] * n, out_specs=[HBM_SPEC] * n,
        input_output_aliases={i: i for i in range(n)},
        scratch_shapes=[pltpu.SemaphoreType.DMA((n, 3)), pltpu.SemaphoreType.DMA((n, 3))],
    )(*lands)
    return list(outs)


def _rs_chips_start(ps, name):
    n = len(ps)

    def body(*refs):
        p_refs, land_refs = refs[:n], refs[n:2 * n]
        send_sems, recv_sems = refs[2 * n], refs[2 * n + 1]
        token = refs[-1]
        x, y, c = lax.axis_index("x"), lax.axis_index("y"), lax.axis_index("c")
        for t in range(n):
            for k in range(1, 4):
                pltpu.make_async_remote_copy(
                    src_ref=p_refs[t].at[k], dst_ref=land_refs[t].at[k],
                    send_sem=send_sems.at[3 * t + k - 1], recv_sem=recv_sems.at[3 * t + k - 1],
                    device_id=_block_owner(x, y, c, k), device_id_type=MESH).start()
        token[...] = jnp.zeros_like(token)

    lands = [lax.empty(p.shape, p.dtype) for p in ps]
    outs = _pcall(
        body, name=name,
        out_shape=(pltpu.SemaphoreType.DMA((3 * n,)), pltpu.SemaphoreType.DMA((3 * n,)),
                   *[pltpu.HBM(a.shape, a.dtype) for a in ps], *[pltpu.HBM(a.shape, a.dtype) for a in lands],
                   _sds((8, LANES), F32)),
        in_specs=[HBM_SPEC] * (2 * n),
        out_specs=(SEM_SPEC, SEM_SPEC, *[HBM_SPEC] * (2 * n), pl.BlockSpec(memory_space=pltpu.VMEM)),
        input_output_aliases={i: 2 + i for i in range(2 * n)},
        compiler_params=pltpu.CompilerParams(has_side_effects=EFFECT),
    )(*[_hbm(a) for a in ps], *[_hbm(a) for a in lands])
    return outs[0], outs[1], list(outs[2:2 + n]), list(outs[2 + n:2 + 2 * n]), outs[-1]


def _rs_chips_wait(started, after, name):
    send_sems, recv_sems, ps, lands, _ = started
    n = len(ps)

    def body(*refs):
        p_refs, land_refs = refs[:n], refs[n:2 * n]
        send_sems, recv_sems = refs[2 * n], refs[2 * n + 1]
        x, y, c = lax.axis_index("x"), lax.axis_index("y"), lax.axis_index("c")
        for t in range(n):
            for k in range(1, 4):
                cp = pltpu.make_async_remote_copy(
                    src_ref=p_refs[t].at[k], dst_ref=land_refs[t].at[k],
                    send_sem=send_sems.at[3 * t + k - 1], recv_sem=recv_sems.at[3 * t + k - 1],
                    device_id=_block_owner(x, y, c, k), device_id_type=MESH)
                cp.wait_send()
                cp.wait_recv()

    outs = _pcall(
        body, name=name,
        out_shape=(*[pltpu.HBM(a.shape, a.dtype) for a in ps], *[pltpu.HBM(a.shape, a.dtype) for a in lands]),
        in_specs=[HBM_SPEC] * (2 * n) + [SEM_SPEC, SEM_SPEC, ANY_SPEC],
        out_specs=[HBM_SPEC] * (2 * n),
        input_output_aliases={i: i for i in range(2 * n)},
        compiler_params=pltpu.CompilerParams(has_side_effects=EFFECT),
    )(*ps, *lands, send_sems, recv_sems, after)
    return list(outs[:n]), list(outs[n:])


def _rs_sibling_copies(g_refs, land_refs, send_sems, recv_sems):
    x, y, c = lax.axis_index("x"), lax.axis_index("y"), lax.axis_index("c")
    me = 4 * x + 2 * y + c
    return [pltpu.make_async_remote_copy(
        src_ref=g_refs[t].at[jnp.bitwise_xor(2 * k + 1, me)], dst_ref=land_refs[t].at[k],
        send_sem=send_sems.at[4 * t + k], recv_sem=recv_sems.at[4 * t + k],
        device_id=(x, y, 1 - c), device_id_type=MESH) for t in range(len(g_refs)) for k in range(4)]


def _rs_sibling_start(gs, name):
    n = len(gs)

    def body(*refs):
        for cp in _rs_sibling_copies(refs[:n], refs[n:2 * n], refs[2 * n], refs[2 * n + 1]):
            cp.start()
        refs[-1][...] = jnp.zeros_like(refs[-1])

    lands = [lax.empty((4,) + g.shape[1:], g.dtype) for g in gs]
    outs = _pcall(
        body, name=name,
        out_shape=(pltpu.SemaphoreType.DMA((4 * n,)), pltpu.SemaphoreType.DMA((4 * n,)),
                   *[pltpu.HBM(a.shape, a.dtype) for a in gs], *[pltpu.HBM(a.shape, a.dtype) for a in lands],
                   _sds((8, LANES), F32)),
        in_specs=[HBM_SPEC] * (2 * n),
        out_specs=(SEM_SPEC, SEM_SPEC, *[HBM_SPEC] * (2 * n), pl.BlockSpec(memory_space=pltpu.VMEM)),
        input_output_aliases={i: 2 + i for i in range(2 * n)},
        compiler_params=pltpu.CompilerParams(has_side_effects=EFFECT),
    )(*[_hbm(a) for a in gs], *[_hbm(a) for a in lands])
    return outs[0], outs[1], list(outs[2:2 + n]), list(outs[2 + n:2 + 2 * n]), outs[-1]


def _rs_sibling_wait(started, after, name):
    send_sems, recv_sems, gs, lands, _ = started
    n = len(gs)

    def body(*refs):
        for cp in _rs_sibling_copies(refs[:n], refs[n:2 * n], refs[2 * n], refs[2 * n + 1]):
            cp.wait_send()
            cp.wait_recv()

    outs = _pcall(
        body, name=name,
        out_shape=(*[pltpu.HBM(a.shape, a.dtype) for a in gs], *[pltpu.HBM(a.shape, a.dtype) for a in lands]),
        in_specs=[HBM_SPEC] * (2 * n) + [SEM_SPEC, SEM_SPEC, ANY_SPEC],
        out_specs=[HBM_SPEC] * (2 * n),
        input_output_aliases={i: i for i in range(2 * n)},
        compiler_params=pltpu.CompilerParams(has_side_effects=EFFECT),
    )(*gs, *lands, send_sems, recv_sems, after)
    return list(outs[:n]), list(outs[n:])


def _rs_add(g, r1, me_arr, name):
    _, R, C = g.shape
    tr = _tile(R, 256)

    def body(me_ref, g_ref, r_ref, o_ref):
        o_ref[...] = (g_ref[...].astype(F32) + r_ref[...].astype(F32)).astype(BF16)

    grid_spec = pltpu.PrefetchScalarGridSpec(
        num_scalar_prefetch=1, grid=(4, R // tr),
        in_specs=[pl.BlockSpec((None, tr, C), lambda k, r, me: (jnp.bitwise_xor(2 * k, me[0]), r, 0)),
                  pl.BlockSpec((None, tr, C), lambda k, r, me: (k, r, 0))],
        out_specs=pl.BlockSpec((None, tr, C), lambda k, r, me: (k, r, 0)))
    return _pcall(body, name=name, grid_spec=grid_spec, out_shape=_sds((4, R, C), BF16),
                  compiler_params=_cp(("parallel", "parallel")))(me_arr, g, r1)


_DIMS = {"nn": (((1,), (0,)), ((), ())), "nt": (((1,), (1,)), ((), ())), "tn": (((0,), (0,)), ((), ()))}


def _dot(a, b, mode="nn"):
    return lax.dot_general(a, b, _DIMS[mode], preferred_element_type=F32)


def _matmul(name, mode, grid, nk, a, a_spec, b, b_spec, out_shapes, out_specs, acc_shape,
            epilogue="plain", extras=(), extra_specs=(), sem=None, widen=None, deps=()):
    n_extra, n_out = len(extras) + len(deps), len(out_shapes)
    extras, extra_specs = list(extras) + list(deps), list(extra_specs) + [ANY_SPEC] * len(deps)

    def body(*refs):
        a_ref, b_ref = refs[0], refs[1]
        ex = refs[2:2 + n_extra - len(deps)]
        outs = refs[2 + n_extra:2 + n_extra + n_out]
        a_tile, b_tile = a_ref[...].astype(BF16), b_ref[...].astype(BF16)
        if widen == "a":
            a_tile = jnp.concatenate([a_tile] + [e[...].astype(BF16) for e in ex], axis=1)
        elif widen == "b":
            b_tile = jnp.concatenate([b_tile] + [e[...].astype(BF16) for e in ex], axis=1)
        part = _dot(a_tile, b_tile, mode)

        def finish(acc):
            if epilogue == "plain":
                if len(ex) and widen is None:
                    acc = acc + ex[0][...]
                outs[0][...] = acc.astype(outs[0].dtype)
            elif epilogue == "relu2":
                u = acc + ex[0][...]
                outs[0][...] = u.astype(outs[0].dtype)
                outs[1][...] = jnp.square(jnp.maximum(u, 0.0)).astype(outs[1].dtype)
            else:
                du = acc * (2.0 * jnp.maximum(ex[0][...].astype(F32), 0.0))
                outs[0][...] = du.astype(outs[0].dtype)
                col = jnp.sum(du, axis=0, keepdims=True)
                inner = pl.program_id(1)

                @pl.when(inner == 0)
                def _():
                    outs[1][...] = col

                @pl.when(inner > 0)
                def _():
                    outs[1][...] += col

        if nk == 1:
            finish(part)
        else:
            acc_ref = refs[2 + n_extra + n_out]
            kidx = pl.program_id(len(grid) - 1)

            @pl.when(kidx == 0)
            def _():
                acc_ref[...] = part

            @pl.when(kidx > 0)
            def _():
                acc_ref[...] += part

            @pl.when(kidx == nk - 1)
            def _():
                finish(acc_ref[...])

    if sem is None:
        sem = ("parallel",) * (len(grid) - (1 if nk > 1 else 0)) + (("arbitrary",) if nk > 1 else ())
    outs = _pcall(
        body, name=name, grid=grid,
        in_specs=[a_spec, b_spec] + list(extra_specs), out_specs=list(out_specs),
        out_shape=list(out_shapes),
        scratch_shapes=[pltpu.VMEM(acc_shape, F32)] if nk > 1 else [],
        compiler_params=_cp(sem),
    )(a, b, *extras)
    return list(outs)


def _cast_bf16(w3, l, me_arr, name, deps=()):
    _, R, C = w3.shape
    tr = _tile(R, 256)

    def body(me_ref, x_ref, *rest):
        rest[-1][...] = x_ref[...].astype(BF16)

    grid_spec = pltpu.PrefetchScalarGridSpec(
        num_scalar_prefetch=1, grid=(R // tr,),
        in_specs=[pl.BlockSpec((None, tr, C), lambda r, me: (l, r, 0))] + [ANY_SPEC] * len(deps),
        out_specs=pl.BlockSpec((None, tr, C), lambda r, me: (me[0], r, 0)))
    return _pcall(body, name=name, grid_spec=grid_spec, out_shape=_sds((N_DEV, R, C), BF16),
                  compiler_params=_cp(("parallel",)))(me_arr, w3, *deps)


def _unblock(wb, name):
    n, R, C = wb.shape

    def body(x_ref, o_ref):
        o_ref[...] = x_ref[...]

    return _pcall(body, name=name, grid=(n,),
                  in_specs=[pl.BlockSpec((None, R, C), lambda j: (j, 0, 0))],
                  out_specs=pl.BlockSpec((R, C), lambda j: (0, j)),
                  out_shape=_sds((R, n * C), wb.dtype), compiler_params=_cp(("parallel",)))(wb)


def _silu_bf16(c16, name):
    def body(c_ref, o_ref):
        v = c_ref[...]
        o_ref[...] = (v / (1.0 + jnp.exp(-v))).astype(BF16)

    return _pcall(body, name=name, out_shape=_sds(c16.shape, BF16))(c16)


def _sum_devices(g, name):
    def body(g_ref, o_ref):
        acc = g_ref[0]
        for d in range(1, N_DEV):
            acc = acc + g_ref[d]
        o_ref[...] = acc

    return _pcall(body, name=name, out_shape=_sds(g.shape[1:], F32))(g)


def _modulate(name, x, sc, sh, deps=()):
    S, D = x.shape
    ts = _tile(S, 256)

    def body(x_ref, sc_ref, sh_ref, *rest):
        rest[-1][...] = (x_ref[...] * (1.0 + sc_ref[...]) + sh_ref[...]).astype(BF16)

    row = pl.BlockSpec((ts, D), lambda i: (i, 0))
    vec = pl.BlockSpec((1, D), lambda i: (0, 0))
    return _pcall(body, name=name, grid=(S // ts,), in_specs=[row, vec, vec] + [ANY_SPEC] * len(deps),
                  out_specs=row, out_shape=_sds((S, D), BF16),
                  compiler_params=_cp(("parallel",)))(x, sc, sh, *deps)


def _ln_stats(r):
    mu = jnp.mean(r, axis=-1, keepdims=True)
    xc = r - mu
    var = jnp.mean(xc * xc, axis=-1, keepdims=True)
    rstd = lax.rsqrt(var + LN_EPS)
    return xc * rstd, rstd


def _ln_fwd(name, xin, y, gate, lng, lnb, sc, sh):
    S, D = xin.shape
    ts = _tile(S, 256)

    def body(x_ref, y_ref, g_ref, lg_ref, lb_ref, sc_ref, sh_ref, xo_ref, h_ref):
        r = DEEPNORM_ALPHA * x_ref[...] + (1.0 + g_ref[...]) * y_ref[...]
        xhat, _ = _ln_stats(r)
        xo = xhat * lg_ref[...] + lb_ref[...]
        xo_ref[...] = xo
        h_ref[...] = (xo * (1.0 + sc_ref[...]) + sh_ref[...]).astype(BF16)

    row = pl.BlockSpec((ts, D), lambda i: (i, 0))
    vec = pl.BlockSpec((1, D), lambda i: (0, 0))
    return _pcall(body, name=name, grid=(S // ts,), in_specs=[row, row] + [vec] * 5,
                  out_specs=[row, row], out_shape=[_sds((S, D), F32), _sds((S, D), BF16)],
                  compiler_params=_cp(("parallel",)))(xin, y, gate, lng, lnb, sc, sh)


def _boundary(name, mode, big, vecs, deps=()):
    S, D = big[0].shape
    ts = _tile(S, 128)
    nb, nv, nd = len(big), len(vecs), len(deps)

    def body(*refs):
        b = [r[...] for r in refs[:nb]]
        v = [r[...] for r in refs[nb:nb + nv]]
        outs = refs[nb + nv + nd:]
        vec_ref = outs[-1]
        i = pl.program_id(0)
        rows = [None] * 8

        def colsum(a):
            return jnp.sum(a, axis=0, keepdims=True)

        if mode == "first":
            dxr, dh, xk = b
            sc = v[0]
            outs[0][...] = dxr + dh * (1.0 + sc)
            rows[0], rows[1] = colsum(dh * xk), colsum(dh)
        else:
            if mode == "mid":
                dxr, dh, xprev, yk = b
                sc, gate, lng, lnb = v
            else:
                target, xprev, yk = b
                gate, lng, lnb = v
            r = DEEPNORM_ALPHA * xprev + (1.0 + gate) * yk
            xhat, rstd = _ln_stats(r)
            if mode == "mid":
                xk = xhat * lng + lnb
                d = dxr + dh * (1.0 + sc)
                rows[0], rows[1] = colsum(dh * xk), colsum(dh)
            else:
                err = xhat * lng + lnb - target
                d = err * (1.0 / D)
                rows[6] = jnp.zeros((1, D), F32) + jnp.sum(err * err)
            rows[2], rows[3] = colsum(d * xhat), colsum(d)
            dxhat = d * lng
            m1 = jnp.mean(dxhat, axis=-1, keepdims=True)
            m2 = jnp.mean(dxhat * xhat, axis=-1, keepdims=True)
            dr = rstd * (dxhat - m1 - xhat * m2)
            dy = (1.0 + gate) * dr
            rows[4], rows[5] = colsum(dr * yk), colsum(dy)
            outs[0][...] = dy.astype(BF16)
            outs[1][...] = DEEPNORM_ALPHA * dr
        @pl.when(i == 0)
        def _():
            vec_ref[...] = jnp.zeros_like(vec_ref)

        for j, row_j in enumerate(rows):
            if row_j is not None:
                vec_ref[j:j + 1, :] += row_j

    row = pl.BlockSpec((ts, D), lambda i: (i, 0))
    vec = pl.BlockSpec((1, D), lambda i: (0, 0))
    vec_out = pl.BlockSpec((8, D), lambda i: (0, 0))
    if mode == "first":
        out_specs, out_shape = [row, vec_out], [_sds((S, D), F32), _sds((8, D), F32)]
    else:
        out_specs = [row, row, vec_out]
        out_shape = [_sds((S, D), BF16), _sds((S, D), F32), _sds((8, D), F32)]
    return _pcall(body, name=name, grid=(S // ts,), in_specs=[row] * nb + [vec] * nv + [ANY_SPEC] * nd,
                  out_specs=out_specs, out_shape=out_shape,
                  compiler_params=_cp(("arbitrary",)))(*big, *vecs, *deps)


def _split_bf16(a):
    hi = a.astype(BF16)
    lo = (a - hi.astype(F32)).astype(BF16)
    return hi, lo


def _cumsum_matrix(tri):
    half = jnp.concatenate([tri.astype(BF16), jnp.ones((LANES, LANES), BF16)], axis=1)
    return jnp.concatenate([half, half], axis=0)


def _chunk_sums(a, mat):
    hi, lo = _split_bf16(a)
    both = _dot(jnp.concatenate([hi, lo], axis=1), mat)
    return both[:, :LANES], both[:, LANES:]


LOG2_E = 1.4426950408889634


def _log2_1m_beta(z2):
    return -(jnp.maximum(z2, 0.0) + jnp.log2(1.0 + jnp.exp2(-jnp.abs(z2))))


def _attn_fwd(name, qkv3):
    _, S, D = qkv3.shape
    H = D // HEAD_DIM
    tq = _tile(S, 512)
    nch = tq // LANES
    scale = HEAD_DIM ** -0.5

    def body(q_ref, k_ref, v_ref, o_ref, t_ref):
        i = pl.program_id(1)
        q = q_ref[...]
        jj = lax.broadcasted_iota(jnp.int32, (LANES, LANES), 0)
        ss = lax.broadcasted_iota(jnp.int32, (LANES, LANES), 1)
        from_here = _cumsum_matrix(jj >= ss)
        causal = (lax.broadcasted_iota(jnp.int32, (tq, tq), 1) < lax.broadcasted_iota(jnp.int32, (tq, tq), 0))

        def group(k0, masked, tail, acc):
            kb = k_ref[pl.ds(k0, tq), :]
            vb = v_ref[pl.ds(k0, tq), :]
            z = _dot(q, kb, "nt") * (scale * LOG2_E)
            lm = _log2_1m_beta(z)
            if masked:
                lm = jnp.where(causal, lm, 0.0)
            args = [None] * nch
            for ch in reversed(range(nch)):
                sl = slice(ch * LANES, (ch + 1) * LANES)
                cum, tot = _chunk_sums(lm[:, sl], from_here)
                args[ch] = z[:, sl] + (cum + tail)
                tail = tail + tot
            a = jnp.exp2(jnp.concatenate(args, axis=1))
            if masked:
                a = jnp.where(causal, a, 0.0)
            return tail, acc + _dot(a.astype(BF16), vb)

        zero = jnp.zeros((tq, HEAD_DIM), F32)
        carry = group(pl.multiple_of(i * tq, tq), True, zero, zero)
        tail, acc = lax.fori_loop(
            0, i, lambda it, cr: group(pl.multiple_of((i - 1 - it) * tq, tq), False, *cr), carry)
        o_ref[...] = acc.astype(BF16)
        t_ref[...] = tail

    return _pcall(
        body, name=name, grid=(H, S // tq),
        in_specs=[pl.BlockSpec((None, tq, HEAD_DIM), lambda h, i: (0, i, h)),
                  pl.BlockSpec((None, S, HEAD_DIM), lambda h, i: (1, 0, h)),
                  pl.BlockSpec((None, S, HEAD_DIM), lambda h, i: (2, 0, h))],
        out_specs=[pl.BlockSpec((tq, HEAD_DIM), lambda h, i: (i, h)),
                   pl.BlockSpec((tq, HEAD_DIM), lambda h, i: (i, h))],
        out_shape=[_sds((S, D), BF16), _sds((S, D), F32)],
        compiler_params=_cp(("parallel", "parallel")),
    )(qkv3, qkv3, qkv3)


def _attn_bwd(name, qkv3, do, totals):
    _, S, D = qkv3.shape
    H = D // HEAD_DIM
    tq = _tile(S, 512)
    nch = tq // LANES
    nq = S // tq
    scale = HEAD_DIM ** -0.5

    def body(q_ref, k_ref, v_ref, do_ref, t_ref, out_ref, dk_acc, dv_acc):
        i = pl.program_id(1)

        @pl.when(i == 0)
        def _():
            dk_acc[...] = jnp.zeros_like(dk_acc)
            dv_acc[...] = jnp.zeros_like(dv_acc)

        q = q_ref[...]
        do_b = do_ref[...]
        total = t_ref[...]
        jj = lax.broadcasted_iota(jnp.int32, (LANES, LANES), 0)
        ss = lax.broadcasted_iota(jnp.int32, (LANES, LANES), 1)
        before = _cumsum_matrix(jj < ss)
        causal = (lax.broadcasted_iota(jnp.int32, (tq, tq), 1) < lax.broadcasted_iota(jnp.int32, (tq, tq), 0))

        def group(k0, masked, head, dsum, dq):
            kb = k_ref[pl.ds(k0, tq), :]
            vb = v_ref[pl.ds(k0, tq), :]
            z = _dot(q, kb, "nt") * (scale * LOG2_E)
            lm = _log2_1m_beta(z)
            sig = jnp.exp2(z + lm)
            if masked:
                lm = jnp.where(causal, lm, 0.0)
            args = []
            for ch in range(nch):
                sl = slice(ch * LANES, (ch + 1) * LANES)
                cum, tot = _chunk_sums(lm[:, sl], before)
                args.append(z[:, sl] + (total - head - cum))
                head = head + tot
            a = jnp.exp2(jnp.concatenate(args, axis=1))
            if masked:
                a = jnp.where(causal, a, 0.0)
            de = _dot(do_b, vb, "nt") * a
            dv_acc[pl.ds(k0, tq), :] += _dot(a.astype(BF16), do_b, "tn")
            sums = []
            for ch in range(nch):
                sl = slice(ch * LANES, (ch + 1) * LANES)
                cum, tot = _chunk_sums(de[:, sl], before)
                sums.append(dsum + cum)
                dsum = dsum + tot
            dz = de - sig * (de + jnp.concatenate(sums, axis=1))
            if masked:
                dz = jnp.where(causal, dz, 0.0)
            dzb = dz.astype(BF16)
            dk_acc[pl.ds(k0, tq), :] += _dot(dzb, q, "tn")
            return head, dsum, dq + _dot(dzb, kb)

        zero = jnp.zeros((tq, HEAD_DIM), F32)
        carry = lax.fori_loop(
            0, i, lambda g, cr: group(pl.multiple_of(g * tq, tq), False, *cr), (zero, zero, zero))
        q0 = pl.multiple_of(i * tq, tq)
        _, _, dq = group(q0, True, *carry)
        out_ref[0, pl.ds(q0, tq), :] = (dq * scale).astype(BF16)

        @pl.when(i == nq - 1)
        def _():
            out_ref[1] = (dk_acc[...] * scale).astype(BF16)
            out_ref[2] = dv_acc[...].astype(BF16)

    return _pcall(
        body, name=name, grid=(H, nq),
        in_specs=[pl.BlockSpec((None, tq, HEAD_DIM), lambda h, i: (0, i, h)),
                  pl.BlockSpec((None, S, HEAD_DIM), lambda h, i: (1, 0, h)),
                  pl.BlockSpec((None, S, HEAD_DIM), lambda h, i: (2, 0, h)),
                  pl.BlockSpec((tq, HEAD_DIM), lambda h, i: (i, h)),
                  pl.BlockSpec((tq, HEAD_DIM), lambda h, i: (i, h))],
        out_specs=pl.BlockSpec((3, S, HEAD_DIM), lambda h, i: (0, 0, h)),
        out_shape=_sds((3, S, D), BF16),
        scratch_shapes=[pltpu.VMEM((S, HEAD_DIM), F32), pltpu.VMEM((S, HEAD_DIM), F32)],
        compiler_params=_cp(("parallel", "arbitrary")),
    )(qkv3, qkv3, qkv3, do, totals)


def _conv_taps(g, t):
    S = g.shape[0]
    g1 = jnp.where(t >= 1, pltpu.roll(g, 1, 0), 0.0)
    g2 = jnp.where(t >= 2, pltpu.roll(g, 2, 0), 0.0)
    return g1, g2


def _conv_fwd(name, bcu3, cw, cb):
    _, S, D = bcu3.shape
    tf = LANES

    def body(x_ref, w_ref, b_ref, p_ref):
        t = lax.broadcasted_iota(jnp.int32, (S, tf), 0)
        g = x_ref[1] * x_ref[2]
        g1, g2 = _conv_taps(g, t)
        y = w_ref[0:1, :] * g2 + w_ref[1:2, :] * g1 + w_ref[2:3, :] * g + b_ref[...]
        p_ref[...] = (x_ref[0] * y).astype(BF16)

    return _pcall(
        body, name=name, grid=(D // tf,),
        in_specs=[pl.BlockSpec((3, S, tf), lambda f: (0, 0, f)), pl.BlockSpec((3, tf), lambda f: (0, f)),
                  pl.BlockSpec((1, tf), lambda f: (0, f))],
        out_specs=pl.BlockSpec((S, tf), lambda f: (0, f)), out_shape=_sds((S, D), BF16),
        compiler_params=_cp(("parallel",)),
    )(bcu3, cw, cb)


def _conv_bwd(name, bcu3, dp, cw, cb):
    _, S, D = bcu3.shape
    tf = LANES

    def body(x_ref, dp_ref, w_ref, b_ref, o_ref, vec_ref):
        t = lax.broadcasted_iota(jnp.int32, (S, tf), 0)
        bg, cg, u = x_ref[0], x_ref[1], x_ref[2]
        g = cg * u
        g1, g2 = _conv_taps(g, t)
        w0, w1, w2 = w_ref[0:1, :], w_ref[1:2, :], w_ref[2:3, :]
        y = w0 * g2 + w1 * g1 + w2 * g + b_ref[...]
        dpv = dp_ref[...]
        o_ref[0] = (dpv * y).astype(BF16)
        dy = dpv * bg

        def colsum(a):
            return jnp.sum(a, axis=0, keepdims=True)

        vec_ref[...] = jnp.zeros_like(vec_ref)
        for j, row_j in enumerate((dy, dy * g2, dy * g1, dy * g)):
            vec_ref[j:j + 1, :] = colsum(row_j)
        dy1 = jnp.where(t < S - 1, pltpu.roll(dy, S - 1, 0), 0.0)
        dy2 = jnp.where(t < S - 2, pltpu.roll(dy, S - 2, 0), 0.0)
        dg = w2 * dy + w1 * dy1 + w0 * dy2
        o_ref[1] = (dg * u).astype(BF16)
        o_ref[2] = (dg * cg).astype(BF16)

    return _pcall(
        body, name=name, grid=(D // tf,),
        in_specs=[pl.BlockSpec((3, S, tf), lambda f: (0, 0, f)), pl.BlockSpec((S, tf), lambda f: (0, f)),
                  pl.BlockSpec((3, tf), lambda f: (0, f)), pl.BlockSpec((1, tf), lambda f: (0, f))],
        out_specs=[pl.BlockSpec((3, S, tf), lambda f: (0, 0, f)), pl.BlockSpec((8, tf), lambda f: (0, f))],
        out_shape=[_sds((3, S, D), BF16), _sds((8, D), F32)],
        compiler_params=_cp(("parallel",)),
    )(bcu3, dp, cw, cb)


def _adamw(name, w3, m3, v3, l, parts, carry=None):
    L, R, C = w3.shape
    budget_rows = (1 << 18) // C
    tr = R if R <= budget_rows else _tile(R, 1 << (budget_rows.bit_length() - 1))
    n_parts = len(parts)
    n_carry = 0 if carry is None else 4
    c1 = 1.0 / (1.0 - ADAM_B1 ** ADAM_STEP)
    c2 = 1.0 / (1.0 - ADAM_B2 ** ADAM_STEP)

    def body(*refs):
        w_ref, m_ref, v_ref = refs[:3]
        p_refs = refs[3:3 + n_parts]
        og, od, om, ov = refs[3 + n_parts + n_carry:]
        g = p_refs[0][...].astype(F32)
        for p in p_refs[1:]:
            g = g + p[...].astype(F32)
        m = ADAM_B1 * m_ref[...] + (1.0 - ADAM_B1) * g
        v = ADAM_B2 * v_ref[...] + (1.0 - ADAM_B2) * jnp.square(g)
        og[...] = g
        om[...] = m
        ov[...] = v
        od[...] = -ADAM_LR * ((m * c1) / (jnp.sqrt(v * c2) + ADAM_EPS) + ADAM_WD * w_ref[...])

    lay = pl.BlockSpec((None, tr, C), lambda r: (l, r, 0))
    in_specs = [lay, lay, lay]
    args = [w3, m3, v3]
    for arr, idx in parts:
        in_specs.append(pl.BlockSpec((None, tr, C), functools.partial(lambda r, j: (j, r, 0), j=idx)))
        args.append(arr)
    aliases = {}
    if carry is not None:
        for j, buf in enumerate(carry):
            aliases[len(args)] = j
            in_specs.append(ANY_SPEC)
            args.append(buf)
    return _pcall(body, name=name, grid=(R // tr,), in_specs=in_specs, out_specs=[lay] * 4,
                  out_shape=[_sds((L, R, C), F32)] * 4, input_output_aliases=aliases,
                  compiler_params=_cp(("parallel",)))(*args)


def kernel(x, c, mod_w, mod_b, ln_g, ln_b, attn_w_qkv, attn_w_o, conv_w_in, conv_w, conv_b, conv_w_out, mlp_w1, mlp_b1, mlp_w2, mlp_b2, loss_target, m_mod_w, m_mod_b, m_ln_g, m_ln_b, m_attn_w_qkv, m_attn_w_o, m_conv_w_in, m_conv_w, m_conv_b, m_conv_w_out, m_mlp_w1, m_mlp_b1, m_mlp_w2, m_mlp_b2, v_mod_w, v_mod_b, v_ln_g, v_ln_b, v_attn_w_qkv, v_attn_w_o, v_conv_w_in, v_conv_w, v_conv_b, v_conv_w_out, v_mlp_w1, v_mlp_b1, v_mlp_w2, v_mlp_b2):
    xs, target = x[0], loss_target[0]
    S, D = xs.shape
    L = mod_w.shape[0]
    LA, LC = attn_w_qkv.shape[0], conv_w_in.shape[0]
    F = mlp_b1.shape[1]
    DB, FB, NC = D // N_DEV, F // N_DEV, mod_w.shape[2]
    CT = DB
    me = 4 * lax.axis_index("x") + 2 * lax.axis_index("y") + lax.axis_index("c")
    me_arr = jnp.reshape(me, (1,)).astype(jnp.int32)

    K = 2 * L

    def mixer_weights(l):
        if l % 2 == 0:
            return attn_w_qkv, attn_w_o, l // 2
        return conv_w_in, conv_w_out, l // 2

    def group_shards(k, deps):
        l = k // 2
        if k % 2 == 0:
            w_in3, w_out3, j = mixer_weights(l)
            return [_cast_bf16(w_in3, j, me_arr, f"cast_win_{l}", deps), _cast_bf16(w_out3, j, me_arr, f"cast_wout_{l}", deps)]
        return [_cast_bf16(mlp_w1, l, me_arr, f"cast_w1_{l}", deps), _cast_bf16(mlp_w2, l, me_arr, f"cast_w2_{l}", deps)]

    first_gather = _ag_start(group_shards(0, []), [], "ag_start_0")
    shards = {k: group_shards(k, [first_gather[3]]) for k in range(1, K)}

    small = [c, ln_g, ln_b, conv_w, conv_b]
    pack = jnp.concatenate([a.reshape(-1) for a in small]).reshape(-1, LANES)
    gath = _allgather([pack], "ag_small_params", deps=[first_gather[3]])[0].reshape(N_DEV, -1)
    offs, pos = [], 0
    for a in small:
        offs.append((pos, a.size))
        pos += a.size

    def unshard(idx, lead):
        o, n = offs[idx]
        a = gath[:, o:o + n].reshape((N_DEV,) + lead + (DB,))
        return jnp.moveaxis(a, 0, -2).reshape(lead + (D,))

    c_all = gath[:, :D]
    lng_full, lnb_full = unshard(1, (L, 2)), unshard(2, (L, 2))
    cw_full, cb_full = unshard(3, (LC, 3)), unshard(4, (LC,))

    cond16 = _silu_bf16(jnp.pad(c_all, ((0, 16 - N_DEV), (0, 0))), "silu_cond")
    tn_mod = NC // 3
    bias_mod = lax.dynamic_slice_in_dim(mod_b, me * NC, NC, axis=1).reshape(L, 1, NC)
    mod_part = _matmul(
        "mod_fwd", "nn", (L, NC // tn_mod), 1,
        cond16, pl.BlockSpec((16, D), lambda l, n: (0, 0)),
        mod_w, pl.BlockSpec((None, D, tn_mod), lambda l, n: (l, 0, n)),
        [_sds((L, 16, NC), F32)], [pl.BlockSpec((None, 16, tn_mod), lambda l, n: (l, 0, n))], None,
        extras=[bias_mod], extra_specs=[pl.BlockSpec((None, 1, tn_mod), lambda l, n: (l, 0, n))])[0]
    mod_gath = _allgather([mod_part[:, :N_DEV].reshape(-1, LANES)], "ag_mod")[0].reshape(N_DEV, L, N_DEV, NC)
    mod_me = lax.dynamic_index_in_dim(mod_gath, me, axis=2, keepdims=False)
    mod_me = jnp.moveaxis(mod_me, 0, 1).reshape(L, 6, 1, D)

    def mod_vecs(k):
        l, o = k // 2, 3 * (k % 2)
        return mod_me[l, o], mod_me[l, o + 1], mod_me[l, o + 2]

    wg = [[None] * 4 for _ in range(L)]
    in_flight = {0: first_gather, 1: _ag_start(shards.pop(1), [mod_gath], "ag_start_1")}

    def fetch_group(k, after):
        first, second = _ag_finish(_ag_wait(in_flight.pop(k), after, f"ag_wait_{k}"), f"ag_finish_{k}")
        if k + 2 < K:
            in_flight[k + 2] = _ag_start(shards.pop(k + 2), [first], f"ag_start_{k + 2}")
        rows = D if k % 2 == 0 else F
        wg[k // 2][2 * (k % 2)], wg[k // 2][2 * (k % 2) + 1] = first, second.reshape(rows, D)

    n_ct = 3 * D // CT
    tm_big = _tile(S, 2048)

    def in_proj(l, h, out_dtype, deps):
        return _matmul(
            f"in_proj_{l}", "nn", (S // tm_big, n_ct), 1,
            h, pl.BlockSpec((tm_big, D), lambda i, ct: (i, 0)),
            wg[l][0], pl.BlockSpec((None, D, CT), lambda i, ct: (ct // 3, 0, ct % 3)),
            [_sds((3, S, D), out_dtype)],
            [pl.BlockSpec((None, tm_big, CT), lambda i, ct: (ct // N_DEV, i, ct % N_DEV))], None, deps=deps)[0]

    tm, tn_half = _tile(S, 512), _tile(D, 1024)
    tm_up = _tile(S, 1024)

    def out_proj(l, a):
        return _matmul(
            f"out_proj_{l}", "nn", (D // tn_half, S // tm), 1,
            a, pl.BlockSpec((tm, D), lambda n, i: (i, 0)),
            wg[l][1], pl.BlockSpec((D, tn_half), lambda n, i: (0, n)),
            [_sds((S, D), F32)], [pl.BlockSpec((tm, tn_half), lambda n, i: (i, n))], None)[0]

    def mlp_up(l, h, deps):
        blk = pl.BlockSpec((tm_up, FB), lambda j, i: (i, j))
        return _matmul(
            f"mlp_up_{l}", "nn", (N_DEV, S // tm_up), 1,
            h, pl.BlockSpec((tm_up, D), lambda j, i: (i, 0)),
            wg[l][2], pl.BlockSpec((None, D, FB), lambda j, i: (j, 0, 0)),
            [_sds((S, F), BF16), _sds((S, F), BF16)], [blk, blk], None, epilogue="relu2",
            extras=[mlp_b1[l].reshape(1, F)], extra_specs=[pl.BlockSpec((1, FB), lambda j, i: (0, j))], deps=deps)

    tn_q = _tile(D, 512)

    def mlp_down(l, act):
        return _matmul(
            f"mlp_down_{l}", "nn", (D // tn_q, S // tm), 1,
            act, pl.BlockSpec((tm, F), lambda n, i: (i, 0)),
            wg[l][3], pl.BlockSpec((F, tn_q), lambda n, i: (0, n)),
            [_sds((S, D), F32)], [pl.BlockSpec((tm, tn_q), lambda n, i: (i, n))], None,
            extras=[mlp_b2[l].reshape(1, D)], extra_specs=[pl.BlockSpec((1, tn_q), lambda n, i: (0, n))])[0]

    xin, ys, hs, saved = [xs], [], [], []
    sh0, sc0, _ = mod_vecs(0)
    cast_done = [in_flight[1][3]] + [buf for k in sorted(shards) for buf in shards[k]]
    h = _modulate("modulate_in", xs, sc0, sh0, cast_done)
    for k in range(K):
        l = k // 2
        hs.append(h)
        fetch_group(k, h)
        started = [in_flight[k + 2][3]] if k + 2 < K else []
        if k % 2 == 0 and l % 2 == 0:
            qkv3 = in_proj(l, h, BF16, started)
            o, totals = _attn_fwd(f"attn_fwd_{l}", qkv3)
            y = out_proj(l, o)
            saved.append((qkv3, o, totals))
        elif k % 2 == 0:
            bcu3 = in_proj(l, h, F32, started)
            j = l // 2
            p = _conv_fwd(f"conv_fwd_{l}", bcu3, cw_full[j], cb_full[j].reshape(1, D))
            y = out_proj(l, p)
            saved.append((bcu3, p))
        else:
            u, act = mlp_up(l, h, started)
            y = mlp_down(l, act)
            saved.append((u, act))
        ys.append(y)
        if k + 1 < K:
            _, _, gate = mod_vecs(k)
            sh_n, sc_n, _ = mod_vecs(k + 1)
            x_next, h = _ln_fwd(f"ln_fwd_{k}", xin[k], y, gate, lng_full[l, k % 2].reshape(1, D),
                                lnb_full[l, k % 2].reshape(1, D), sc_n, sh_n)
            xin.append(x_next)

    tm_g = _tile(D, 512)

    def grad_rows(name, a, dy, rows):
        return _matmul(
            name, "tn", (rows // tm_g, D // tn_half), 1,
            a, pl.BlockSpec((S, tm_g), lambda m, n: (0, m)),
            dy, pl.BlockSpec((S, tn_half), lambda m, n: (0, n)),
            [_sds((rows, D), BF16)], [pl.BlockSpec((tm_g, tn_half), lambda m, n: (m, n))], None)[0]

    def back_out_proj(l, dy, out_dtype):
        return _matmul(
            f"back_out_proj_{l}", "nt", (D // tn_half, S // tm), 1,
            dy, pl.BlockSpec((tm, D), lambda n, i: (i, 0)),
            wg[l][1], pl.BlockSpec((tn_half, D), lambda n, i: (n, 0)),
            [_sds((S, D), out_dtype)], [pl.BlockSpec((tm, tn_half), lambda n, i: (i, n))], None)[0]

    def back_in_proj(l, d3, deps):
        def part(p):
            return pl.BlockSpec((None, tm, D), lambda i, n: (p, i, 0))
        return _matmul(
            f"back_in_proj_{l}", "nt", (S // tm, D // tn_q), 1,
            d3, part(0),
            _unblock(wg[l][0], f"unblock_win_{l}"), pl.BlockSpec((tn_q, 3 * D), lambda i, n: (n, 0)),
            [_sds((S, D), F32)], [pl.BlockSpec((tm, tn_q), lambda i, n: (i, n))], None,
            extras=[d3, d3], extra_specs=[part(1), part(2)], widen="a", deps=deps)[0]

    def grad_in_proj(l, h, d3):
        def tile(r):
            return pl.BlockSpec((None, S, CT), lambda j, m: ((3 * j + r) // N_DEV, 0, (3 * j + r) % N_DEV))
        return _matmul(
            f"grad_in_proj_{l}", "tn", (N_DEV, D // tm_g), 1,
            h, pl.BlockSpec((S, tm_g), lambda j, m: (0, m)),
            d3, tile(0),
            [_sds((N_DEV, D, 3 * CT), BF16)],
            [pl.BlockSpec((None, tm_g, 3 * CT), lambda j, m: (j, m, 0))], None,
            extras=[d3, d3], extra_specs=[tile(1), tile(2)], widen="b")[0]

    def back_mlp_down(l, dy, u):
        blk = pl.BlockSpec((tm_up, FB), lambda j, i: (i, j))
        return _matmul(
            f"back_mlp_down_{l}", "nt", (N_DEV, S // tm_up), 1,
            dy, pl.BlockSpec((tm_up, D), lambda j, i: (i, 0)),
            wg[l][3], pl.BlockSpec((FB, D), lambda j, i: (j, 0)),
            [_sds((S, F), BF16), _sds((1, F), F32)], [blk, pl.BlockSpec((1, FB), lambda j, i: (0, j))], None,
            epilogue="drelu2", extras=[u], extra_specs=[blk], sem=("parallel", "arbitrary"))

    def back_mlp_up(l, du, deps):
        return _matmul(
            f"back_mlp_up_{l}", "nt", (S // tm, D // tn_q), 1,
            du, pl.BlockSpec((tm, F), lambda i, n: (i, 0)),
            _unblock(wg[l][2], f"unblock_w1_{l}"), pl.BlockSpec((tn_q, F), lambda i, n: (n, 0)),
            [_sds((S, D), F32)], [pl.BlockSpec((tm, tn_q), lambda i, n: (i, n))], None, deps=deps)[0]

    def grad_mlp_up(l, h, du):
        return _matmul(
            f"grad_mlp_up_{l}", "tn", (N_DEV, D // tm_g), 1,
            h, pl.BlockSpec((S, tm_g), lambda j, m: (0, m)),
            du, pl.BlockSpec((S, FB), lambda j, m: (0, j)),
            [_sds((N_DEV, D, FB), BF16)], [pl.BlockSpec((None, tm_g, FB), lambda j, m: (j, m, 0))], None)[0]

    dmod = [[None] * 6 for _ in range(L)]
    dlng = [[None, None] for _ in range(L)]
    dlnb = [[None, None] for _ in range(L)]
    db1, db2 = [None] * L, [None] * L
    dcw, dcb = [None] * LC, [None] * LC

    carries = {"attn_in": None, "attn_out": None, "conv_in": None, "conv_out": None, "w1": None, "w2": None}
    moments = {"attn_in": (attn_w_qkv, m_attn_w_qkv, v_attn_w_qkv), "attn_out": (attn_w_o, m_attn_w_o, v_attn_w_o),
               "conv_in": (conv_w_in, m_conv_w_in, v_conv_w_in), "conv_out": (conv_w_out, m_conv_w_out, v_conv_w_out),
               "w1": (mlp_w1, m_mlp_w1, v_mlp_w1), "w2": (mlp_w2, m_mlp_w2, v_mlp_w2)}
    pending = []

    def finish_pending(after):
        if not pending:
            return
        k, started = pending.pop()
        ps, r2 = _rs_chips_wait(started, after, f"rs_wait_{k}")
        l = k // 2
        if k % 2 == 1:
            keys = [("w1", l), ("w2", l)]
        else:
            kind = "attn" if l % 2 == 0 else "conv"
            keys = [(kind + "_in", l // 2), (kind + "_out", l // 2)]
        for t, (key, idx) in enumerate(keys):
            w3, m3, v3 = moments[key]
            parts = [(ps[t], 0), (r2[t], 1), (r2[t], 2), (r2[t], 3)]
            carries[key] = _adamw(f"adamw_{key}_{l}", w3, m3, v3, idx, parts, carries[key])

    def reduce_begin(k, gs):
        return _rs_sibling_start(gs, f"rs_sibling_start_{k}")

    def reduce_send(k, exchange, after):
        finish_pending(after)
        gs, r1 = _rs_sibling_wait(exchange, after, f"rs_sibling_wait_{k}")
        ps = [_rs_add(g, r, me_arr, f"rs_add_{k}_{t}") for t, (g, r) in enumerate(zip(gs, r1))]
        started = _rs_chips_start(ps, f"rs_start_{k}")
        pending.append((k, started))
        return started[4]

    def take_ln_rows(k, vec):
        l, o = k // 2, 3 * (k % 2)
        dlng[l][k % 2], dlnb[l][k % 2] = vec[2], vec[3]
        dmod[l][o + 2] = vec[4]
        if k % 2 == 1:
            db2[l] = vec[5]

    def take_mod_rows(k, vec):
        l, o = k // 2, 3 * (k % 2)
        dmod[l][o + 1], dmod[l][o] = vec[0], vec[1]

    lk = K - 1
    _, _, gate = mod_vecs(lk)
    dy, dxr, vec = _boundary(f"boundary_{lk}", "last", (target, xin[lk], ys[lk]),
                             (gate, lng_full[lk // 2, 1].reshape(1, D), lnb_full[lk // 2, 1].reshape(1, D)))
    take_ln_rows(lk, vec)
    loss_local = 0.5 * vec[6, 0] / D
    grad_x = None
    for k in range(K - 1, -1, -1):
        l = k // 2
        if k % 2 == 1:
            u, act = saved[k]
            g_w2 = grad_rows(f"grad_mlp_down_{l}", act, dy, F).reshape(N_DEV, FB, D)
            du, db1_l = back_mlp_down(l, dy, u)
            db1[l] = db1_l[0]
            g_w1 = grad_mlp_up(l, hs[k], du)
            exchange = reduce_begin(k, [g_w1, g_w2])
            dh = back_mlp_up(l, du, [exchange[4]])
            token = reduce_send(k, exchange, dh)
        else:
            if l % 2 == 0:
                qkv3, o, totals = saved[k]
                g_out = grad_rows(f"grad_out_proj_{l}", o, dy, D).reshape(N_DEV, DB, D)
                do = back_out_proj(l, dy, BF16)
                d3 = _attn_bwd(f"attn_bwd_{l}", qkv3, do, totals)
            else:
                bcu3, p = saved[k]
                j = l // 2
                g_out = grad_rows(f"grad_out_proj_{l}", p, dy, D).reshape(N_DEV, DB, D)
                dp = back_out_proj(l, dy, F32)
                d3, cvec = _conv_bwd(f"conv_bwd_{l}", bcu3, dp, cw_full[j], cb_full[j].reshape(1, D))
                dcb[j], dcw[j] = cvec[0], cvec[1:4]
            g_in = grad_in_proj(l, hs[k], d3)
            exchange = reduce_begin(k, [g_in, g_out])
            if k == 0:
                token = reduce_send(k, exchange, d3)
                dh = back_in_proj(l, d3, [token])
            else:
                dh = back_in_proj(l, d3, [exchange[4]])
                token = reduce_send(k, exchange, dh)
        sh_k, sc_k, _ = mod_vecs(k)
        if k > 0:
            kp = k - 1
            _, _, gate = mod_vecs(kp)
            dy, dxr, vec = _boundary(
                f"boundary_{kp}", "mid", (dxr, dh, xin[kp], ys[kp]),
                (sc_k, gate, lng_full[kp // 2, kp % 2].reshape(1, D), lnb_full[kp // 2, kp % 2].reshape(1, D)),
                deps=[token])
            take_mod_rows(k, vec)
            take_ln_rows(kp, vec)
        else:
            grad_x, vec = _boundary("boundary_in", "first", (dxr, dh, xs), (sc_k,), deps=[token])
            take_mod_rows(k, vec)
    finish_pending(grad_x)
    results = carries

    flat = [jnp.stack([jnp.stack(r) for r in dmod]).reshape(-1), jnp.stack(db1).reshape(-1),
            jnp.stack(db2).reshape(-1), jnp.stack([jnp.stack(r) for r in dlng]).reshape(-1),
            jnp.stack([jnp.stack(r) for r in dlnb]).reshape(-1), jnp.stack(dcw).reshape(-1),
            jnp.stack(dcb).reshape(-1)]
    sizes = [a.size for a in flat]
    small_g = _allgather([jnp.concatenate(flat).reshape(-1, LANES)], "ag_small_grads")[0]
    total = _sum_devices(small_g, "sum_small_grads").reshape(-1)
    starts = [sum(sizes[:i]) for i in range(len(sizes))]

    def piece(i, shape):
        return total[starts[i]:starts[i] + sizes[i]].reshape(shape)

    def my_block(a):
        return lax.dynamic_slice_in_dim(a, me * DB, DB, axis=a.ndim - 1)

    g_mod_b = piece(0, (L, 6 * D))
    g_b1, g_b2 = piece(1, (L, F)), piece(2, (L, D))
    g_lng, g_lnb = my_block(piece(3, (L, 2, D))), my_block(piece(4, (L, 2, D)))
    g_cw, g_cb = my_block(piece(5, (LC, 3, D))), my_block(piece(6, (LC, D)))

    dmod_all = small_g.reshape(N_DEV, -1)[:, :sizes[0]].reshape(N_DEV, L, 6 * D)
    dmod_mine = jnp.moveaxis(lax.dynamic_slice_in_dim(dmod_all, me * NC, NC, axis=2), 0, 1)
    dmod16 = jnp.pad(dmod_mine, ((0, 0), (0, 16 - N_DEV), (0, 0)))
    tm_mod = _tile(D, 512)
    g_mod_w = _matmul(
        "grad_mod_w", "tn", (L, D // tm_mod), 1,
        cond16, pl.BlockSpec((16, tm_mod), lambda l, m: (0, m)),
        dmod16, pl.BlockSpec((None, 16, NC), lambda l, m: (l, 0, 0)),
        [_sds((L, D, NC), F32)], [pl.BlockSpec((None, tm_mod, NC), lambda l, m: (l, m, 0))], None)[0]
    res_mod_w = None
    for l in range(L):
        res_mod_w = _adamw(f"adamw_mod_w_{l}", mod_w, m_mod_w, v_mod_w, l, [(g_mod_w, l)], res_mod_w)

    small_w = [mod_b, ln_g, ln_b, conv_w, conv_b, mlp_b1, mlp_b2]
    small_m = [m_mod_b, m_ln_g, m_ln_b, m_conv_w, m_conv_b, m_mlp_b1, m_mlp_b2]
    small_v = [v_mod_b, v_ln_g, v_ln_b, v_conv_w, v_conv_b, v_mlp_b1, v_mlp_b2]
    small_gr = [g_mod_b, g_lng, g_lnb, g_cw, g_cb, g_b1, g_b2]
    n_small = sum(a.size for a in small_w)
    pad = (-n_small) % (8 * LANES)

    def packed(arrs):
        v = jnp.concatenate([a.reshape(-1) for a in arrs])
        return jnp.pad(v, (0, pad)).reshape(1, -1, LANES)

    sg, sd, sm, sv = _adamw("adamw_small", packed(small_w), packed(small_m), packed(small_v), 0,
                            [(packed(small_gr), 0)])

    def unpacked(buf):
        out, pos = [], 0
        flat_buf = buf.reshape(-1)
        for a in small_w:
            out.append(flat_buf[pos:pos + a.size].reshape(a.shape))
            pos += a.size
        return out

    small_res = [unpacked(b) for b in (sg, sd, sm, sv)]
    loss = lax.psum(loss_local, ("x", "y", "c"))

    def leaf(kind):
        s = small_res[kind]
        return [res_mod_w[kind], s[0], s[1], s[2], results["attn_in"][kind], results["attn_out"][kind],
                results["conv_in"][kind], s[3], s[4], results["conv_out"][kind], results["w1"][kind], s[5],
                results["w2"][kind], s[6]]

    return (loss, grad_x[None], *leaf(0), *leaf(1), *leaf(2), *leaf(3))
```

```python
import functools

import jax
import jax.numpy as jnp
from jax import lax
from jax.experimental import pallas as pl
from jax.experimental.pallas import tpu as pltpu

F32 = jnp.float32
BF16 = jnp.bfloat16
N_DEV = 8
HEAD_DIM = 128
LANES = 128
DEEPNORM_ALPHA = 8.0 ** 0.25
LN_EPS = 1e-5
ADAM_LR = 0.001
ADAM_B1 = 0.9
ADAM_B2 = 0.999
ADAM_EPS = 1e-08
ADAM_WD = 0.01
ADAM_STEP = 10
VMEM_LIMIT_BYTES = 52 * 1024 * 1024
MESH = pl.DeviceIdType.MESH
HBM_SPEC = pl.BlockSpec(memory_space=pltpu.HBM)
ANY_SPEC = pl.BlockSpec(memory_space=pl.ANY)
SEM_SPEC = pl.BlockSpec(memory_space=pltpu.SEMAPHORE)
EFFECT = pltpu.SideEffectType.DATAFLOW_SIDE_EFFECTING


def _pcall(body, **kw):
    return pl.pallas_call(body, **kw)


def _cp(sem=None):
    return pltpu.CompilerParams(dimension_semantics=sem, vmem_limit_bytes=VMEM_LIMIT_BYTES)


def _tile(n, pref):
    t = min(n, pref)
    assert n % t == 0, (n, pref)
    return t


def _sds(shape, dtype):
    return jax.ShapeDtypeStruct(tuple(shape), dtype)


def _allgather(xs, name, deps=()):
    n, nd = len(xs), len(deps)

    def body(*refs):
        x_refs = refs[:n]
        out_refs = refs[n + nd:2 * n + nd]
        send_sems, recv_sems, local_sems = refs[2 * n + nd:]
        x, y, c = lax.axis_index("x"), lax.axis_index("y"), lax.axis_index("c")
        me, sibling = (x, y, c), (x, y, 1 - c)
        chips = [(1 - x, y), (x, 1 - y), (1 - x, 1 - y)]

        def slot(t, px, py, pc):
            return out_refs[t].at[4 * px + 2 * py + pc]

        def copy(t, k, block, to, src=None):
            return pltpu.make_async_remote_copy(
                src_ref=slot(t, *block) if src is None else src, dst_ref=slot(t, *block),
                send_sem=send_sems.at[t, k], recv_sem=recv_sems.at[t, k],
                device_id=to, device_id_type=MESH)

        mine = [pltpu.make_async_copy(x_refs[t], slot(t, *me), local_sems.at[t]) for t in range(n)]
        for cp in mine:
            cp.start()
        first = []
        for t in range(n):
            first.append(copy(t, 0, me, sibling, src=x_refs[t]))
            for j, chip in enumerate(chips):
                first.append(copy(t, 1 + j, me, (*chip, c), src=x_refs[t]))
        for cp in first:
            cp.start()
        passed = []
        for j, chip in enumerate(chips):
            for t in range(n):
                copy(t, 1 + j, (*chip, c), me).wait_recv()
                fwd = copy(t, 4 + j, (*chip, c), sibling)
                fwd.start()
                passed.append(fwd)
        for t in range(n):
            copy(t, 0, sibling, me).wait_recv()
            for j, chip in enumerate(chips):
                copy(t, 4 + j, (*chip, 1 - c), me).wait_recv()
        for cp in first + passed:
            cp.wait_send()
        for cp in mine:
            cp.wait()

    outs = _pcall(
        body, name=name,
        out_shape=[_sds((N_DEV,) + a.shape, a.dtype) for a in xs],
        in_specs=[HBM_SPEC] * n + [ANY_SPEC] * nd, out_specs=[HBM_SPEC] * n,
        scratch_shapes=[pltpu.SemaphoreType.DMA((n, 7)), pltpu.SemaphoreType.DMA((n, 7)),
                        pltpu.SemaphoreType.DMA((n,))],
    )(*xs, *deps)
    return list(outs)


def _peers(x, y, c):
    return [(x, y, 1 - c), (1 - x, y, c), (x, 1 - y, c), (1 - x, 1 - y, c)]


def _block_owner(x, y, c, k):
    return ((1 - x) if k >> 1 else x, (1 - y) if k & 1 else y, c)


def _hbm(a):
    return pltpu.with_memory_space_constraint(a, pltpu.HBM)


def _ag_start(lands, deps, name):
    n, nd = len(lands), len(deps)

    def body(*refs):
        land_refs = refs[:n]
        send_sems, recv_sems = refs[n + nd], refs[n + nd + 1]
        token = refs[-1]
        x, y, c = lax.axis_index("x"), lax.axis_index("y"), lax.axis_index("c")
        for t in range(n):
            mine = land_refs[t].at[4 * x + 2 * y + c]
            for k, to in enumerate(_peers(x, y, c)):
                pltpu.make_async_remote_copy(
                    src_ref=mine, dst_ref=mine,
                    send_sem=send_sems.at[4 * t + k], recv_sem=recv_sems.at[4 * t + k],
                    device_id=to, device_id_type=MESH).start()
        token[...] = jnp.zeros_like(token)

    outs = _pcall(
        body, name=name,
        out_shape=(pltpu.SemaphoreType.DMA((4 * n,)), pltpu.SemaphoreType.DMA((4 * n,)),
                   *[pltpu.HBM(a.shape, a.dtype) for a in lands], _sds((8, LANES), F32)),
        in_specs=[HBM_SPEC] * n + [ANY_SPEC] * nd,
        out_specs=(SEM_SPEC, SEM_SPEC, *[HBM_SPEC] * n, pl.BlockSpec(memory_space=pltpu.VMEM)),
        input_output_aliases={i: 2 + i for i in range(n)},
        compiler_params=pltpu.CompilerParams(has_side_effects=EFFECT),
    )(*[_hbm(a) for a in lands], *deps)
    return outs[0], outs[1], list(outs[2:2 + n]), outs[-1]


def _ag_wait(started, after, name):
    send_sems, recv_sems, lands, _ = started
    n = len(lands)

    def body(*refs):
        land_refs = refs[:n]
        send_sems, recv_sems = refs[n], refs[n + 1]
        x, y, c = lax.axis_index("x"), lax.axis_index("y"), lax.axis_index("c")
        for t in range(n):
            for k, (px, py, pc) in enumerate(_peers(x, y, c)):
                cp = pltpu.make_async_remote_copy(
                    src_ref=land_refs[t].at[4 * x + 2 * y + c], dst_ref=land_refs[t].at[4 * px + 2 * py + pc],
                    send_sem=send_sems.at[4 * t + k], recv_sem=recv_sems.at[4 * t + k],
                    device_id=(px, py, pc), device_id_type=MESH)
                cp.wait_send()
                cp.wait_recv()

    outs = _pcall(
        body, name=name,
        out_shape=tuple(pltpu.HBM(a.shape, a.dtype) for a in lands),
        in_specs=[HBM_SPEC] * n + [SEM_SPEC, SEM_SPEC, ANY_SPEC],
        out_specs=[HBM_SPEC] * n,
        input_output_aliases={i: i for i in range(n)},
        compiler_params=pltpu.CompilerParams(has_side_effects=EFFECT),
    )(*lands, send_sems, recv_sems, after)
    return list(outs)


def _ag_finish(lands, name):
    n = len(lands)

    def body(*refs):
        out_refs = refs[n:2 * n]
        send_sems, recv_sems = refs[2 * n:]
        x, y, c = lax.axis_index("x"), lax.axis_index("y"), lax.axis_index("c")
        chips = [(1 - x, y), (x, 1 - y), (1 - x, 1 - y)]

        def passing(t, j, pc):
            px, py = chips[j]
            rows = out_refs[t].at[4 * px + 2 * py + pc]
            return pltpu.make_async_remote_copy(
                src_ref=rows, dst_ref=rows, send_sem=send_sems.at[t, j], recv_sem=recv_sems.at[t, j],
                device_id=(x, y, 1 - c), device_id_type=MESH)

        sends = [passing(t, j, c) for t in range(n) for j in range(3)]
        for cp in sends:
            cp.start()
        for t in range(n):
            for j in range(3):
                passing(t, j, 1 - c).wait_recv()
        for cp in sends:
            cp.wait_send()

    outs = _pcall(
        body, name=name,
        out_shape=[_sds(a.shape, a.dtype) for a in lands],
        in_specs=[HBM_SPEC] * n, out_specs=[HBM_SPEC] * n,
        input_output_aliases={i: i for i in range(n)},
        scratch_shapes=[pltpu.SemaphoreType.DMA((n, 3)), pltpu.SemaphoreType.DMA((n, 3))],
    )(*lands)
    return list(outs)


def _rs_chips_start(ps, name):
    n = len(ps)

    def body(*refs):
        p_refs, land_refs = refs[:n], refs[n:2 * n]
        send_sems, recv_sems = refs[2 * n], refs[2 * n + 1]
        token = refs[-1]
        x, y, c = lax.axis_index("x"), lax.axis_index("y"), lax.axis_index("c")
        for t in range(n):
            for k in range(1, 4):
                pltpu.make_async_remote_copy(
                    src_ref=p_refs[t].at[k], dst_ref=land_refs[t].at[k],
                    send_sem=send_sems.at[3 * t + k - 1], recv_sem=recv_sems.at[3 * t + k - 1],
                    device_id=_block_owner(x, y, c, k), device_id_type=MESH).start()
        token[...] = jnp.zeros_like(token)

    lands = [lax.empty(p.shape, p.dtype) for p in ps]
    outs = _pcall(
        body, name=name,
        out_shape=(pltpu.SemaphoreType.DMA((3 * n,)), pltpu.SemaphoreType.DMA((3 * n,)),
                   *[pltpu.HBM(a.shape, a.dtype) for a in ps], *[pltpu.HBM(a.shape, a.dtype) for a in lands],
                   _sds((8, LANES), F32)),
        in_specs=[HBM_SPEC] * (2 * n),
        out_specs=(SEM_SPEC, SEM_SPEC, *[HBM_SPEC] * (2 * n), pl.BlockSpec(memory_space=pltpu.VMEM)),
        input_output_aliases={i: 2 + i for i in range(2 * n)},
        compiler_params=pltpu.CompilerParams(has_side_effects=EFFECT),
    )(*[_hbm(a) for a in ps], *[_hbm(a) for a in lands])
    return outs[0], outs[1], list(outs[2:2 + n]), list(outs[2 + n:2 + 2 * n]), outs[-1]


def _rs_chips_wait(started, after, name):
    send_sems, recv_sems, ps, lands, _ = started
    n = len(ps)

    def body(*refs):
        p_refs, land_refs = refs[:n], refs[n:2 * n]
        send_sems, recv_sems = refs[2 * n], refs[2 * n + 1]
        x, y, c = lax.axis_index("x"), lax.axis_index("y"), lax.axis_index("c")
        for t in range(n):
            for k in range(1, 4):
                cp = pltpu.make_async_remote_copy(
                    src_ref=p_refs[t].at[k], dst_ref=land_refs[t].at[k],
                    send_sem=send_sems.at[3 * t + k - 1], recv_sem=recv_sems.at[3 * t + k - 1],
                    device_id=_block_owner(x, y, c, k), device_id_type=MESH)
                cp.wait_send()
                cp.wait_recv()

    outs = _pcall(
        body, name=name,
        out_shape=(*[pltpu.HBM(a.shape, a.dtype) for a in ps], *[pltpu.HBM(a.shape, a.dtype) for a in lands]),
        in_specs=[HBM_SPEC] * (2 * n) + [SEM_SPEC, SEM_SPEC, ANY_SPEC],
        out_specs=[HBM_SPEC] * (2 * n),
        input_output_aliases={i: i for i in range(2 * n)},
        compiler_params=pltpu.CompilerParams(has_side_effects=EFFECT),
    )(*ps, *lands, send_sems, recv_sems, after)
    return list(outs[:n]), list(outs[n:])


def _rs_sibling_copies(g_refs, land_refs, send_sems, recv_sems):
    x, y, c = lax.axis_index("x"), lax.axis_index("y"), lax.axis_index("c")
    me = 4 * x + 2 * y + c
    return [pltpu.make_async_remote_copy(
        src_ref=g_refs[t].at[jnp.bitwise_xor(2 * k + 1, me)], dst_ref=land_refs[t].at[k],
        send_sem=send_sems.at[4 * t + k], recv_sem=recv_sems.at[4 * t + k],
        device_id=(x, y, 1 - c), device_id_type=MESH) for t in range(len(g_refs)) for k in range(4)]


def _rs_sibling_start(gs, name):
    n = len(gs)

    def body(*refs):
        for cp in _rs_sibling_copies(refs[:n], refs[n:2 * n], refs[2 * n], refs[2 * n + 1]):
            cp.start()
        refs[-1][...] = jnp.zeros_like(refs[-1])

    lands = [lax.empty((4,) + g.shape[1:], g.dtype) for g in gs]
    outs = _pcall(
        body, name=name,
        out_shape=(pltpu.SemaphoreType.DMA((4 * n,)), pltpu.SemaphoreType.DMA((4 * n,)),
                   *[pltpu.HBM(a.shape, a.dtype) for a in gs], *[pltpu.HBM(a.shape, a.dtype) for a in lands],
                   _sds((8, LANES), F32)),
        in_specs=[HBM_SPEC] * (2 * n),
        out_specs=(SEM_SPEC, SEM_SPEC, *[HBM_SPEC] * (2 * n), pl.BlockSpec(memory_space=pltpu.VMEM)),
        input_output_aliases={i: 2 + i for i in range(2 * n)},
        compiler_params=pltpu.CompilerParams(has_side_effects=EFFECT),
    )(*[_hbm(a) for a in gs], *[_hbm(a) for a in lands])
    return outs[0], outs[1], list(outs[2:2 + n]), list(outs[2 + n:2 + 2 * n]), outs[-1]


def _rs_sibling_wait(started, after, name):
    send_sems, recv_sems, gs, lands, _ = started
    n = len(gs)

    def body(*refs):
        for cp in _rs_sibling_copies(refs[:n], refs[n:2 * n], refs[2 * n], refs[2 * n + 1]):
            cp.wait_send()
            cp.wait_recv()

    outs = _pcall(
        body, name=name,
        out_shape=(*[pltpu.HBM(a.shape, a.dtype) for a in gs], *[pltpu.HBM(a.shape, a.dtype) for a in lands]),
        in_specs=[HBM_SPEC] * (2 * n) + [SEM_SPEC, SEM_SPEC, ANY_SPEC],
        out_specs=[HBM_SPEC] * (2 * n),
        input_output_aliases={i: i for i in range(2 * n)},
        compiler_params=pltpu.CompilerParams(has_side_effects=EFFECT),
    )(*gs, *lands, send_sems, recv_sems, after)
    return list(outs[:n]), list(outs[n:])


def _rs_add(g, r1, me_arr, name):
    _, R, C = g.shape
    tr = _tile(R, 256)

    def body(me_ref, g_ref, r_ref, o_ref):
        o_ref[...] = (g_ref[...].astype(F32) + r_ref[...].astype(F32)).astype(BF16)

    grid_spec = pltpu.PrefetchScalarGridSpec(
        num_scalar_prefetch=1, grid=(4, R // tr),
        in_specs=[pl.BlockSpec((None, tr, C), lambda k, r, me: (jnp.bitwise_xor(2 * k, me[0]), r, 0)),
                  pl.BlockSpec((None, tr, C), lambda k, r, me: (k, r, 0))],
        out_specs=pl.BlockSpec((None, tr, C), lambda k, r, me: (k, r, 0)))
    return _pcall(body, name=name, grid_spec=grid_spec, out_shape=_sds((4, R, C), BF16),
                  compiler_params=_cp(("parallel", "parallel")))(me_arr, g, r1)


_DIMS = {"nn": (((1,), (0,)), ((), ())), "nt": (((1,), (1,)), ((), ())), "tn": (((0,), (0,)), ((), ()))}


def _dot(a, b, mode="nn"):
    return lax.dot_general(a, b, _DIMS[mode], preferred_element_type=F32)


def _matmul(name, mode, grid, nk, a, a_spec, b, b_spec, out_shapes, out_specs, acc_shape,
            epilogue="plain", extras=(), extra_specs=(), sem=None, widen=None, deps=()):
    n_extra, n_out = len(extras) + len(deps), len(out_shapes)
    extras, extra_specs = list(extras) + list(deps), list(extra_specs) + [ANY_SPEC] * len(deps)

    def body(*refs):
        a_ref, b_ref = refs[0], refs[1]
        ex = refs[2:2 + n_extra - len(deps)]
        outs = refs[2 + n_extra:2 + n_extra + n_out]
        a_tile, b_tile = a_ref[...].astype(BF16), b_ref[...].astype(BF16)
        if widen == "a":
            a_tile = jnp.concatenate([a_tile] + [e[...].astype(BF16) for e in ex], axis=1)
        elif widen == "b":
            b_tile = jnp.concatenate([b_tile] + [e[...].astype(BF16) for e in ex], axis=1)
        part = _dot(a_tile, b_tile, mode)

        def finish(acc):
            if epilogue == "plain":
                if len(ex) and widen is None:
                    acc = acc + ex[0][...]
                outs[0][...] = acc.astype(outs[0].dtype)
            elif epilogue == "relu2":
                u = acc + ex[0][...]
                outs[0][...] = u.astype(outs[0].dtype)
                outs[1][...] = jnp.square(jnp.maximum(u, 0.0)).astype(outs[1].dtype)
            else:
                du = acc * (2.0 * jnp.maximum(ex[0][...].astype(F32), 0.0))
                outs[0][...] = du.astype(outs[0].dtype)
                col = jnp.sum(du, axis=0, keepdims=True)
                inner = pl.program_id(1)

                @pl.when(inner == 0)
                def _():
                    outs[1][...] = col

                @pl.when(inner > 0)
                def _():
                    outs[1][...] += col

        if nk == 1:
            finish(part)
        else:
            acc_ref = refs[2 + n_extra + n_out]
            kidx = pl.program_id(len(grid) - 1)

            @pl.when(kidx == 0)
            def _():
                acc_ref[...] = part

            @pl.when(kidx > 0)
            def _():
                acc_ref[...] += part

            @pl.when(kidx == nk - 1)
            def _():
                finish(acc_ref[...])

    if sem is None:
        sem = ("parallel",) * (len(grid) - (1 if nk > 1 else 0)) + (("arbitrary",) if nk > 1 else ())
    outs = _pcall(
        body, name=name, grid=grid,
        in_specs=[a_spec, b_spec] + list(extra_specs), out_specs=list(out_specs),
        out_shape=list(out_shapes),
        scratch_shapes=[pltpu.VMEM(acc_shape, F32)] if nk > 1 else [],
        compiler_params=_cp(sem),
    )(a, b, *extras)
    return list(outs)


def _cast_bf16(w3, l, me_arr, name, deps=()):
    _, R, C = w3.shape
    tr = _tile(R, 256)

    def body(me_ref, x_ref, *rest):
        rest[-1][...] = x_ref[...].astype(BF16)

    grid_spec = pltpu.PrefetchScalarGridSpec(
        num_scalar_prefetch=1, grid=(R // tr,),
        in_specs=[pl.BlockSpec((None, tr, C), lambda r, me: (l, r, 0))] + [ANY_SPEC] * len(deps),
        out_specs=pl.BlockSpec((None, tr, C), lambda r, me: (me[0], r, 0)))
    return _pcall(body, name=name, grid_spec=grid_spec, out_shape=_sds((N_DEV, R, C), BF16),
                  compiler_params=_cp(("parallel",)))(me_arr, w3, *deps)


def _unblock(wb, name):
    n, R, C = wb.shape

    def body(x_ref, o_ref):
        o_ref[...] = x_ref[...]

    return _pcall(body, name=name, grid=(n,),
                  in_specs=[pl.BlockSpec((None, R, C), lambda j: (j, 0, 0))],
                  out_specs=pl.BlockSpec((R, C), lambda j: (0, j)),
                  out_shape=_sds((R, n * C), wb.dtype), compiler_params=_cp(("parallel",)))(wb)


def _silu_bf16(c16, name):
    def body(c_ref, o_ref):
        v = c_ref[...]
        o_ref[...] = (v / (1.0 + jnp.exp(-v))).astype(BF16)

    return _pcall(body, name=name, out_shape=_sds(c16.shape, BF16))(c16)


def _sum_devices(g, name):
    def body(g_ref, o_ref):
        acc = g_ref[0]
        for d in range(1, N_DEV):
            acc = acc + g_ref[d]
        o_ref[...] = acc

    return _pcall(body, name=name, out_shape=_sds(g.shape[1:], F32))(g)


def _modulate(name, x, sc, sh, deps=()):
    S, D = x.shape
    ts = _tile(S, 256)

    def body(x_ref, sc_ref, sh_ref, *rest):
        rest[-1][...] = (x_ref[...] * (1.0 + sc_ref[...]) + sh_ref[...]).astype(BF16)

    row = pl.BlockSpec((ts, D), lambda i: (i, 0))
    vec = pl.BlockSpec((1, D), lambda i: (0, 0))
    return _pcall(body, name=name, grid=(S // ts,), in_specs=[row, vec, vec] + [ANY_SPEC] * len(deps),
                  out_specs=row, out_shape=_sds((S, D), BF16),
                  compiler_params=_cp(("parallel",)))(x, sc, sh, *deps)


def _ln_stats(r):
    mu = jnp.mean(r, axis=-1, keepdims=True)
    xc = r - mu
    var = jnp.mean(xc * xc, axis=-1, keepdims=True)
    rstd = lax.rsqrt(var + LN_EPS)
    return xc * rstd, rstd


def _ln_fwd(name, xin, y, gate, lng, lnb, sc, sh):
    S, D = xin.shape
    ts = _tile(S, 256)

    def body(x_ref, y_ref, g_ref, lg_ref, lb_ref, sc_ref, sh_ref, xo_ref, h_ref):
        r = DEEPNORM_ALPHA * x_ref[...] + (1.0 + g_ref[...]) * y_ref[...]
        xhat, _ = _ln_stats(r)
        xo = xhat * lg_ref[...] + lb_ref[...]
        xo_ref[...] = xo
        h_ref[...] = (xo * (1.0 + sc_ref[...]) + sh_ref[...]).astype(BF16)

    row = pl.BlockSpec((ts, D), lambda i: (i, 0))
    vec = pl.BlockSpec((1, D), lambda i: (0, 0))
    return _pcall(body, name=name, grid=(S // ts,), in_specs=[row, row] + [vec] * 5,
                  out_specs=[row, row], out_shape=[_sds((S, D), F32), _sds((S, D), BF16)],
                  compiler_params=_cp(("parallel",)))(xin, y, gate, lng, lnb, sc, sh)


def _boundary(name, mode, big, vecs, deps=()):
    S, D = big[0].shape
    ts = _tile(S, 128)
    nb, nv, nd = len(big), len(vecs), len(deps)

    def body(*refs):
        b = [r[...] for r in refs[:nb]]
        v = [r[...] for r in refs[nb:nb + nv]]
        outs = refs[nb + nv + nd:]
        vec_ref = outs[-1]
        i = pl.program_id(0)
        rows = [None] * 8

        def colsum(a):
            return jnp.sum(a, axis=0, keepdims=True)

        if mode == "first":
            dxr, dh, xk = b
            sc = v[0]
            outs[0][...] = dxr + dh * (1.0 + sc)
            rows[0], rows[1] = colsum(dh * xk), colsum(dh)
        else:
            if mode == "mid":
                dxr, dh, xprev, yk = b
                sc, gate, lng, lnb = v
            else:
                target, xprev, yk = b
                gate, lng, lnb = v
            r = DEEPNORM_ALPHA * xprev + (1.0 + gate) * yk
            xhat, rstd = _ln_stats(r)
            if mode == "mid":
                xk = xhat * lng + lnb
                d = dxr + dh * (1.0 + sc)
                rows[0], rows[1] = colsum(dh * xk), colsum(dh)
            else:
                err = xhat * lng + lnb - target
                d = err * (1.0 / D)
                rows[6] = jnp.zeros((1, D), F32) + jnp.sum(err * err)
            rows[2], rows[3] = colsum(d * xhat), colsum(d)
            dxhat = d * lng
            m1 = jnp.mean(dxhat, axis=-1, keepdims=True)
            m2 = jnp.mean(dxhat * xhat, axis=-1, keepdims=True)
            dr = rstd * (dxhat - m1 - xhat * m2)
            dy = (1.0 + gate) * dr
            rows[4], rows[5] = colsum(dr * yk), colsum(dy)
            outs[0][...] = dy.astype(BF16)
            outs[1][...] = DEEPNORM_ALPHA * dr
        @pl.when(i == 0)
        def _():
            vec_ref[...] = jnp.zeros_like(vec_ref)

        for j, row_j in enumerate(rows):
            if row_j is not None:
                vec_ref[j:j + 1, :] += row_j

    row = pl.BlockSpec((ts, D), lambda i: (i, 0))
    vec = pl.BlockSpec((1, D), lambda i: (0, 0))
    vec_out = pl.BlockSpec((8, D), lambda i: (0, 0))
    if mode == "first":
        out_specs, out_shape = [row, vec_out], [_sds((S, D), F32), _sds((8, D), F32)]
    else:
        out_specs = [row, row, vec_out]
        out_shape = [_sds((S, D), BF16), _sds((S, D), F32), _sds((8, D), F32)]
    return _pcall(body, name=name, grid=(S // ts,), in_specs=[row] * nb + [vec] * nv + [ANY_SPEC] * nd,
                  out_specs=out_specs, out_shape=out_shape,
                  compiler_params=_cp(("arbitrary",)))(*big, *vecs, *deps)


def _split_bf16(a):
    hi = a.astype(BF16)
    lo = (a - hi.astype(F32)).astype(BF16)
    return hi, lo


def _cumsum_matrix(tri):
    half = jnp.concatenate([tri.astype(BF16), jnp.ones((LANES, LANES), BF16)], axis=1)
    return jnp.concatenate([half, half], axis=0)


def _chunk_sums(a, mat):
    hi, lo = _split_bf16(a)
    both = _dot(jnp.concatenate([hi, lo], axis=1), mat)
    return both[:, :LANES], both[:, LANES:]


LOG2_E = 1.4426950408889634


def _log2_1m_beta(z2):
    return -(jnp.maximum(z2, 0.0) + jnp.log2(1.0 + jnp.exp2(-jnp.abs(z2))))


def _attn_fwd(name, qkv3):
    _, S, D = qkv3.shape
    H = D // HEAD_DIM
    tq = _tile(S, 512)
    nch = tq // LANES
    scale = HEAD_DIM ** -0.5

    def body(q_ref, k_ref, v_ref, o_ref, t_ref):
        i = pl.program_id(1)
        q = q_ref[...]
        jj = lax.broadcasted_iota(jnp.int32, (LANES, LANES), 0)
        ss = lax.broadcasted_iota(jnp.int32, (LANES, LANES), 1)
        from_here = _cumsum_matrix(jj >= ss)
        causal = (lax.broadcasted_iota(jnp.int32, (tq, tq), 1) < lax.broadcasted_iota(jnp.int32, (tq, tq), 0))

        def group(k0, masked, tail, acc):
            kb = k_ref[pl.ds(k0, tq), :]
            vb = v_ref[pl.ds(k0, tq), :]
            z = _dot(q, kb, "nt") * (scale * LOG2_E)
            lm = _log2_1m_beta(z)
            if masked:
                lm = jnp.where(causal, lm, 0.0)
            args = [None] * nch
            for ch in reversed(range(nch)):
                sl = slice(ch * LANES, (ch + 1) * LANES)
                cum, tot = _chunk_sums(lm[:, sl], from_here)
                args[ch] = z[:, sl] + (cum + tail)
                tail = tail + tot
            a = jnp.exp2(jnp.concatenate(args, axis=1))
            if masked:
                a = jnp.where(causal, a, 0.0)
            return tail, acc + _dot(a.astype(BF16), vb)

        zero = jnp.zeros((tq, HEAD_DIM), F32)
        carry = group(pl.multiple_of(i * tq, tq), True, zero, zero)
        def at(g):
            return pl.multiple_of(g * tq, tq)

        carry = lax.fori_loop(
            0, i // 2, lambda p, cr: group(at(i - 2 - 2 * p), False, *group(at(i - 1 - 2 * p), False, *cr)), carry)
        tail, acc = lax.fori_loop(0, i % 2, lambda _, cr: group(0, False, *cr), carry)
        o_ref[...] = acc.astype(BF16)
        t_ref[...] = tail

    return _pcall(
        body, name=name, grid=(H, S // tq),
        in_specs=[pl.BlockSpec((None, tq, HEAD_DIM), lambda h, i: (0, i, h)),
                  pl.BlockSpec((None, S, HEAD_DIM), lambda h, i: (1, 0, h)),
                  pl.BlockSpec((None, S, HEAD_DIM), lambda h, i: (2, 0, h))],
        out_specs=[pl.BlockSpec((tq, HEAD_DIM), lambda h, i: (i, h)),
                   pl.BlockSpec((tq, HEAD_DIM), lambda h, i: (i, h))],
        out_shape=[_sds((S, D), BF16), _sds((S, D), F32)],
        compiler_params=_cp(("parallel", "parallel")),
    )(qkv3, qkv3, qkv3)


def _attn_bwd(name, qkv3, do, totals):
    _, S, D = qkv3.shape
    H = D // HEAD_DIM
    tq = _tile(S, 512)
    nch = tq // LANES
    nq = S // tq
    scale = HEAD_DIM ** -0.5

    def body(q_ref, k_ref, v_ref, do_ref, t_ref, out_ref, dk_acc, dv_acc):
        i = pl.program_id(1)

        @pl.when(i == 0)
        def _():
            dk_acc[...] = jnp.zeros_like(dk_acc)
            dv_acc[...] = jnp.zeros_like(dv_acc)

        q = q_ref[...]
        do_b = do_ref[...]
        total = t_ref[...]
        jj = lax.broadcasted_iota(jnp.int32, (LANES, LANES), 0)
        ss = lax.broadcasted_iota(jnp.int32, (LANES, LANES), 1)
        before = _cumsum_matrix(jj < ss)
        causal = (lax.broadcasted_iota(jnp.int32, (tq, tq), 1) < lax.broadcasted_iota(jnp.int32, (tq, tq), 0))

        def group(k0, masked, head, dsum, dq):
            kb = k_ref[pl.ds(k0, tq), :]
            vb = v_ref[pl.ds(k0, tq), :]
            z = _dot(q, kb, "nt") * (scale * LOG2_E)
            lm = _log2_1m_beta(z)
            sig = jnp.exp2(z + lm)
            if masked:
                lm = jnp.where(causal, lm, 0.0)
            args = []
            for ch in range(nch):
                sl = slice(ch * LANES, (ch + 1) * LANES)
                cum, tot = _chunk_sums(lm[:, sl], before)
                args.append(z[:, sl] + (total - head - cum))
                head = head + tot
            a = jnp.exp2(jnp.concatenate(args, axis=1))
            if masked:
                a = jnp.where(causal, a, 0.0)
            de = _dot(do_b, vb, "nt") * a
            dv_acc[pl.ds(k0, tq), :] += _dot(a.astype(BF16), do_b, "tn")
            sums = []
            for ch in range(nch):
                sl = slice(ch * LANES, (ch + 1) * LANES)
                cum, tot = _chunk_sums(de[:, sl], before)
                sums.append(dsum + cum)
                dsum = dsum + tot
            dz = de - sig * (de + jnp.concatenate(sums, axis=1))
            if masked:
                dz = jnp.where(causal, dz, 0.0)
            dzb = dz.astype(BF16)
            dk_acc[pl.ds(k0, tq), :] += _dot(dzb, q, "tn")
            return head, dsum, dq + _dot(dzb, kb)

        zero = jnp.zeros((tq, HEAD_DIM), F32)
        def at(g):
            return pl.multiple_of(g * tq, tq)

        carry = lax.fori_loop(
            0, i // 2, lambda p, cr: group(at(2 * p + 1), False, *group(at(2 * p), False, *cr)), (zero, zero, zero))
        carry = lax.fori_loop(0, i % 2, lambda _, cr: group(at(i - 1), False, *cr), carry)
        q0 = pl.multiple_of(i * tq, tq)
        _, _, dq = group(q0, True, *carry)
        out_ref[0, pl.ds(q0, tq), :] = (dq * scale).astype(BF16)

        @pl.when(i == nq - 1)
        def _():
            out_ref[1] = (dk_acc[...] * scale).astype(BF16)
            out_ref[2] = dv_acc[...].astype(BF16)

    return _pcall(
        body, name=name, grid=(H, nq),
        in_specs=[pl.BlockSpec((None, tq, HEAD_DIM), lambda h, i: (0, i, h)),
                  pl.BlockSpec((None, S, HEAD_DIM), lambda h, i: (1, 0, h)),
                  pl.BlockSpec((None, S, HEAD_DIM), lambda h, i: (2, 0, h)),
                  pl.BlockSpec((tq, HEAD_DIM), lambda h, i: (i, h)),
                  pl.BlockSpec((tq, HEAD_DIM), lambda h, i: (i, h))],
        out_specs=pl.BlockSpec((3, S, HEAD_DIM), lambda h, i: (0, 0, h)),
        out_shape=_sds((3, S, D), BF16),
        scratch_shapes=[pltpu.VMEM((S, HEAD_DIM), F32), pltpu.VMEM((S, HEAD_DIM), F32)],
        compiler_params=_cp(("parallel", "arbitrary")),
    )(qkv3, qkv3, qkv3, do, totals)


def _conv_taps(g, t):
    S = g.shape[0]
    g1 = jnp.where(t >= 1, pltpu.roll(g, 1, 0), 0.0)
    g2 = jnp.where(t >= 2, pltpu.roll(g, 2, 0), 0.0)
    return g1, g2


def _conv_fwd(name, bcu3, cw, cb):
    _, S, D = bcu3.shape
    tf = LANES

    def body(x_ref, w_ref, b_ref, p_ref):
        t = lax.broadcasted_iota(jnp.int32, (S, tf), 0)
        g = x_ref[1] * x_ref[2]
        g1, g2 = _conv_taps(g, t)
        y = w_ref[0:1, :] * g2 + w_ref[1:2, :] * g1 + w_ref[2:3, :] * g + b_ref[...]
        p_ref[...] = (x_ref[0] * y).astype(BF16)

    return _pcall(
        body, name=name, grid=(D // tf,),
        in_specs=[pl.BlockSpec((3, S, tf), lambda f: (0, 0, f)), pl.BlockSpec((3, tf), lambda f: (0, f)),
                  pl.BlockSpec((1, tf), lambda f: (0, f))],
        out_specs=pl.BlockSpec((S, tf), lambda f: (0, f)), out_shape=_sds((S, D), BF16),
        compiler_params=_cp(("parallel",)),
    )(bcu3, cw, cb)


def _conv_bwd(name, bcu3, dp, cw, cb):
    _, S, D = bcu3.shape
    tf = LANES

    def body(x_ref, dp_ref, w_ref, b_ref, o_ref, vec_ref):
        t = lax.broadcasted_iota(jnp.int32, (S, tf), 0)
        bg, cg, u = x_ref[0], x_ref[1], x_ref[2]
        g = cg * u
        g1, g2 = _conv_taps(g, t)
        w0, w1, w2 = w_ref[0:1, :], w_ref[1:2, :], w_ref[2:3, :]
        y = w0 * g2 + w1 * g1 + w2 * g + b_ref[...]
        dpv = dp_ref[...]
        o_ref[0] = (dpv * y).astype(BF16)
        dy = dpv * bg

        def colsum(a):
            return jnp.sum(a, axis=0, keepdims=True)

        vec_ref[...] = jnp.zeros_like(vec_ref)
        for j, row_j in enumerate((dy, dy * g2, dy * g1, dy * g)):
            vec_ref[j:j + 1, :] = colsum(row_j)
        dy1 = jnp.where(t < S - 1, pltpu.roll(dy, S - 1, 0), 0.0)
        dy2 = jnp.where(t < S - 2, pltpu.roll(dy, S - 2, 0), 0.0)
        dg = w2 * dy + w1 * dy1 + w0 * dy2
        o_ref[1] = (dg * u).astype(BF16)
        o_ref[2] = (dg * cg).astype(BF16)

    return _pcall(
        body, name=name, grid=(D // tf,),
        in_specs=[pl.BlockSpec((3, S, tf), lambda f: (0, 0, f)), pl.BlockSpec((S, tf), lambda f: (0, f)),
                  pl.BlockSpec((3, tf), lambda f: (0, f)), pl.BlockSpec((1, tf), lambda f: (0, f))],
        out_specs=[pl.BlockSpec((3, S, tf), lambda f: (0, 0, f)), pl.BlockSpec((8, tf), lambda f: (0, f))],
        out_shape=[_sds((3, S, D), BF16), _sds((8, D), F32)],
        compiler_params=_cp(("parallel",)),
    )(bcu3, dp, cw, cb)


def _adamw(name, w3, m3, v3, l, parts, carry=None):
    L, R, C = w3.shape
    budget_rows = (1 << 18) // C
    tr = R if R <= budget_rows else _tile(R, 1 << (budget_rows.bit_length() - 1))
    n_parts = len(parts)
    n_carry = 0 if carry is None else 4
    c1 = 1.0 / (1.0 - ADAM_B1 ** ADAM_STEP)
    c2 = 1.0 / (1.0 - ADAM_B2 ** ADAM_STEP)

    def body(*refs):
        w_ref, m_ref, v_ref = refs[:3]
        p_refs = refs[3:3 + n_parts]
        og, od, om, ov = refs[3 + n_parts + n_carry:]
        g = p_refs[0][...].astype(F32)
        for p in p_refs[1:]:
            g = g + p[...].astype(F32)
        m = ADAM_B1 * m_ref[...] + (1.0 - ADAM_B1) * g
        v = ADAM_B2 * v_ref[...] + (1.0 - ADAM_B2) * jnp.square(g)
        og[...] = g
        om[...] = m
        ov[...] = v
        od[...] = -ADAM_LR * ((m * c1) / (jnp.sqrt(v * c2) + ADAM_EPS) + ADAM_WD * w_ref[...])

    lay = pl.BlockSpec((None, tr, C), lambda r: (l, r, 0))
    in_specs = [lay, lay, lay]
    args = [w3, m3, v3]
    for arr, idx in parts:
        in_specs.append(pl.BlockSpec((None, tr, C), functools.partial(lambda r, j: (j, r, 0), j=idx)))
        args.append(arr)
    aliases = {}
    if carry is not None:
        for j, buf in enumerate(carry):
            aliases[len(args)] = j
            in_specs.append(ANY_SPEC)
            args.append(buf)
    return _pcall(body, name=name, grid=(R // tr,), in_specs=in_specs, out_specs=[lay] * 4,
                  out_shape=[_sds((L, R, C), F32)] * 4, input_output_aliases=aliases,
                  compiler_params=_cp(("parallel",)))(*args)


def kernel(x, c, mod_w, mod_b, ln_g, ln_b, attn_w_qkv, attn_w_o, conv_w_in, conv_w, conv_b, conv_w_out, mlp_w1, mlp_b1, mlp_w2, mlp_b2, loss_target, m_mod_w, m_mod_b, m_ln_g, m_ln_b, m_attn_w_qkv, m_attn_w_o, m_conv_w_in, m_conv_w, m_conv_b, m_conv_w_out, m_mlp_w1, m_mlp_b1, m_mlp_w2, m_mlp_b2, v_mod_w, v_mod_b, v_ln_g, v_ln_b, v_attn_w_qkv, v_attn_w_o, v_conv_w_in, v_conv_w, v_conv_b, v_conv_w_out, v_mlp_w1, v_mlp_b1, v_mlp_w2, v_mlp_b2):
    xs, target = x[0], loss_target[0]
    S, D = xs.shape
    L = mod_w.shape[0]
    LA, LC = attn_w_qkv.shape[0], conv_w_in.shape[0]
    F = mlp_b1.shape[1]
    DB, FB, NC = D // N_DEV, F // N_DEV, mod_w.shape[2]
    CT = DB
    me = 4 * lax.axis_index("x") + 2 * lax.axis_index("y") + lax.axis_index("c")
    me_arr = jnp.reshape(me, (1,)).astype(jnp.int32)

    K = 2 * L

    def mixer_weights(l):
        if l % 2 == 0:
            return attn_w_qkv, attn_w_o, l // 2
        return conv_w_in, conv_w_out, l // 2

    def group_shards(k, deps):
        l = k // 2
        if k % 2 == 0:
            w_in3, w_out3, j = mixer_weights(l)
            return [_cast_bf16(w_in3, j, me_arr, f"cast_win_{l}", deps), _cast_bf16(w_out3, j, me_arr, f"cast_wout_{l}", deps)]
        return [_cast_bf16(mlp_w1, l, me_arr, f"cast_w1_{l}", deps), _cast_bf16(mlp_w2, l, me_arr, f"cast_w2_{l}", deps)]

    first_gather = _ag_start(group_shards(0, []), [], "ag_start_0")
    shards = {k: group_shards(k, [first_gather[3]]) for k in range(1, K)}

    small = [c, ln_g, ln_b, conv_w, conv_b]
    pack = jnp.concatenate([a.reshape(-1) for a in small]).reshape(-1, LANES)
    gath = _allgather([pack], "ag_small_params", deps=[first_gather[3]])[0].reshape(N_DEV, -1)
    offs, pos = [], 0
    for a in small:
        offs.append((pos, a.size))
        pos += a.size

    def unshard(idx, lead):
        o, n = offs[idx]
        a = gath[:, o:o + n].reshape((N_DEV,) + lead + (DB,))
        return jnp.moveaxis(a, 0, -2).reshape(lead + (D,))

    c_all = gath[:, :D]
    lng_full, lnb_full = unshard(1, (L, 2)), unshard(2, (L, 2))
    cw_full, cb_full = unshard(3, (LC, 3)), unshard(4, (LC,))

    cond16 = _silu_bf16(jnp.pad(c_all, ((0, 16 - N_DEV), (0, 0))), "silu_cond")
    tn_mod = NC // 3
    bias_mod = lax.dynamic_slice_in_dim(mod_b, me * NC, NC, axis=1).reshape(L, 1, NC)
    mod_part = _matmul(
        "mod_fwd", "nn", (L, NC // tn_mod), 1,
        cond16, pl.BlockSpec((16, D), lambda l, n: (0, 0)),
        mod_w, pl.BlockSpec((None, D, tn_mod), lambda l, n: (l, 0, n)),
        [_sds((L, 16, NC), F32)], [pl.BlockSpec((None, 16, tn_mod), lambda l, n: (l, 0, n))], None,
        extras=[bias_mod], extra_specs=[pl.BlockSpec((None, 1, tn_mod), lambda l, n: (l, 0, n))])[0]
    mod_gath = _allgather([mod_part[:, :N_DEV].reshape(-1, LANES)], "ag_mod")[0].reshape(N_DEV, L, N_DEV, NC)
    mod_me = lax.dynamic_index_in_dim(mod_gath, me, axis=2, keepdims=False)
    mod_me = jnp.moveaxis(mod_me, 0, 1).reshape(L, 6, 1, D)

    def mod_vecs(k):
        l, o = k // 2, 3 * (k % 2)
        return mod_me[l, o], mod_me[l, o + 1], mod_me[l, o + 2]

    wg = [[None] * 4 for _ in range(L)]
    in_flight = {0: first_gather, 1: _ag_start(shards.pop(1), [mod_gath], "ag_start_1")}

    def fetch_group(k, after):
        first, second = _ag_finish(_ag_wait(in_flight.pop(k), after, f"ag_wait_{k}"), f"ag_finish_{k}")
        if k + 2 < K:
            in_flight[k + 2] = _ag_start(shards.pop(k + 2), [first], f"ag_start_{k + 2}")
        rows = D if k % 2 == 0 else F
        wg[k // 2][2 * (k % 2)], wg[k // 2][2 * (k % 2) + 1] = first, second.reshape(rows, D)

    n_ct = 3 * D // CT
    tm_big = _tile(S, 2048)

    def in_proj(l, h, out_dtype, deps):
        return _matmul(
            f"in_proj_{l}", "nn", (S // tm_big, n_ct), 1,
            h, pl.BlockSpec((tm_big, D), lambda i, ct: (i, 0)),
            wg[l][0], pl.BlockSpec((None, D, CT), lambda i, ct: (ct // 3, 0, ct % 3)),
            [_sds((3, S, D), out_dtype)],
            [pl.BlockSpec((None, tm_big, CT), lambda i, ct: (ct // N_DEV, i, ct % N_DEV))], None, deps=deps)[0]

    tm, tn_half = _tile(S, 512), _tile(D, 1024)
    tm_up = _tile(S, 1024)

    def out_proj(l, a):
        return _matmul(
            f"out_proj_{l}", "nn", (D // tn_half, S // tm), 1,
            a, pl.BlockSpec((tm, D), lambda n, i: (i, 0)),
            wg[l][1], pl.BlockSpec((D, tn_half), lambda n, i: (0, n)),
            [_sds((S, D), F32)], [pl.BlockSpec((tm, tn_half), lambda n, i: (i, n))], None)[0]

    def mlp_up(l, h, deps):
        blk = pl.BlockSpec((tm_up, FB), lambda j, i: (i, j))
        return _matmul(
            f"mlp_up_{l}", "nn", (N_DEV, S // tm_up), 1,
            h, pl.BlockSpec((tm_up, D), lambda j, i: (i, 0)),
            wg[l][2], pl.BlockSpec((None, D, FB), lambda j, i: (j, 0, 0)),
            [_sds((S, F), BF16), _sds((S, F), BF16)], [blk, blk], None, epilogue="relu2",
            extras=[mlp_b1[l].reshape(1, F)], extra_specs=[pl.BlockSpec((1, FB), lambda j, i: (0, j))], deps=deps)

    tn_q = _tile(D, 512)

    def mlp_down(l, act):
        return _matmul(
            f"mlp_down_{l}", "nn", (D // tn_q, S // tm), 1,
            act, pl.BlockSpec((tm, F), lambda n, i: (i, 0)),
            wg[l][3], pl.BlockSpec((F, tn_q), lambda n, i: (0, n)),
            [_sds((S, D), F32)], [pl.BlockSpec((tm, tn_q), lambda n, i: (i, n))], None,
            extras=[mlp_b2[l].reshape(1, D)], extra_specs=[pl.BlockSpec((1, tn_q), lambda n, i: (0, n))])[0]

    xin, ys, hs, saved = [xs], [], [], []
    sh0, sc0, _ = mod_vecs(0)
    cast_done = [in_flight[1][3]] + [buf for k in sorted(shards) for buf in shards[k]]
    h = _modulate("modulate_in", xs, sc0, sh0, cast_done)
    for k in range(K):
        l = k // 2
        hs.append(h)
        fetch_group(k, h)
        started = [in_flight[k + 2][3]] if k + 2 < K else []
        if k % 2 == 0 and l % 2 == 0:
            qkv3 = in_proj(l, h, BF16, started)
            o, totals = _attn_fwd(f"attn_fwd_{l}", qkv3)
            y = out_proj(l, o)
            saved.append((qkv3, o, totals))
        elif k % 2 == 0:
            bcu3 = in_proj(l, h, F32, started)
            j = l // 2
            p = _conv_fwd(f"conv_fwd_{l}", bcu3, cw_full[j], cb_full[j].reshape(1, D))
            y = out_proj(l, p)
            saved.append((bcu3, p))
        else:
            u, act = mlp_up(l, h, started)
            y = mlp_down(l, act)
            saved.append((u, act))
        ys.append(y)
        if k + 1 < K:
            _, _, gate = mod_vecs(k)
            sh_n, sc_n, _ = mod_vecs(k + 1)
            x_next, h = _ln_fwd(f"ln_fwd_{k}", xin[k], y, gate, lng_full[l, k % 2].reshape(1, D),
                                lnb_full[l, k % 2].reshape(1, D), sc_n, sh_n)
            xin.append(x_next)

    tm_g = _tile(D, 512)

    def grad_rows(name, a, dy, rows):
        return _matmul(
            name, "tn", (rows // tm_g, D // tn_half), 1,
            a, pl.BlockSpec((S, tm_g), lambda m, n: (0, m)),
            dy, pl.BlockSpec((S, tn_half), lambda m, n: (0, n)),
            [_sds((rows, D), BF16)], [pl.BlockSpec((tm_g, tn_half), lambda m, n: (m, n))], None)[0]

    def back_out_proj(l, dy, out_dtype):
        return _matmul(
            f"back_out_proj_{l}", "nt", (D // tn_half, S // tm), 1,
            dy, pl.BlockSpec((tm, D), lambda n, i: (i, 0)),
            wg[l][1], pl.BlockSpec((tn_half, D), lambda n, i: (n, 0)),
            [_sds((S, D), out_dtype)], [pl.BlockSpec((tm, tn_half), lambda n, i: (i, n))], None)[0]

    def back_in_proj(l, d3, deps):
        def part(p):
            return pl.BlockSpec((None, tm, D), lambda i, n: (p, i, 0))
        return _matmul(
            f"back_in_proj_{l}", "nt", (S // tm, D // tn_q), 1,
            d3, part(0),
            _unblock(wg[l][0], f"unblock_win_{l}"), pl.BlockSpec((tn_q, 3 * D), lambda i, n: (n, 0)),
            [_sds((S, D), F32)], [pl.BlockSpec((tm, tn_q), lambda i, n: (i, n))], None,
            extras=[d3, d3], extra_specs=[part(1), part(2)], widen="a", deps=deps)[0]

    def grad_in_proj(l, h, d3):
        def tile(r):
            return pl.BlockSpec((None, S, CT), lambda j, m: ((3 * j + r) // N_DEV, 0, (3 * j + r) % N_DEV))
        return _matmul(
            f"grad_in_proj_{l}", "tn", (N_DEV, D // tm_g), 1,
            h, pl.BlockSpec((S, tm_g), lambda j, m: (0, m)),
            d3, tile(0),
            [_sds((N_DEV, D, 3 * CT), BF16)],
            [pl.BlockSpec((None, tm_g, 3 * CT), lambda j, m: (j, m, 0))], None,
            extras=[d3, d3], extra_specs=[tile(1), tile(2)], widen="b")[0]

    def back_mlp_down(l, dy, u):
        blk = pl.BlockSpec((tm_up, FB), lambda j, i: (i, j))
        return _matmul(
            f"back_mlp_down_{l}", "nt", (N_DEV, S // tm_up), 1,
            dy, pl.BlockSpec((tm_up, D), lambda j, i: (i, 0)),
            wg[l][3], pl.BlockSpec((FB, D), lambda j, i: (j, 0)),
            [_sds((S, F), BF16), _sds((1, F), F32)], [blk, pl.BlockSpec((1, FB), lambda j, i: (0, j))], None,
            epilogue="drelu2", extras=[u], extra_specs=[blk], sem=("parallel", "arbitrary"))

    def back_mlp_up(l, du, deps):
        return _matmul(
            f"back_mlp_up_{l}", "nt", (S // tm, D // tn_q), 1,
            du, pl.BlockSpec((tm, F), lambda i, n: (i, 0)),
            _unblock(wg[l][2], f"unblock_w1_{l}"), pl.BlockSpec((tn_q, F), lambda i, n: (n, 0)),
            [_sds((S, D), F32)], [pl.BlockSpec((tm, tn_q), lambda i, n: (i, n))], None, deps=deps)[0]

    def grad_mlp_up(l, h, du):
        return _matmul(
            f"grad_mlp_up_{l}", "tn", (N_DEV, D // tm_g), 1,
            h, pl.BlockSpec((S, tm_g), lambda j, m: (0, m)),
            du, pl.BlockSpec((S, FB), lambda j, m: (0, j)),
            [_sds((N_DEV, D, FB), BF16)], [pl.BlockSpec((None, tm_g, FB), lambda j, m: (j, m, 0))], None)[0]

    dmod = [[None] * 6 for _ in range(L)]
    dlng = [[None, None] for _ in range(L)]
    dlnb = [[None, None] for _ in range(L)]
    db1, db2 = [None] * L, [None] * L
    dcw, dcb = [None] * LC, [None] * LC

    carries = {"attn_in": None, "attn_out": None, "conv_in": None, "conv_out": None, "w1": None, "w2": None}
    moments = {"attn_in": (attn_w_qkv, m_attn_w_qkv, v_attn_w_qkv), "attn_out": (attn_w_o, m_attn_w_o, v_attn_w_o),
               "conv_in": (conv_w_in, m_conv_w_in, v_conv_w_in), "conv_out": (conv_w_out, m_conv_w_out, v_conv_w_out),
               "w1": (mlp_w1, m_mlp_w1, v_mlp_w1), "w2": (mlp_w2, m_mlp_w2, v_mlp_w2)}
    pending = []

    def finish_pending(after):
        if not pending:
            return
        k, started = pending.pop()
        ps, r2 = _rs_chips_wait(started, after, f"rs_wait_{k}")
        l = k // 2
        if k % 2 == 1:
            keys = [("w1", l), ("w2", l)]
        else:
            kind = "attn" if l % 2 == 0 else "conv"
            keys = [(kind + "_in", l // 2), (kind + "_out", l // 2)]
        for t, (key, idx) in enumerate(keys):
            w3, m3, v3 = moments[key]
            parts = [(ps[t], 0), (r2[t], 1), (r2[t], 2), (r2[t], 3)]
            carries[key] = _adamw(f"adamw_{key}_{l}", w3, m3, v3, idx, parts, carries[key])

    def reduce_begin(k, gs):
        return _rs_sibling_start(gs, f"rs_sibling_start_{k}")

    def reduce_send(k, exchange, after):
        finish_pending(after)
        gs, r1 = _rs_sibling_wait(exchange, after, f"rs_sibling_wait_{k}")
        ps = [_rs_add(g, r, me_arr, f"rs_add_{k}_{t}") for t, (g, r) in enumerate(zip(gs, r1))]
        started = _rs_chips_start(ps, f"rs_start_{k}")
        pending.append((k, started))
        return started[4]

    def take_ln_rows(k, vec):
        l, o = k // 2, 3 * (k % 2)
        dlng[l][k % 2], dlnb[l][k % 2] = vec[2], vec[3]
        dmod[l][o + 2] = vec[4]
        if k % 2 == 1:
            db2[l] = vec[5]

    def take_mod_rows(k, vec):
        l, o = k // 2, 3 * (k % 2)
        dmod[l][o + 1], dmod[l][o] = vec[0], vec[1]

    lk = K - 1
    _, _, gate = mod_vecs(lk)
    dy, dxr, vec = _boundary(f"boundary_{lk}", "last", (target, xin[lk], ys[lk]),
                             (gate, lng_full[lk // 2, 1].reshape(1, D), lnb_full[lk // 2, 1].reshape(1, D)))
    take_ln_rows(lk, vec)
    loss_local = 0.5 * vec[6, 0] / D
    grad_x = None
    for k in range(K - 1, -1, -1):
        l = k // 2
        if k % 2 == 1:
            u, act = saved[k]
            g_w2 = grad_rows(f"grad_mlp_down_{l}", act, dy, F).reshape(N_DEV, FB, D)
            du, db1_l = back_mlp_down(l, dy, u)
            db1[l] = db1_l[0]
            g_w1 = grad_mlp_up(l, hs[k], du)
            exchange = reduce_begin(k, [g_w1, g_w2])
            dh = back_mlp_up(l, du, [exchange[4]])
            token = reduce_send(k, exchange, dh)
        else:
            if l % 2 == 0:
                qkv3, o, totals = saved[k]
                g_out = grad_rows(f"grad_out_proj_{l}", o, dy, D).reshape(N_DEV, DB, D)
                do = back_out_proj(l, dy, BF16)
                d3 = _attn_bwd(f"attn_bwd_{l}", qkv3, do, totals)
            else:
                bcu3, p = saved[k]
                j = l // 2
                g_out = grad_rows(f"grad_out_proj_{l}", p, dy, D).reshape(N_DEV, DB, D)
                dp = back_out_proj(l, dy, F32)
                d3, cvec = _conv_bwd(f"conv_bwd_{l}", bcu3, dp, cw_full[j], cb_full[j].reshape(1, D))
                dcb[j], dcw[j] = cvec[0], cvec[1:4]
            g_in = grad_in_proj(l, hs[k], d3)
            exchange = reduce_begin(k, [g_in, g_out])
            if k == 0:
                token = reduce_send(k, exchange, d3)
                dh = back_in_proj(l, d3, [token])
            else:
                dh = back_in_proj(l, d3, [exchange[4]])
                token = reduce_send(k, exchange, dh)
        sh_k, sc_k, _ = mod_vecs(k)
        if k > 0:
            kp = k - 1
            _, _, gate = mod_vecs(kp)
            dy, dxr, vec = _boundary(
                f"boundary_{kp}", "mid", (dxr, dh, xin[kp], ys[kp]),
                (sc_k, gate, lng_full[kp // 2, kp % 2].reshape(1, D), lnb_full[kp // 2, kp % 2].reshape(1, D)),
                deps=[token])
            take_mod_rows(k, vec)
            take_ln_rows(kp, vec)
        else:
            grad_x, vec = _boundary("boundary_in", "first", (dxr, dh, xs), (sc_k,), deps=[token])
            take_mod_rows(k, vec)
    finish_pending(grad_x)
    results = carries

    flat = [jnp.stack([jnp.stack(r) for r in dmod]).reshape(-1), jnp.stack(db1).reshape(-1),
            jnp.stack(db2).reshape(-1), jnp.stack([jnp.stack(r) for r in dlng]).reshape(-1),
            jnp.stack([jnp.stack(r) for r in dlnb]).reshape(-1), jnp.stack(dcw).reshape(-1),
            jnp.stack(dcb).reshape(-1)]
    sizes = [a.size for a in flat]
    small_g = _allgather([jnp.concatenate(flat).reshape(-1, LANES)], "ag_small_grads")[0]
    total = _sum_devices(small_g, "sum_small_grads").reshape(-1)
    starts = [sum(sizes[:i]) for i in range(len(sizes))]

    def piece(i, shape):
        return total[starts[i]:starts[i] + sizes[i]].reshape(shape)

    def my_block(a):
        return lax.dynamic_slice_in_dim(a, me * DB, DB, axis=a.ndim - 1)

    g_mod_b = piece(0, (L, 6 * D))
    g_b1, g_b2 = piece(1, (L, F)), piece(2, (L, D))
    g_lng, g_lnb = my_block(piece(3, (L, 2, D))), my_block(piece(4, (L, 2, D)))
    g_cw, g_cb = my_block(piece(5, (LC, 3, D))), my_block(piece(6, (LC, D)))

    dmod_all = small_g.reshape(N_DEV, -1)[:, :sizes[0]].reshape(N_DEV, L, 6 * D)
    dmod_mine = jnp.moveaxis(lax.dynamic_slice_in_dim(dmod_all, me * NC, NC, axis=2), 0, 1)
    dmod16 = jnp.pad(dmod_mine, ((0, 0), (0, 16 - N_DEV), (0, 0)))
    tm_mod = _tile(D, 512)
    g_mod_w = _matmul(
        "grad_mod_w", "tn", (L, D // tm_mod), 1,
        cond16, pl.BlockSpec((16, tm_mod), lambda l, m: (0, m)),
        dmod16, pl.BlockSpec((None, 16, NC), lambda l, m: (l, 0, 0)),
        [_sds((L, D, NC), F32)], [pl.BlockSpec((None, tm_mod, NC), lambda l, m: (l, m, 0))], None)[0]
    res_mod_w = None
    for l in range(L):
        res_mod_w = _adamw(f"adamw_mod_w_{l}", mod_w, m_mod_w, v_mod_w, l, [(g_mod_w, l)], res_mod_w)

    small_w = [mod_b, ln_g, ln_b, conv_w, conv_b, mlp_b1, mlp_b2]
    small_m = [m_mod_b, m_ln_g, m_ln_b, m_conv_w, m_conv_b, m_mlp_b1, m_mlp_b2]
    small_v = [v_mod_b, v_ln_g, v_ln_b, v_conv_w, v_conv_b, v_mlp_b1, v_mlp_b2]
    small_gr = [g_mod_b, g_lng, g_lnb, g_cw, g_cb, g_b1, g_b2]
    n_small = sum(a.size for a in small_w)
    pad = (-n_small) % (8 * LANES)

    def packed(arrs):
        v = jnp.concatenate([a.reshape(-1) for a in arrs])
        return jnp.pad(v, (0, pad)).reshape(1, -1, LANES)

    sg, sd, sm, sv = _adamw("adamw_small", packed(small_w), packed(small_m), packed(small_v), 0,
                            [(packed(small_gr), 0)])

    def unpacked(buf):
        out, pos = [], 0
        flat_buf = buf.reshape(-1)
        for a in small_w:
            out.append(flat_buf[pos:pos + a.size].reshape(a.shape))
            pos += a.size
        return out

    small_res = [unpacked(b) for b in (sg, sd, sm, sv)]
    loss = lax.psum(loss_local, ("x", "y", "c"))

    def leaf(kind):
        s = small_res[kind]
        return [res_mod_w[kind], s[0], s[1], s[2], results["attn_in"][kind], results["attn_out"][kind],
                results["conv_in"][kind], s[3], s[4], results["conv_out"][kind], results["w1"][kind], s[5],
                results["w2"][kind], s[6]]

    return (loss, grad_x[None], *leaf(0), *leaf(1), *leaf(2), *leaf(3))
```

```python
import functools

import jax
import jax.numpy as jnp
from jax import lax
from jax.experimental import pallas as pl
from jax.experimental.pallas import tpu as pltpu

F32 = jnp.float32
BF16 = jnp.bfloat16
N_DEV = 8
HEAD_DIM = 128
LANES = 128
DEEPNORM_ALPHA = 8.0 ** 0.25
LN_EPS = 1e-5
ADAM_LR = 0.001
ADAM_B1 = 0.9
ADAM_B2 = 0.999
ADAM_EPS = 1e-08
ADAM_WD = 0.01
ADAM_STEP = 10
VMEM_LIMIT_BYTES = 52 * 1024 * 1024
MESH = pl.DeviceIdType.MESH
HBM_SPEC = pl.BlockSpec(memory_space=pltpu.HBM)
ANY_SPEC = pl.BlockSpec(memory_space=pl.ANY)
SEM_SPEC = pl.BlockSpec(memory_space=pltpu.SEMAPHORE)
EFFECT = pltpu.SideEffectType.DATAFLOW_SIDE_EFFECTING


def _pcall(body, **kw):
    return pl.pallas_call(body, **kw)


def _cp(sem=None):
    return pltpu.CompilerParams(dimension_semantics=sem, vmem_limit_bytes=VMEM_LIMIT_BYTES)


def _tile(n, pref):
    t = min(n, pref)
    assert n % t == 0, (n, pref)
    return t


def _sds(shape, dtype):
    return jax.ShapeDtypeStruct(tuple(shape), dtype)


def _allgather(xs, name, deps=()):
    n, nd = len(xs), len(deps)

    def body(*refs):
        x_refs = refs[:n]
        out_refs = refs[n + nd:2 * n + nd]
        send_sems, recv_sems, local_sems = refs[2 * n + nd:]
        x, y, c = lax.axis_index("x"), lax.axis_index("y"), lax.axis_index("c")
        me, sibling = (x, y, c), (x, y, 1 - c)
        chips = [(1 - x, y), (x, 1 - y), (1 - x, 1 - y)]

        def slot(t, px, py, pc):
            return out_refs[t].at[4 * px + 2 * py + pc]

        def copy(t, k, block, to, src=None):
            return pltpu.make_async_remote_copy(
                src_ref=slot(t, *block) if src is None else src, dst_ref=slot(t, *block),
                send_sem=send_sems.at[t, k], recv_sem=recv_sems.at[t, k],
                device_id=to, device_id_type=MESH)

        mine = [pltpu.make_async_copy(x_refs[t], slot(t, *me), local_sems.at[t]) for t in range(n)]
        for cp in mine:
            cp.start()
        first = []
        for t in range(n):
            first.append(copy(t, 0, me, sibling, src=x_refs[t]))
            for j, chip in enumerate(chips):
                first.append(copy(t, 1 + j, me, (*chip, c), src=x_refs[t]))
        for cp in first:
            cp.start()
        passed = []
        for j, chip in enumerate(chips):
            for t in range(n):
                copy(t, 1 + j, (*chip, c), me).wait_recv()
                fwd = copy(t, 4 + j, (*chip, c), sibling)
                fwd.start()
                passed.append(fwd)
        for t in range(n):
            copy(t, 0, sibling, me).wait_recv()
            for j, chip in enumerate(chips):
                copy(t, 4 + j, (*chip, 1 - c), me).wait_recv()
        for cp in first + passed:
            cp.wait_send()
        for cp in mine:
            cp.wait()

    outs = _pcall(
        body, name=name,
        out_shape=[_sds((N_DEV,) + a.shape, a.dtype) for a in xs],
        in_specs=[HBM_SPEC] * n + [ANY_SPEC] * nd, out_specs=[HBM_SPEC] * n,
        scratch_shapes=[pltpu.SemaphoreType.DMA((n, 7)), pltpu.SemaphoreType.DMA((n, 7)),
                        pltpu.SemaphoreType.DMA((n,))],
    )(*xs, *deps)
    return list(outs)


def _peers(x, y, c):
    return [(x, y, 1 - c), (1 - x, y, c), (x, 1 - y, c), (1 - x, 1 - y, c)]


def _block_owner(x, y, c, k):
    return ((1 - x) if k >> 1 else x, (1 - y) if k & 1 else y, c)


def _hbm(a):
    return pltpu.with_memory_space_constraint(a, pltpu.HBM)


def _ag_start(lands, deps, name):
    n, nd = len(lands), len(deps)

    def body(*refs):
        land_refs = refs[:n]
        send_sems, recv_sems = refs[n + nd], refs[n + nd + 1]
        token = refs[-1]
        x, y, c = lax.axis_index("x"), lax.axis_index("y"), lax.axis_index("c")
        for t in range(n):
            mine = land_refs[t].at[4 * x + 2 * y + c]
            for k, to in enumerate(_peers(x, y, c)):
                pltpu.make_async_remote_copy(
                    src_ref=mine, dst_ref=mine,
                    send_sem=send_sems.at[4 * t + k], recv_sem=recv_sems.at[4 * t + k],
                    device_id=to, device_id_type=MESH).start()
        token[...] = jnp.zeros_like(token)

    outs = _pcall(
        body, name=name,
        out_shape=(pltpu.SemaphoreType.DMA((4 * n,)), pltpu.SemaphoreType.DMA((4 * n,)),
                   *[pltpu.HBM(a.shape, a.dtype) for a in lands], _sds((8, LANES), F32)),
        in_specs=[HBM_SPEC] * n + [ANY_SPEC] * nd,
        out_specs=(SEM_SPEC, SEM_SPEC, *[HBM_SPEC] * n, pl.BlockSpec(memory_space=pltpu.VMEM)),
        input_output_aliases={i: 2 + i for i in range(n)},
        compiler_params=pltpu.CompilerParams(has_side_effects=EFFECT),
    )(*[_hbm(a) for a in lands], *deps)
    return outs[0], outs[1], list(outs[2:2 + n]), outs[-1]


def _ag_wait(started, after, name):
    send_sems, recv_sems, lands, _ = started
    n = len(lands)

    def body(*refs):
        land_refs = refs[:n]
        send_sems, recv_sems = refs[n], refs[n + 1]
        x, y, c = lax.axis_index("x"), lax.axis_index("y"), lax.axis_index("c")
        for t in range(n):
            for k, (px, py, pc) in enumerate(_peers(x, y, c)):
                cp = pltpu.make_async_remote_copy(
                    src_ref=land_refs[t].at[4 * x + 2 * y + c], dst_ref=land_refs[t].at[4 * px + 2 * py + pc],
                    send_sem=send_sems.at[4 * t + k], recv_sem=recv_sems.at[4 * t + k],
                    device_id=(px, py, pc), device_id_type=MESH)
                cp.wait_send()
                cp.wait_recv()

    outs = _pcall(
        body, name=name,
        out_shape=tuple(pltpu.HBM(a.shape, a.dtype) for a in lands),
        in_specs=[HBM_SPEC] * n + [SEM_SPEC, SEM_SPEC, ANY_SPEC],
        out_specs=[HBM_SPEC] * n,
        input_output_aliases={i: i for i in range(n)},
        compiler_params=pltpu.CompilerParams(has_side_effects=EFFECT),
    )(*lands, send_sems, recv_sems, after)
    return list(outs)


def _ag_finish(lands, name):
    n = len(lands)

    def body(*refs):
        out_refs = refs[n:2 * n]
        send_sems, recv_sems = refs[2 * n:]
        x, y, c = lax.axis_index("x"), lax.axis_index("y"), lax.axis_index("c")
        chips = [(1 - x, y), (x, 1 - y), (1 - x, 1 - y)]

        def passing(t, j, pc):
            px, py = chips[j]
            rows = out_refs[t].at[4 * px + 2 * py + pc]
            return pltpu.make_async_remote_copy(
                src_ref=rows, dst_ref=rows, send_sem=send_sems.at[t, j], recv_sem=recv_sems.at[t, j],
                device_id=(x, y, 1 - c), device_id_type=MESH)

        sends = [passing(t, j, c) for t in range(n) for j in range(3)]
        for cp in sends:
            cp.start()
        for t in range(n):
            for j in range(3):
                passing(t, j, 1 - c).wait_recv()
        for cp in sends:
            cp.wait_send()

    outs = _pcall(
        body, name=name,
        out_shape=[_sds(a.shape, a.dtype) for a in lands],
        in_specs=[HBM_SPEC] * n, out_specs=[HBM_SPEC] * n,
        input_output_aliases={i: i for i in range(n)},
        scratch_shapes=[pltpu.SemaphoreType.DMA((n, 3)), pltpu.SemaphoreType.DMA((n, 3))],
    )(*lands)
    return list(outs)


def _rs_chips_start(ps, name):
    n = len(ps)

    def body(*refs):
        p_refs, land_refs = refs[:n], refs[n:2 * n]
        send_sems, recv_sems = refs[2 * n], refs[2 * n + 1]
        token = refs[-1]
        x, y, c = lax.axis_index("x"), lax.axis_index("y"), lax.axis_index("c")
        for t in range(n):
            for k in range(1, 4):
                pltpu.make_async_remote_copy(
                    src_ref=p_refs[t].at[k], dst_ref=land_refs[t].at[k],
                    send_sem=send_sems.at[3 * t + k - 1], recv_sem=recv_sems.at[3 * t + k - 1],
                    device_id=_block_owner(x, y, c, k), device_id_type=MESH).start()
        token[...] = jnp.zeros_like(token)

    lands = [lax.empty(p.shape, p.dtype) for p in ps]
    outs = _pcall(
        body, name=name,
        out_shape=(pltpu.SemaphoreType.DMA((3 * n,)), pltpu.SemaphoreType.DMA((3 * n,)),
                   *[pltpu.HBM(a.shape, a.dtype) for a in ps], *[pltpu.HBM(a.shape, a.dtype) for a in lands],
                   _sds((8, LANES), F32)),
        in_specs=[HBM_SPEC] * (2 * n),
        out_specs=(SEM_SPEC, SEM_SPEC, *[HBM_SPEC] * (2 * n), pl.BlockSpec(memory_space=pltpu.VMEM)),
        input_output_aliases={i: 2 + i for i in range(2 * n)},
        compiler_params=pltpu.CompilerParams(has_side_effects=EFFECT),
    )(*[_hbm(a) for a in ps], *[_hbm(a) for a in lands])
    return outs[0], outs[1], list(outs[2:2 + n]), list(outs[2 + n:2 + 2 * n]), outs[-1]


def _rs_chips_wait(started, after, name):
    send_sems, recv_sems, ps, lands, _ = started
    n = len(ps)

    def body(*refs):
        p_refs, land_refs = refs[:n], refs[n:2 * n]
        send_sems, recv_sems = refs[2 * n], refs[2 * n + 1]
        x, y, c = lax.axis_index("x"), lax.axis_index("y"), lax.axis_index("c")
        for t in range(n):
            for k in range(1, 4):
                cp = pltpu.make_async_remote_copy(
                    src_ref=p_refs[t].at[k], dst_ref=land_refs[t].at[k],
                    send_sem=send_sems.at[3 * t + k - 1], recv_sem=recv_sems.at[3 * t + k - 1],
                    device_id=_block_owner(x, y, c, k), device_id_type=MESH)
                cp.wait_send()
                cp.wait_recv()

    outs = _pcall(
        body, name=name,
        out_shape=(*[pltpu.HBM(a.shape, a.dtype) for a in ps], *[pltpu.HBM(a.shape, a.dtype) for a in lands]),
        in_specs=[HBM_SPEC] * (2 * n) + [SEM_SPEC, SEM_SPEC, ANY_SPEC],
        out_specs=[HBM_SPEC] * (2 * n),
        input_output_aliases={i: i for i in range(2 * n)},
        compiler_params=pltpu.CompilerParams(has_side_effects=EFFECT),
    )(*ps, *lands, send_sems, recv_sems, after)
    return list(outs[:n]), list(outs[n:])


def _rs_sibling_copies(g_refs, land_refs, send_sems, recv_sems):
    x, y, c = lax.axis_index("x"), lax.axis_index("y"), lax.axis_index("c")
    me = 4 * x + 2 * y + c
    return [pltpu.make_async_remote_copy(
        src_ref=g_refs[t].at[jnp.bitwise_xor(2 * k + 1, me)], dst_ref=land_refs[t].at[k],
        send_sem=send_sems.at[4 * t + k], recv_sem=recv_sems.at[4 * t + k],
        device_id=(x, y, 1 - c), device_id_type=MESH) for t in range(len(g_refs)) for k in range(4)]


def _rs_sibling_start(gs, name):
    n = len(gs)

    def body(*refs):
        for cp in _rs_sibling_copies(refs[:n], refs[n:2 * n], refs[2 * n], refs[2 * n + 1]):
            cp.start()
        refs[-1][...] = jnp.zeros_like(refs[-1])

    lands = [lax.empty((4,) + g.shape[1:], g.dtype) for g in gs]
    outs = _pcall(
        body, name=name,
        out_shape=(pltpu.SemaphoreType.DMA((4 * n,)), pltpu.SemaphoreType.DMA((4 * n,)),
                   *[pltpu.HBM(a.shape, a.dtype) for a in gs], *[pltpu.HBM(a.shape, a.dtype) for a in lands],
                   _sds((8, LANES), F32)),
        in_specs=[HBM_SPEC] * (2 * n),
        out_specs=(SEM_SPEC, SEM_SPEC, *[HBM_SPEC] * (2 * n), pl.BlockSpec(memory_space=pltpu.VMEM)),
        input_output_aliases={i: 2 + i for i in range(2 * n)},
        compiler_params=pltpu.CompilerParams(has_side_effects=EFFECT),
    )(*[_hbm(a) for a in gs], *[_hbm(a) for a in lands])
    return outs[0], outs[1], list(outs[2:2 + n]), list(outs[2 + n:2 + 2 * n]), outs[-1]


def _rs_sibling_wait(started, after, name):
    send_sems, recv_sems, gs, lands, _ = started
    n = len(gs)

    def body(*refs):
        for cp in _rs_sibling_copies(refs[:n], refs[n:2 * n], refs[2 * n], refs[2 * n + 1]):
            cp.wait_send()
            cp.wait_recv()

    outs = _pcall(
        body, name=name,
        out_shape=(*[pltpu.HBM(a.shape, a.dtype) for a in gs], *[pltpu.HBM(a.shape, a.dtype) for a in lands]),
        in_specs=[HBM_SPEC] * (2 * n) + [SEM_SPEC, SEM_SPEC, ANY_SPEC],
        out_specs=[HBM_SPEC] * (2 * n),
        input_output_aliases={i: i for i in range(2 * n)},
        compiler_params=pltpu.CompilerParams(has_side_effects=EFFECT),
    )(*gs, *lands, send_sems, recv_sems, after)
    return list(outs[:n]), list(outs[n:])


def _rs_add(g, r1, me_arr, name):
    _, R, C = g.shape
    tr = _tile(R, 256)

    def body(me_ref, g_ref, r_ref, o_ref):
        o_ref[...] = (g_ref[...].astype(F32) + r_ref[...].astype(F32)).astype(BF16)

    grid_spec = pltpu.PrefetchScalarGridSpec(
        num_scalar_prefetch=1, grid=(4, R // tr),
        in_specs=[pl.BlockSpec((None, tr, C), lambda k, r, me: (jnp.bitwise_xor(2 * k, me[0]), r, 0)),
                  pl.BlockSpec((None, tr, C), lambda k, r, me: (k, r, 0))],
        out_specs=pl.BlockSpec((None, tr, C), lambda k, r, me: (k, r, 0)))
    return _pcall(body, name=name, grid_spec=grid_spec, out_shape=_sds((4, R, C), BF16),
                  compiler_params=_cp(("parallel", "parallel")))(me_arr, g, r1)


_DIMS = {"nn": (((1,), (0,)), ((), ())), "nt": (((1,), (1,)), ((), ())), "tn": (((0,), (0,)), ((), ()))}


def _dot(a, b, mode="nn"):
    return lax.dot_general(a, b, _DIMS[mode], preferred_element_type=F32)


def _matmul(name, mode, grid, nk, a, a_spec, b, b_spec, out_shapes, out_specs, acc_shape,
            epilogue="plain", extras=(), extra_specs=(), sem=None, widen=None, deps=()):
    n_extra, n_out = len(extras) + len(deps), len(out_shapes)
    extras, extra_specs = list(extras) + list(deps), list(extra_specs) + [ANY_SPEC] * len(deps)

    def body(*refs):
        a_ref, b_ref = refs[0], refs[1]
        ex = refs[2:2 + n_extra - len(deps)]
        outs = refs[2 + n_extra:2 + n_extra + n_out]
        a_tile, b_tile = a_ref[...].astype(BF16), b_ref[...].astype(BF16)
        if widen == "a":
            a_tile = jnp.concatenate([a_tile] + [e[...].astype(BF16) for e in ex], axis=1)
        elif widen == "b":
            b_tile = jnp.concatenate([b_tile] + [e[...].astype(BF16) for e in ex], axis=1)
        part = _dot(a_tile, b_tile, mode)

        def finish(acc):
            if epilogue == "plain":
                if len(ex) and widen is None:
                    acc = acc + ex[0][...]
                outs[0][...] = acc.astype(outs[0].dtype)
            elif epilogue == "relu2":
                u = acc + ex[0][...]
                outs[0][...] = u.astype(outs[0].dtype)
                outs[1][...] = jnp.square(jnp.maximum(u, 0.0)).astype(outs[1].dtype)
            else:
                du = acc * (2.0 * jnp.maximum(ex[0][...].astype(F32), 0.0))
                outs[0][...] = du.astype(outs[0].dtype)
                col = jnp.sum(du, axis=0, keepdims=True)
                inner = pl.program_id(1)

                @pl.when(inner == 0)
                def _():
                    outs[1][...] = col

                @pl.when(inner > 0)
                def _():
                    outs[1][...] += col

        if nk == 1:
            finish(part)
        else:
            acc_ref = refs[2 + n_extra + n_out]
            kidx = pl.program_id(len(grid) - 1)

            @pl.when(kidx == 0)
            def _():
                acc_ref[...] = part

            @pl.when(kidx > 0)
            def _():
                acc_ref[...] += part

            @pl.when(kidx == nk - 1)
            def _():
                finish(acc_ref[...])

    if sem is None:
        sem = ("parallel",) * (len(grid) - (1 if nk > 1 else 0)) + (("arbitrary",) if nk > 1 else ())
    outs = _pcall(
        body, name=name, grid=grid,
        in_specs=[a_spec, b_spec] + list(extra_specs), out_specs=list(out_specs),
        out_shape=list(out_shapes),
        scratch_shapes=[pltpu.VMEM(acc_shape, F32)] if nk > 1 else [],
        compiler_params=_cp(sem),
    )(a, b, *extras)
    return list(outs)


def _cast_bf16(w3, l, me_arr, name, deps=()):
    _, R, C = w3.shape
    tr = _tile(R, 256)

    def body(me_ref, x_ref, *rest):
        rest[-1][...] = x_ref[...].astype(BF16)

    grid_spec = pltpu.PrefetchScalarGridSpec(
        num_scalar_prefetch=1, grid=(R // tr,),
        in_specs=[pl.BlockSpec((None, tr, C), lambda r, me: (l, r, 0))] + [ANY_SPEC] * len(deps),
        out_specs=pl.BlockSpec((None, tr, C), lambda r, me: (me[0], r, 0)))
    return _pcall(body, name=name, grid_spec=grid_spec, out_shape=_sds((N_DEV, R, C), BF16),
                  compiler_params=_cp(("parallel",)))(me_arr, w3, *deps)


def _unblock(wb, name):
    n, R, C = wb.shape

    def body(x_ref, o_ref):
        o_ref[...] = x_ref[...]

    return _pcall(body, name=name, grid=(n,),
                  in_specs=[pl.BlockSpec((None, R, C), lambda j: (j, 0, 0))],
                  out_specs=pl.BlockSpec((R, C), lambda j: (0, j)),
                  out_shape=_sds((R, n * C), wb.dtype), compiler_params=_cp(("parallel",)))(wb)


def _silu_bf16(c16, name):
    def body(c_ref, o_ref):
        v = c_ref[...]
        o_ref[...] = (v / (1.0 + jnp.exp(-v))).astype(BF16)

    return _pcall(body, name=name, out_shape=_sds(c16.shape, BF16))(c16)


def _sum_devices(g, name):
    def body(g_ref, o_ref):
        acc = g_ref[0]
        for d in range(1, N_DEV):
            acc = acc + g_ref[d]
        o_ref[...] = acc

    return _pcall(body, name=name, out_shape=_sds(g.shape[1:], F32))(g)


def _modulate(name, x, sc, sh, deps=()):
    S, D = x.shape
    ts = _tile(S, 256)

    def body(x_ref, sc_ref, sh_ref, *rest):
        rest[-1][...] = (x_ref[...] * (1.0 + sc_ref[...]) + sh_ref[...]).astype(BF16)

    row = pl.BlockSpec((ts, D), lambda i: (i, 0))
    vec = pl.BlockSpec((1, D), lambda i: (0, 0))
    return _pcall(body, name=name, grid=(S // ts,), in_specs=[row, vec, vec] + [ANY_SPEC] * len(deps),
                  out_specs=row, out_shape=_sds((S, D), BF16),
                  compiler_params=_cp(("parallel",)))(x, sc, sh, *deps)


def _ln_stats(r):
    mu = jnp.mean(r, axis=-1, keepdims=True)
    xc = r - mu
    var = jnp.mean(xc * xc, axis=-1, keepdims=True)
    rstd = lax.rsqrt(var + LN_EPS)
    return xc * rstd, rstd


def _ln_fwd(name, xin, y, gate, lng, lnb, sc, sh):
    S, D = xin.shape
    ts = _tile(S, 256)

    def body(x_ref, y_ref, g_ref, lg_ref, lb_ref, sc_ref, sh_ref, xo_ref, h_ref):
        r = DEEPNORM_ALPHA * x_ref[...] + (1.0 + g_ref[...]) * y_ref[...]
        xhat, _ = _ln_stats(r)
        xo = xhat * lg_ref[...] + lb_ref[...]
        xo_ref[...] = xo
        h_ref[...] = (xo * (1.0 + sc_ref[...]) + sh_ref[...]).astype(BF16)

    row = pl.BlockSpec((ts, D), lambda i: (i, 0))
    vec = pl.BlockSpec((1, D), lambda i: (0, 0))
    return _pcall(body, name=name, grid=(S // ts,), in_specs=[row, row] + [vec] * 5,
                  out_specs=[row, row], out_shape=[_sds((S, D), F32), _sds((S, D), BF16)],
                  compiler_params=_cp(("parallel",)))(xin, y, gate, lng, lnb, sc, sh)


def _boundary(name, mode, big, vecs, deps=()):
    S, D = big[0].shape
    ts = _tile(S, 128)
    nb, nv, nd = len(big), len(vecs), len(deps)

    def body(*refs):
        b = [r[...] for r in refs[:nb]]
        v = [r[...] for r in refs[nb:nb + nv]]
        outs = refs[nb + nv + nd:]
        vec_ref = outs[-1]
        i = pl.program_id(0)
        rows = [None] * 8

        def colsum(a):
            return jnp.sum(a, axis=0, keepdims=True)

        if mode == "first":
            dxr, dh, xk = b
            sc = v[0]
            outs[0][...] = dxr + dh * (1.0 + sc)
            rows[0], rows[1] = colsum(dh * xk), colsum(dh)
        else:
            if mode == "mid":
                dxr, dh, xprev, yk = b
                sc, gate, lng, lnb = v
            else:
                target, xprev, yk = b
                gate, lng, lnb = v
            r = DEEPNORM_ALPHA * xprev + (1.0 + gate) * yk
            xhat, rstd = _ln_stats(r)
            if mode == "mid":
                xk = xhat * lng + lnb
                d = dxr + dh * (1.0 + sc)
                rows[0], rows[1] = colsum(dh * xk), colsum(dh)
            else:
                err = xhat * lng + lnb - target
                d = err * (1.0 / D)
                rows[6] = jnp.zeros((1, D), F32) + jnp.sum(err * err)
            rows[2], rows[3] = colsum(d * xhat), colsum(d)
            dxhat = d * lng
            m1 = jnp.mean(dxhat, axis=-1, keepdims=True)
            m2 = jnp.mean(dxhat * xhat, axis=-1, keepdims=True)
            dr = rstd * (dxhat - m1 - xhat * m2)
            dy = (1.0 + gate) * dr
            rows[4], rows[5] = colsum(dr * yk), colsum(dy)
            outs[0][...] = dy.astype(BF16)
            outs[1][...] = DEEPNORM_ALPHA * dr
        @pl.when(i == 0)
        def _():
            vec_ref[...] = jnp.zeros_like(vec_ref)

        for j, row_j in enumerate(rows):
            if row_j is not None:
                vec_ref[j:j + 1, :] += row_j

    row = pl.BlockSpec((ts, D), lambda i: (i, 0))
    vec = pl.BlockSpec((1, D), lambda i: (0, 0))
    vec_out = pl.BlockSpec((8, D), lambda i: (0, 0))
    if mode == "first":
        out_specs, out_shape = [row, vec_out], [_sds((S, D), F32), _sds((8, D), F32)]
    else:
        out_specs = [row, row, vec_out]
        out_shape = [_sds((S, D), BF16), _sds((S, D), F32), _sds((8, D), F32)]
    return _pcall(body, name=name, grid=(S // ts,), in_specs=[row] * nb + [vec] * nv + [ANY_SPEC] * nd,
                  out_specs=out_specs, out_shape=out_shape,
                  compiler_params=_cp(("arbitrary",)))(*big, *vecs, *deps)


def _split_bf16(a):
    hi = a.astype(BF16)
    lo = (a - hi.astype(F32)).astype(BF16)
    return hi, lo


def _cumsum_matrix(tri):
    half = jnp.concatenate([tri.astype(BF16), jnp.ones((LANES, LANES), BF16)], axis=1)
    return jnp.concatenate([half, half], axis=0)


def _chunk_sums(a, mat):
    hi, lo = _split_bf16(a)
    both = _dot(jnp.concatenate([hi, lo], axis=1), mat)
    return both[:, :LANES], both[:, LANES:]


LOG2_E = 1.4426950408889634


def _log2_1m_beta(z2):
    return -(jnp.maximum(z2, 0.0) + jnp.log2(1.0 + jnp.exp2(-jnp.abs(z2))))


def _attn_fwd(name, qkv3):
    _, S, D = qkv3.shape
    H = D // HEAD_DIM
    tq = _tile(S, 512)
    nch = tq // LANES
    scale = HEAD_DIM ** -0.5

    def body(q_ref, k_ref, v_ref, o_ref, t_ref):
        i = pl.program_id(1)
        q = q_ref[...]
        jj = lax.broadcasted_iota(jnp.int32, (LANES, LANES), 0)
        ss = lax.broadcasted_iota(jnp.int32, (LANES, LANES), 1)
        from_here = _cumsum_matrix(jj >= ss)
        causal = (lax.broadcasted_iota(jnp.int32, (tq, tq), 1) < lax.broadcasted_iota(jnp.int32, (tq, tq), 0))

        def group(k0, masked, tail, acc):
            kb = k_ref[pl.ds(k0, tq), :]
            vb = v_ref[pl.ds(k0, tq), :]
            z = _dot(q, kb, "nt") * (scale * LOG2_E)
            lm = _log2_1m_beta(z)
            if masked:
                lm = jnp.where(causal, lm, 0.0)
            args = [None] * nch
            for ch in reversed(range(nch)):
                sl = slice(ch * LANES, (ch + 1) * LANES)
                cum, tot = _chunk_sums(lm[:, sl], from_here)
                args[ch] = z[:, sl] + (cum + tail)
                tail = tail + tot
            a = jnp.exp2(jnp.concatenate(args, axis=1))
            if masked:
                a = jnp.where(causal, a, 0.0)
            return tail, acc + _dot(a.astype(BF16), vb)

        zero = jnp.zeros((tq, HEAD_DIM), F32)
        carry = group(pl.multiple_of(i * tq, tq), True, zero, zero)
        def at(g):
            return pl.multiple_of(g * tq, tq)

        carry = lax.fori_loop(
            0, i // 2, lambda p, cr: group(at(i - 2 - 2 * p), False, *group(at(i - 1 - 2 * p), False, *cr)), carry)
        tail, acc = lax.fori_loop(0, i % 2, lambda _, cr: group(0, False, *cr), carry)
        o_ref[...] = acc.astype(BF16)
        t_ref[...] = tail

    return _pcall(
        body, name=name, grid=(H, S // tq),
        in_specs=[pl.BlockSpec((None, tq, HEAD_DIM), lambda h, i: (0, i, h)),
                  pl.BlockSpec((None, S, HEAD_DIM), lambda h, i: (1, 0, h)),
                  pl.BlockSpec((None, S, HEAD_DIM), lambda h, i: (2, 0, h))],
        out_specs=[pl.BlockSpec((tq, HEAD_DIM), lambda h, i: (i, h)),
                   pl.BlockSpec((tq, HEAD_DIM), lambda h, i: (i, h))],
        out_shape=[_sds((S, D), BF16), _sds((S, D), F32)],
        compiler_params=_cp(("parallel", "parallel")),
    )(qkv3, qkv3, qkv3)


def _attn_bwd(name, qkv3, do, totals):
    _, S, D = qkv3.shape
    H = D // HEAD_DIM
    tq = _tile(S, 512)
    nch = tq // LANES
    nq = S // tq
    scale = HEAD_DIM ** -0.5

    def body(q_ref, k_ref, v_ref, do_ref, t_ref, out_ref, dk_acc, dv_acc):
        i = pl.program_id(1)

        @pl.when(i == 0)
        def _():
            dk_acc[...] = jnp.zeros_like(dk_acc)
            dv_acc[...] = jnp.zeros_like(dv_acc)

        q = q_ref[...]
        do_b = do_ref[...]
        total = t_ref[...]
        jj = lax.broadcasted_iota(jnp.int32, (LANES, LANES), 0)
        ss = lax.broadcasted_iota(jnp.int32, (LANES, LANES), 1)
        before = _cumsum_matrix(jj < ss)
        causal = (lax.broadcasted_iota(jnp.int32, (tq, tq), 1) < lax.broadcasted_iota(jnp.int32, (tq, tq), 0))

        def group(k0, masked, head, dsum, dq):
            kb = k_ref[pl.ds(k0, tq), :]
            vb = v_ref[pl.ds(k0, tq), :]
            z = _dot(q, kb, "nt") * (scale * LOG2_E)
            lm = _log2_1m_beta(z)
            sig = jnp.exp2(z + lm)
            if masked:
                lm = jnp.where(causal, lm, 0.0)
            args = []
            for ch in range(nch):
                sl = slice(ch * LANES, (ch + 1) * LANES)
                cum, tot = _chunk_sums(lm[:, sl], before)
                args.append(z[:, sl] + (total - head - cum))
                head = head + tot
            a = jnp.exp2(jnp.concatenate(args, axis=1))
            if masked:
                a = jnp.where(causal, a, 0.0)
            de = _dot(do_b, vb, "nt") * a
            dv_acc[pl.ds(k0, tq), :] += _dot(a.astype(BF16), do_b, "tn")
            sums = []
            for ch in range(nch):
                sl = slice(ch * LANES, (ch + 1) * LANES)
                cum, tot = _chunk_sums(de[:, sl], before)
                sums.append(dsum + cum)
                dsum = dsum + tot
            dz = de - sig * (de + jnp.concatenate(sums, axis=1))
            if masked:
                dz = jnp.where(causal, dz, 0.0)
            dzb = dz.astype(BF16)
            dk_acc[pl.ds(k0, tq), :] += _dot(dzb, q, "tn")
            return head, dsum, dq + _dot(dzb, kb)

        zero = jnp.zeros((tq, HEAD_DIM), F32)
        def at(g):
            return pl.multiple_of(g * tq, tq)

        carry = lax.fori_loop(
            0, i // 2, lambda p, cr: group(at(2 * p + 1), False, *group(at(2 * p), False, *cr)), (zero, zero, zero))
        carry = lax.fori_loop(0, i % 2, lambda _, cr: group(at(i - 1), False, *cr), carry)
        q0 = pl.multiple_of(i * tq, tq)
        _, _, dq = group(q0, True, *carry)
        out_ref[0, pl.ds(q0, tq), :] = (dq * scale).astype(BF16)

        @pl.when(i == nq - 1)
        def _():
            out_ref[1] = (dk_acc[...] * scale).astype(BF16)
            out_ref[2] = dv_acc[...].astype(BF16)

    return _pcall(
        body, name=name, grid=(H, nq),
        in_specs=[pl.BlockSpec((None, tq, HEAD_DIM), lambda h, i: (0, i, h)),
                  pl.BlockSpec((None, S, HEAD_DIM), lambda h, i: (1, 0, h)),
                  pl.BlockSpec((None, S, HEAD_DIM), lambda h, i: (2, 0, h)),
                  pl.BlockSpec((tq, HEAD_DIM), lambda h, i: (i, h)),
                  pl.BlockSpec((tq, HEAD_DIM), lambda h, i: (i, h))],
        out_specs=pl.BlockSpec((3, S, HEAD_DIM), lambda h, i: (0, 0, h)),
        out_shape=_sds((3, S, D), BF16),
        scratch_shapes=[pltpu.VMEM((S, HEAD_DIM), F32), pltpu.VMEM((S, HEAD_DIM), F32)],
        compiler_params=_cp(("parallel", "arbitrary")),
    )(qkv3, qkv3, qkv3, do, totals)


def _conv_taps(g, t):
    S = g.shape[0]
    g1 = jnp.where(t >= 1, pltpu.roll(g, 1, 0), 0.0)
    g2 = jnp.where(t >= 2, pltpu.roll(g, 2, 0), 0.0)
    return g1, g2


def _conv_fwd(name, bcu3, cw, cb):
    _, S, D = bcu3.shape
    tf = LANES

    def body(x_ref, w_ref, b_ref, p_ref):
        t = lax.broadcasted_iota(jnp.int32, (S, tf), 0)
        g = x_ref[1] * x_ref[2]
        g1, g2 = _conv_taps(g, t)
        y = w_ref[0:1, :] * g2 + w_ref[1:2, :] * g1 + w_ref[2:3, :] * g + b_ref[...]
        p_ref[...] = (x_ref[0] * y).astype(BF16)

    return _pcall(
        body, name=name, grid=(D // tf,),
        in_specs=[pl.BlockSpec((3, S, tf), lambda f: (0, 0, f)), pl.BlockSpec((3, tf), lambda f: (0, f)),
                  pl.BlockSpec((1, tf), lambda f: (0, f))],
        out_specs=pl.BlockSpec((S, tf), lambda f: (0, f)), out_shape=_sds((S, D), BF16),
        compiler_params=_cp(("parallel",)),
    )(bcu3, cw, cb)


def _conv_bwd(name, bcu3, dp, cw, cb):
    _, S, D = bcu3.shape
    tf = LANES

    def body(x_ref, dp_ref, w_ref, b_ref, o_ref, vec_ref):
        t = lax.broadcasted_iota(jnp.int32, (S, tf), 0)
        bg, cg, u = x_ref[0], x_ref[1], x_ref[2]
        g = cg * u
        g1, g2 = _conv_taps(g, t)
        w0, w1, w2 = w_ref[0:1, :], w_ref[1:2, :], w_ref[2:3, :]
        y = w0 * g2 + w1 * g1 + w2 * g + b_ref[...]
        dpv = dp_ref[...]
        o_ref[0] = (dpv * y).astype(BF16)
        dy = dpv * bg

        def colsum(a):
            return jnp.sum(a, axis=0, keepdims=True)

        vec_ref[...] = jnp.zeros_like(vec_ref)
        for j, row_j in enumerate((dy, dy * g2, dy * g1, dy * g)):
            vec_ref[j:j + 1, :] = colsum(row_j)
        dy1 = jnp.where(t < S - 1, pltpu.roll(dy, S - 1, 0), 0.0)
        dy2 = jnp.where(t < S - 2, pltpu.roll(dy, S - 2, 0), 0.0)
        dg = w2 * dy + w1 * dy1 + w0 * dy2
        o_ref[1] = (dg * u).astype(BF16)
        o_ref[2] = (dg * cg).astype(BF16)

    return _pcall(
        body, name=name, grid=(D // tf,),
        in_specs=[pl.BlockSpec((3, S, tf), lambda f: (0, 0, f)), pl.BlockSpec((S, tf), lambda f: (0, f)),
                  pl.BlockSpec((3, tf), lambda f: (0, f)), pl.BlockSpec((1, tf), lambda f: (0, f))],
        out_specs=[pl.BlockSpec((3, S, tf), lambda f: (0, 0, f)), pl.BlockSpec((8, tf), lambda f: (0, f))],
        out_shape=[_sds((3, S, D), BF16), _sds((8, D), F32)],
        compiler_params=_cp(("parallel",)),
    )(bcu3, dp, cw, cb)


def _adamw(name, w3, m3, v3, l, parts, carry=None):
    L, R, C = w3.shape
    budget_rows = (1 << 18) // C
    tr = R if R <= budget_rows else _tile(R, 1 << (budget_rows.bit_length() - 1))
    n_parts = len(parts)
    n_carry = 0 if carry is None else 4
    c1 = 1.0 / (1.0 - ADAM_B1 ** ADAM_STEP)
    c2 = 1.0 / (1.0 - ADAM_B2 ** ADAM_STEP)

    def body(*refs):
        w_ref, m_ref, v_ref = refs[:3]
        p_refs = refs[3:3 + n_parts]
        og, od, om, ov = refs[3 + n_parts + n_carry:]
        g = p_refs[0][...].astype(F32)
        for p in p_refs[1:]:
            g = g + p[...].astype(F32)
        m = ADAM_B1 * m_ref[...] + (1.0 - ADAM_B1) * g
        v = ADAM_B2 * v_ref[...] + (1.0 - ADAM_B2) * jnp.square(g)
        og[...] = g
        om[...] = m
        ov[...] = v
        od[...] = -ADAM_LR * ((m * c1) / (jnp.sqrt(v * c2) + ADAM_EPS) + ADAM_WD * w_ref[...])

    lay = pl.BlockSpec((None, tr, C), lambda r: (l, r, 0))
    in_specs = [lay, lay, lay]
    args = [w3, m3, v3]
    for arr, idx in parts:
        in_specs.append(pl.BlockSpec((None, tr, C), functools.partial(lambda r, j: (j, r, 0), j=idx)))
        args.append(arr)
    aliases = {}
    if carry is not None:
        for j, buf in enumerate(carry):
            aliases[len(args)] = j
            in_specs.append(ANY_SPEC)
            args.append(buf)
    return _pcall(body, name=name, grid=(R // tr,), in_specs=in_specs, out_specs=[lay] * 4,
                  out_shape=[_sds((L, R, C), F32)] * 4, input_output_aliases=aliases,
                  compiler_params=_cp(("parallel",)))(*args)


def kernel(x, c, mod_w, mod_b, ln_g, ln_b, attn_w_qkv, attn_w_o, conv_w_in, conv_w, conv_b, conv_w_out, mlp_w1, mlp_b1, mlp_w2, mlp_b2, loss_target, m_mod_w, m_mod_b, m_ln_g, m_ln_b, m_attn_w_qkv, m_attn_w_o, m_conv_w_in, m_conv_w, m_conv_b, m_conv_w_out, m_mlp_w1, m_mlp_b1, m_mlp_w2, m_mlp_b2, v_mod_w, v_mod_b, v_ln_g, v_ln_b, v_attn_w_qkv, v_attn_w_o, v_conv_w_in, v_conv_w, v_conv_b, v_conv_w_out, v_mlp_w1, v_mlp_b1, v_mlp_w2, v_mlp_b2):
    xs, target = x[0], loss_target[0]
    S, D = xs.shape
    L = mod_w.shape[0]
    LA, LC = attn_w_qkv.shape[0], conv_w_in.shape[0]
    F = mlp_b1.shape[1]
    DB, FB, NC = D // N_DEV, F // N_DEV, mod_w.shape[2]
    CT = DB
    me = 4 * lax.axis_index("x") + 2 * lax.axis_index("y") + lax.axis_index("c")
    me_arr = jnp.reshape(me, (1,)).astype(jnp.int32)

    K = 2 * L

    def mixer_weights(l):
        if l % 2 == 0:
            return attn_w_qkv, attn_w_o, l // 2
        return conv_w_in, conv_w_out, l // 2

    def group_shards(k, deps):
        l = k // 2
        if k % 2 == 0:
            w_in3, w_out3, j = mixer_weights(l)
            return [_cast_bf16(w_in3, j, me_arr, f"cast_win_{l}", deps), _cast_bf16(w_out3, j, me_arr, f"cast_wout_{l}", deps)]
        return [_cast_bf16(mlp_w1, l, me_arr, f"cast_w1_{l}", deps), _cast_bf16(mlp_w2, l, me_arr, f"cast_w2_{l}", deps)]

    first_shards = group_shards(0, [])

    small = [c, ln_g, ln_b, conv_w, conv_b]
    pack = jnp.concatenate([a.reshape(-1) for a in small]).reshape(-1, LANES)
    gath = _allgather([pack], "ag_small_params")[0].reshape(N_DEV, -1)
    offs, pos = [], 0
    for a in small:
        offs.append((pos, a.size))
        pos += a.size

    def unshard(idx, lead):
        o, n = offs[idx]
        a = gath[:, o:o + n].reshape((N_DEV,) + lead + (DB,))
        return jnp.moveaxis(a, 0, -2).reshape(lead + (D,))

    c_all = gath[:, :D]
    lng_full, lnb_full = unshard(1, (L, 2)), unshard(2, (L, 2))
    cw_full, cb_full = unshard(3, (LC, 3)), unshard(4, (LC,))

    cond16 = _silu_bf16(jnp.pad(c_all, ((0, 16 - N_DEV), (0, 0))), "silu_cond")
    tn_mod = NC // 3
    bias_mod = lax.dynamic_slice_in_dim(mod_b, me * NC, NC, axis=1).reshape(L, 1, NC)
    mod_part = _matmul(
        "mod_fwd", "nn", (L, NC // tn_mod), 1,
        cond16, pl.BlockSpec((16, D), lambda l, n: (0, 0)),
        mod_w, pl.BlockSpec((None, D, tn_mod), lambda l, n: (l, 0, n)),
        [_sds((L, 16, NC), F32)], [pl.BlockSpec((None, 16, tn_mod), lambda l, n: (l, 0, n))], None,
        extras=[bias_mod], extra_specs=[pl.BlockSpec((None, 1, tn_mod), lambda l, n: (l, 0, n))])[0]
    mod_gath = _allgather([mod_part[:, :N_DEV].reshape(-1, LANES)], "ag_mod")[0].reshape(N_DEV, L, N_DEV, NC)
    mod_me = lax.dynamic_index_in_dim(mod_gath, me, axis=2, keepdims=False)
    mod_me = jnp.moveaxis(mod_me, 0, 1).reshape(L, 6, 1, D)

    def mod_vecs(k):
        l, o = k // 2, 3 * (k % 2)
        return mod_me[l, o], mod_me[l, o + 1], mod_me[l, o + 2]

    wg = [[None] * 4 for _ in range(L)]
    first_in = _ag_start(first_shards[:1], [mod_gath], "ag_start_0")
    first_out = _ag_start(first_shards[1:], [first_in[3]], "ag_start_0_out")
    shards = {k: group_shards(k, [first_in[3]]) for k in range(1, K)}
    in_flight = {1: _ag_start(shards.pop(1), [first_out[3]], "ag_start_1")}

    def fetch_group(k, after):
        if k == 0:
            (first,) = _ag_finish(_ag_wait(first_in, after, "ag_wait_0"), "ag_finish_0")
            wg[0][0] = first
        else:
            first, second = _ag_finish(_ag_wait(in_flight.pop(k), after, f"ag_wait_{k}"), f"ag_finish_{k}")
            rows = D if k % 2 == 0 else F
            wg[k // 2][2 * (k % 2)], wg[k // 2][2 * (k % 2) + 1] = first, second.reshape(rows, D)
        if k + 2 < K:
            in_flight[k + 2] = _ag_start(shards.pop(k + 2), [first], f"ag_start_{k + 2}")

    def fetch_first_out(after):
        (w_out,) = _ag_finish(_ag_wait(first_out, after, "ag_wait_0_out"), "ag_finish_0_out")
        wg[0][1] = w_out.reshape(D, D)

    n_ct = 3 * D // CT
    tm_big = _tile(S, 2048)

    def in_proj(l, h, out_dtype, deps):
        return _matmul(
            f"in_proj_{l}", "nn", (S // tm_big, n_ct), 1,
            h, pl.BlockSpec((tm_big, D), lambda i, ct: (i, 0)),
            wg[l][0], pl.BlockSpec((None, D, CT), lambda i, ct: (ct // 3, 0, ct % 3)),
            [_sds((3, S, D), out_dtype)],
            [pl.BlockSpec((None, tm_big, CT), lambda i, ct: (ct // N_DEV, i, ct % N_DEV))], None, deps=deps)[0]

    tm, tn_half = _tile(S, 512), _tile(D, 1024)
    tm_up = _tile(S, 1024)

    def out_proj(l, a):
        return _matmul(
            f"out_proj_{l}", "nn", (D // tn_half, S // tm), 1,
            a, pl.BlockSpec((tm, D), lambda n, i: (i, 0)),
            wg[l][1], pl.BlockSpec((D, tn_half), lambda n, i: (0, n)),
            [_sds((S, D), F32)], [pl.BlockSpec((tm, tn_half), lambda n, i: (i, n))], None)[0]

    def mlp_up(l, h, deps):
        blk = pl.BlockSpec((tm_up, FB), lambda j, i: (i, j))
        return _matmul(
            f"mlp_up_{l}", "nn", (N_DEV, S // tm_up), 1,
            h, pl.BlockSpec((tm_up, D), lambda j, i: (i, 0)),
            wg[l][2], pl.BlockSpec((None, D, FB), lambda j, i: (j, 0, 0)),
            [_sds((S, F), BF16), _sds((S, F), BF16)], [blk, blk], None, epilogue="relu2",
            extras=[mlp_b1[l].reshape(1, F)], extra_specs=[pl.BlockSpec((1, FB), lambda j, i: (0, j))], deps=deps)

    tn_q = _tile(D, 512)

    def mlp_down(l, act):
        return _matmul(
            f"mlp_down_{l}", "nn", (D // tn_q, S // tm), 1,
            act, pl.BlockSpec((tm, F), lambda n, i: (i, 0)),
            wg[l][3], pl.BlockSpec((F, tn_q), lambda n, i: (0, n)),
            [_sds((S, D), F32)], [pl.BlockSpec((tm, tn_q), lambda n, i: (i, n))], None,
            extras=[mlp_b2[l].reshape(1, D)], extra_specs=[pl.BlockSpec((1, tn_q), lambda n, i: (0, n))])[0]

    xin, ys, hs, saved = [xs], [], [], []
    sh0, sc0, _ = mod_vecs(0)
    cast_done = [in_flight[1][3]] + [buf for k in sorted(shards) for buf in shards[k]]
    h = _modulate("modulate_in", xs, sc0, sh0, cast_done)
    for k in range(K):
        l = k // 2
        hs.append(h)
        fetch_group(k, h)
        started = [in_flight[k + 2][3]] if k + 2 < K else []
        if k % 2 == 0 and l % 2 == 0:
            qkv3 = in_proj(l, h, BF16, started)
            o, totals = _attn_fwd(f"attn_fwd_{l}", qkv3)
            if k == 0:
                fetch_first_out(o)
            y = out_proj(l, o)
            saved.append((qkv3, o, totals))
        elif k % 2 == 0:
            bcu3 = in_proj(l, h, F32, started)
            j = l // 2
            p = _conv_fwd(f"conv_fwd_{l}", bcu3, cw_full[j], cb_full[j].reshape(1, D))
            y = out_proj(l, p)
            saved.append((bcu3, p))
        else:
            u, act = mlp_up(l, h, started)
            y = mlp_down(l, act)
            saved.append((u, act))
        ys.append(y)
        if k + 1 < K:
            _, _, gate = mod_vecs(k)
            sh_n, sc_n, _ = mod_vecs(k + 1)
            x_next, h = _ln_fwd(f"ln_fwd_{k}", xin[k], y, gate, lng_full[l, k % 2].reshape(1, D),
                                lnb_full[l, k % 2].reshape(1, D), sc_n, sh_n)
            xin.append(x_next)

    tm_g = _tile(D, 512)

    def grad_rows(name, a, dy, rows):
        return _matmul(
            name, "tn", (rows // tm_g, D // tn_half), 1,
            a, pl.BlockSpec((S, tm_g), lambda m, n: (0, m)),
            dy, pl.BlockSpec((S, tn_half), lambda m, n: (0, n)),
            [_sds((rows, D), BF16)], [pl.BlockSpec((tm_g, tn_half), lambda m, n: (m, n))], None)[0]

    def back_out_proj(l, dy, out_dtype):
        return _matmul(
            f"back_out_proj_{l}", "nt", (D // tn_half, S // tm), 1,
            dy, pl.BlockSpec((tm, D), lambda n, i: (i, 0)),
            wg[l][1], pl.BlockSpec((tn_half, D), lambda n, i: (n, 0)),
            [_sds((S, D), out_dtype)], [pl.BlockSpec((tm, tn_half), lambda n, i: (i, n))], None)[0]

    def back_in_proj(l, d3, deps):
        def part(p):
            return pl.BlockSpec((None, tm, D), lambda i, n: (p, i, 0))
        return _matmul(
            f"back_in_proj_{l}", "nt", (S // tm, D // tn_q), 1,
            d3, part(0),
            _unblock(wg[l][0], f"unblock_win_{l}"), pl.BlockSpec((tn_q, 3 * D), lambda i, n: (n, 0)),
            [_sds((S, D), F32)], [pl.BlockSpec((tm, tn_q), lambda i, n: (i, n))], None,
            extras=[d3, d3], extra_specs=[part(1), part(2)], widen="a", deps=deps)[0]

    def grad_in_proj(l, h, d3):
        def tile(r):
            return pl.BlockSpec((None, S, CT), lambda j, m: ((3 * j + r) // N_DEV, 0, (3 * j + r) % N_DEV))
        return _matmul(
            f"grad_in_proj_{l}", "tn", (N_DEV, D // tm_g), 1,
            h, pl.BlockSpec((S, tm_g), lambda j, m: (0, m)),
            d3, tile(0),
            [_sds((N_DEV, D, 3 * CT), BF16)],
            [pl.BlockSpec((None, tm_g, 3 * CT), lambda j, m: (j, m, 0))], None,
            extras=[d3, d3], extra_specs=[tile(1), tile(2)], widen="b")[0]

    def back_mlp_down(l, dy, u):
        blk = pl.BlockSpec((tm_up, FB), lambda j, i: (i, j))
        return _matmul(
            f"back_mlp_down_{l}", "nt", (N_DEV, S // tm_up), 1,
            dy, pl.BlockSpec((tm_up, D), lambda j, i: (i, 0)),
            wg[l][3], pl.BlockSpec((FB, D), lambda j, i: (j, 0)),
            [_sds((S, F), BF16), _sds((1, F), F32)], [blk, pl.BlockSpec((1, FB), lambda j, i: (0, j))], None,
            epilogue="drelu2", extras=[u], extra_specs=[blk], sem=("parallel", "arbitrary"))

    def back_mlp_up(l, du, deps):
        return _matmul(
            f"back_mlp_up_{l}", "nt", (S // tm, D // tn_q), 1,
            du, pl.BlockSpec((tm, F), lambda i, n: (i, 0)),
            _unblock(wg[l][2], f"unblock_w1_{l}"), pl.BlockSpec((tn_q, F), lambda i, n: (n, 0)),
            [_sds((S, D), F32)], [pl.BlockSpec((tm, tn_q), lambda i, n: (i, n))], None, deps=deps)[0]

    def grad_mlp_up(l, h, du):
        return _matmul(
            f"grad_mlp_up_{l}", "tn", (N_DEV, D // tm_g), 1,
            h, pl.BlockSpec((S, tm_g), lambda j, m: (0, m)),
            du, pl.BlockSpec((S, FB), lambda j, m: (0, j)),
            [_sds((N_DEV, D, FB), BF16)], [pl.BlockSpec((None, tm_g, FB), lambda j, m: (j, m, 0))], None)[0]

    dmod = [[None] * 6 for _ in range(L)]
    dlng = [[None, None] for _ in range(L)]
    dlnb = [[None, None] for _ in range(L)]
    db1, db2 = [None] * L, [None] * L
    dcw, dcb = [None] * LC, [None] * LC

    carries = {"attn_in": None, "attn_out": None, "conv_in": None, "conv_out": None, "w1": None, "w2": None}
    moments = {"attn_in": (attn_w_qkv, m_attn_w_qkv, v_attn_w_qkv), "attn_out": (attn_w_o, m_attn_w_o, v_attn_w_o),
               "conv_in": (conv_w_in, m_conv_w_in, v_conv_w_in), "conv_out": (conv_w_out, m_conv_w_out, v_conv_w_out),
               "w1": (mlp_w1, m_mlp_w1, v_mlp_w1), "w2": (mlp_w2, m_mlp_w2, v_mlp_w2)}
    pending = []

    def finish_pending(after):
        if not pending:
            return
        k, started = pending.pop()
        ps, r2 = _rs_chips_wait(started, after, f"rs_wait_{k}")
        l = k // 2
        if k % 2 == 1:
            keys = [("w1", l), ("w2", l)]
        else:
            kind = "attn" if l % 2 == 0 else "conv"
            keys = [(kind + "_in", l // 2), (kind + "_out", l // 2)]
        for t, (key, idx) in enumerate(keys):
            w3, m3, v3 = moments[key]
            parts = [(ps[t], 0), (r2[t], 1), (r2[t], 2), (r2[t], 3)]
            carries[key] = _adamw(f"adamw_{key}_{l}", w3, m3, v3, idx, parts, carries[key])

    def reduce_begin(k, gs):
        return _rs_sibling_start(gs, f"rs_sibling_start_{k}")

    def reduce_send(k, exchange, after):
        finish_pending(after)
        gs, r1 = _rs_sibling_wait(exchange, after, f"rs_sibling_wait_{k}")
        ps = [_rs_add(g, r, me_arr, f"rs_add_{k}_{t}") for t, (g, r) in enumerate(zip(gs, r1))]
        started = _rs_chips_start(ps, f"rs_start_{k}")
        pending.append((k, started))
        return started[4]

    def take_ln_rows(k, vec):
        l, o = k // 2, 3 * (k % 2)
        dlng[l][k % 2], dlnb[l][k % 2] = vec[2], vec[3]
        dmod[l][o + 2] = vec[4]
        if k % 2 == 1:
            db2[l] = vec[5]

    def take_mod_rows(k, vec):
        l, o = k // 2, 3 * (k % 2)
        dmod[l][o + 1], dmod[l][o] = vec[0], vec[1]

    lk = K - 1
    _, _, gate = mod_vecs(lk)
    dy, dxr, vec = _boundary(f"boundary_{lk}", "last", (target, xin[lk], ys[lk]),
                             (gate, lng_full[lk // 2, 1].reshape(1, D), lnb_full[lk // 2, 1].reshape(1, D)))
    take_ln_rows(lk, vec)
    loss_local = 0.5 * vec[6, 0] / D
    grad_x = None
    for k in range(K - 1, -1, -1):
        l = k // 2
        if k % 2 == 1:
            u, act = saved[k]
            g_w2 = grad_rows(f"grad_mlp_down_{l}", act, dy, F).reshape(N_DEV, FB, D)
            du, db1_l = back_mlp_down(l, dy, u)
            db1[l] = db1_l[0]
            g_w1 = grad_mlp_up(l, hs[k], du)
            exchange = reduce_begin(k, [g_w1, g_w2])
            dh = back_mlp_up(l, du, [exchange[4]])
            token = reduce_send(k, exchange, dh)
        else:
            if l % 2 == 0:
                qkv3, o, totals = saved[k]
                g_out = grad_rows(f"grad_out_proj_{l}", o, dy, D).reshape(N_DEV, DB, D)
                do = back_out_proj(l, dy, BF16)
                d3 = _attn_bwd(f"attn_bwd_{l}", qkv3, do, totals)
            else:
                bcu3, p = saved[k]
                j = l // 2
                g_out = grad_rows(f"grad_out_proj_{l}", p, dy, D).reshape(N_DEV, DB, D)
                dp = back_out_proj(l, dy, F32)
                d3, cvec = _conv_bwd(f"conv_bwd_{l}", bcu3, dp, cw_full[j], cb_full[j].reshape(1, D))
                dcb[j], dcw[j] = cvec[0], cvec[1:4]
            g_in = grad_in_proj(l, hs[k], d3)
            exchange = reduce_begin(k, [g_in, g_out])
            if k == 0:
                token = reduce_send(k, exchange, d3)
                dh = back_in_proj(l, d3, [token])
            else:
                dh = back_in_proj(l, d3, [exchange[4]])
                token = reduce_send(k, exchange, dh)
        sh_k, sc_k, _ = mod_vecs(k)
        if k > 0:
            kp = k - 1
            _, _, gate = mod_vecs(kp)
            dy, dxr, vec = _boundary(
                f"boundary_{kp}", "mid", (dxr, dh, xin[kp], ys[kp]),
                (sc_k, gate, lng_full[kp // 2, kp % 2].reshape(1, D), lnb_full[kp // 2, kp % 2].reshape(1, D)),
                deps=[token])
            take_mod_rows(k, vec)
            take_ln_rows(kp, vec)
        else:
            grad_x, vec = _boundary("boundary_in", "first", (dxr, dh, xs), (sc_k,), deps=[token])
            take_mod_rows(k, vec)
    finish_pending(grad_x)
    results = carries

    flat = [jnp.stack([jnp.stack(r) for r in dmod]).reshape(-1), jnp.stack(db1).reshape(-1),
            jnp.stack(db2).reshape(-1), jnp.stack([jnp.stack(r) for r in dlng]).reshape(-1),
            jnp.stack([jnp.stack(r) for r in dlnb]).reshape(-1), jnp.stack(dcw).reshape(-1),
            jnp.stack(dcb).reshape(-1)]
    sizes = [a.size for a in flat]
    small_g = _allgather([jnp.concatenate(flat).reshape(-1, LANES)], "ag_small_grads")[0]
    total = _sum_devices(small_g, "sum_small_grads").reshape(-1)
    starts = [sum(sizes[:i]) for i in range(len(sizes))]

    def piece(i, shape):
        return total[starts[i]:starts[i] + sizes[i]].reshape(shape)

    def my_block(a):
        return lax.dynamic_slice_in_dim(a, me * DB, DB, axis=a.ndim - 1)

    g_mod_b = piece(0, (L, 6 * D))
    g_b1, g_b2 = piece(1, (L, F)), piece(2, (L, D))
    g_lng, g_lnb = my_block(piece(3, (L, 2, D))), my_block(piece(4, (L, 2, D)))
    g_cw, g_cb = my_block(piece(5, (LC, 3, D))), my_block(piece(6, (LC, D)))

    dmod_all = small_g.reshape(N_DEV, -1)[:, :sizes[0]].reshape(N_DEV, L, 6 * D)
    dmod_mine = jnp.moveaxis(lax.dynamic_slice_in_dim(dmod_all, me * NC, NC, axis=2), 0, 1)
    dmod16 = jnp.pad(dmod_mine, ((0, 0), (0, 16 - N_DEV), (0, 0)))
    tm_mod = _tile(D, 512)
    g_mod_w = _matmul(
        "grad_mod_w", "tn", (L, D // tm_mod), 1,
        cond16, pl.BlockSpec((16, tm_mod), lambda l, m: (0, m)),
        dmod16, pl.BlockSpec((None, 16, NC), lambda l, m: (l, 0, 0)),
        [_sds((L, D, NC), F32)], [pl.BlockSpec((None, tm_mod, NC), lambda l, m: (l, m, 0))], None)[0]
    res_mod_w = None
    for l in range(L):
        res_mod_w = _adamw(f"adamw_mod_w_{l}", mod_w, m_mod_w, v_mod_w, l, [(g_mod_w, l)], res_mod_w)

    small_w = [mod_b, ln_g, ln_b, conv_w, conv_b, mlp_b1, mlp_b2]
    small_m = [m_mod_b, m_ln_g, m_ln_b, m_conv_w, m_conv_b, m_mlp_b1, m_mlp_b2]
    small_v = [v_mod_b, v_ln_g, v_ln_b, v_conv_w, v_conv_b, v_mlp_b1, v_mlp_b2]
    small_gr = [g_mod_b, g_lng, g_lnb, g_cw, g_cb, g_b1, g_b2]
    n_small = sum(a.size for a in small_w)
    pad = (-n_small) % (8 * LANES)

    def packed(arrs):
        v = jnp.concatenate([a.reshape(-1) for a in arrs])
        return jnp.pad(v, (0, pad)).reshape(1, -1, LANES)

    sg, sd, sm, sv = _adamw("adamw_small", packed(small_w), packed(small_m), packed(small_v), 0,
                            [(packed(small_gr), 0)])

    def unpacked(buf):
        out, pos = [], 0
        flat_buf = buf.reshape(-1)
        for a in small_w:
            out.append(flat_buf[pos:pos + a.size].reshape(a.shape))
            pos += a.size
        return out

    small_res = [unpacked(b) for b in (sg, sd, sm, sv)]
    loss = lax.psum(loss_local, ("x", "y", "c"))

    def leaf(kind):
        s = small_res[kind]
        return [res_mod_w[kind], s[0], s[1], s[2], results["attn_in"][kind], results["attn_out"][kind],
                results["conv_in"][kind], s[3], s[4], results["conv_out"][kind], results["w1"][kind], s[5],
                results["w2"][kind], s[6]]

    return (loss, grad_x[None], *leaf(0), *leaf(1), *leaf(2), *leaf(3))
```
